```python
import functools
import jax
import jax.numpy as jnp
from jax import lax
import numpy as np

D_MODEL = 1024
BATCH = 1
SEQ = 16384
DEPTH = 1
DEC_BATCH = 32
DEC_SEQ = 4
PAST_LEN = 16384
PAGE_SIZE = 128

D_HEAD = 64
H_MOBA = 8
H_NSA = 8
G_NSA = 2
R_NSA = H_NSA // G_NSA
MOBA_BLOCK = 256
MOBA_TOPK = 3
CMP_STRIDE = 16
CMP_LEN = 2 * CMP_STRIDE
CMP_HIDDEN = 2 * D_HEAD
SLC_BLOCK = 64
SLC_TOPK = 16
SLC_INIT = 1
SLC_LOCAL = 2
WINDOW = 512
N_EXPERTS = 256
TOPK_EXPERTS = 8
N_EXPERT_GROUPS = 8
TOPK_GROUPS = 4
D_EXPERT = D_MODEL // 4
D_SHARED = D_MODEL // 4
ROUTED_SCALE = 2.5
QBLK = 128
EPS = 1e-6
COL_QA = H_MOBA * D_HEAD
COL_KVA = 2 * H_MOBA * D_HEAD
COL_QB = H_NSA * D_HEAD
COL_KVB = 4 * G_NSA * D_HEAD
COL_KVW = 2 * G_NSA * D_HEAD
COL_GB = 3 * H_NSA
D_PROJ = COL_QA + COL_KVA + COL_QB + COL_KVB + COL_KVW + COL_GB
D_MIX = (H_MOBA + H_NSA) * D_HEAD

kernel_name = 'hybrid_moba_nsa_moe_step'


def rmsnorm(x, g):
    xf = x.astype(jnp.float32)
    y = xf * lax.rsqrt(jnp.mean(xf * xf, axis=-1, keepdims=True) + EPS)
    return (y * g.astype(jnp.float32)).astype(x.dtype)


def masked_softmax(s, valid):
    s = jnp.where(valid, s, -jnp.inf)
    m = jnp.max(s, axis=-1, keepdims=True)
    p = jnp.exp(s - jnp.where(jnp.isfinite(m), m, 0.0))
    return p / jnp.maximum(jnp.sum(p, axis=-1, keepdims=True), 1e-30)


def alibi_slopes():
    n = H_MOBA + H_NSA
    m = 2.0 ** (-8.0 * jnp.arange(1, n + 1, dtype=jnp.float32) / n)
    return m[0::2][:H_MOBA], m[1::2][:H_NSA].reshape(G_NSA, R_NSA)


def _index_grid(n_batch, h0, nh, nd):
    bi = jnp.arange(n_batch).reshape((n_batch,) + (1,) * (nd - 1))
    hi = (h0 + jnp.arange(nh)).reshape((1, nh) + (1,) * (nd - 2))
    return bi, hi


def take_rows(x, pos, h0, nh):
    bi, hi = _index_grid(x.shape[0], h0, nh, pos.ndim)
    return x[bi, jnp.clip(pos, 0, x.shape[1] - 1), hi]


def take_paged(pool, layer, page_table, new, pos, h0, nh):
    past = page_table.shape[1] * PAGE_SIZE
    bi, hi = _index_grid(page_table.shape[0], h0, nh, pos.ndim)
    pp = jnp.clip(pos, 0, past - 1)
    old = pool[layer, page_table[bi, pp // PAGE_SIZE], pp % PAGE_SIZE, hi]
    cur = take_rows(new, pos - past, h0, nh)
    return jnp.where((pos >= past)[..., None], cur, old)


def moba_block_means(fetch_a, length):
    nb = -(-length // MOBA_BLOCK)
    def one(j):
        k = fetch_a((j * MOBA_BLOCK + jnp.arange(MOBA_BLOCK))[None, None], 0, H_MOBA)
        return jnp.mean(k.astype(jnp.float32), axis=2)
    return jnp.moveaxis(lax.map(one, jnp.arange(nb)), 0, 2)


def moba_attend(q, pos, kmean, fetch_a, slopes):
    b, tb = q.shape[:2]
    nb = kmean.shape[2]
    qh = q.transpose(0, 2, 1, 3)
    own = pos // MOBA_BLOCK
    gate = jnp.einsum('bhqd,bhnd->bhqn', qh, kmean, preferred_element_type=jnp.float32)
    gate = jnp.where(jnp.arange(nb)[None, :] < own[:, None], gate, -jnp.inf)
    gv, gi = lax.top_k(gate, min(MOBA_TOPK, nb))
    blocks = jnp.concatenate([gi, jnp.broadcast_to(own[None, None, :, None], gi.shape[:3] + (1,))], axis=-1)
    ok = jnp.concatenate([jnp.isfinite(gv), jnp.ones(gi.shape[:3] + (1,), bool)], axis=-1)
    kpos = (blocks[..., None] * MOBA_BLOCK + jnp.arange(MOBA_BLOCK)).reshape(b, H_MOBA, tb, -1)
    valid = jnp.repeat(ok, MOBA_BLOCK, axis=-1) & (kpos <= pos[:, None])
    k = fetch_a(kpos, 0, H_MOBA)
    v = fetch_a(kpos, H_MOBA, H_MOBA)
    dist = (pos[:, None] - kpos).astype(jnp.float32)
    s = jnp.einsum('bhqd,bhqkd->bhqk', qh, k, preferred_element_type=jnp.float32) * D_HEAD ** -0.5 - slopes[:, None, None] * dist
    p = masked_softmax(s, valid)
    o = jnp.einsum('bhqk,bhqkd->bhqd', p.astype(v.dtype), v)
    return o.transpose(0, 2, 1, 3).reshape(b, tb, H_MOBA * D_HEAD)


def nsa_compress(fetch_b, length, cmp_pe, cmp_w1, cmp_b1, cmp_w2, cmp_b2):
    n_chunks = length // CMP_STRIDE
    pos = jnp.arange(n_chunks * CMP_STRIDE)[None, None]
    half = CMP_STRIDE * D_HEAD
    out = []
    for j in range(2):
        rows = fetch_b(pos, j * G_NSA, G_NSA)
        chunks = rows.reshape(rows.shape[0], G_NSA, n_chunks, half)
        w1 = cmp_w1[j]
        hid = chunks[:, :, :-1] @ w1[:half] + chunks[:, :, 1:] @ w1[half:] + cmp_pe[j].reshape(-1) @ w1 + cmp_b1[j]
        out.append(jax.nn.gelu(hid) @ cmp_w2[j] + cmp_b2[j])
    return out[0], out[1]


def nsa_attend(q, g, pos, kc, vc, fetch_b, fetch_w, length, slopes):
    b, tb = q.shape[:2]
    scale = D_HEAD ** -0.5
    sl = slopes[:, :, None, None]
    qg = q.reshape(b, tb, G_NSA, R_NSA, D_HEAD).transpose(0, 2, 3, 1, 4)
    nc = kc.shape[2]
    cpos = jnp.arange(nc) * CMP_STRIDE + CMP_LEN - 1
    dist_c = (pos[:, None] - cpos[None, :]).astype(jnp.float32)
    s_c = jnp.einsum('bgrqd,bgnd->bgrqn', qg, kc, preferred_element_type=jnp.float32) * scale - sl * dist_c
    p_c = masked_softmax(s_c, dist_c >= 0)
    o_c = jnp.einsum('bgrqn,bgnd->bgrqd', p_c.astype(vc.dtype), vc)
    ns = -(-length // SLC_BLOCK)
    ratio = SLC_BLOCK // CMP_STRIDE
    lr = CMP_LEN // CMP_STRIDE
    offs = np.array([m - n for m in range(ratio) for n in range(lr)])
    ci = ratio * np.arange(ns)[:, None] + offs[None, :]
    imp = jnp.sum(p_c, axis=2)
    imp = jnp.where((ci >= 0) & (ci < nc), imp[..., np.clip(ci, 0, nc - 1)], 0.0).sum(-1)
    blk = jnp.arange(ns)[None, :]
    qb = (pos // SLC_BLOCK)[:, None]
    forced = (blk < SLC_INIT) | ((qb - blk >= 0) & (qb - blk < SLC_LOCAL))
    score = jnp.where(forced, jnp.inf, jnp.where(blk <= qb, imp, -jnp.inf))
    sv, si = lax.top_k(score, min(SLC_TOPK, ns))
    kpos = (si[..., None] * SLC_BLOCK + jnp.arange(SLC_BLOCK)).reshape(b, G_NSA, tb, -1)
    valid_s = jnp.repeat(sv > -jnp.inf, SLC_BLOCK, axis=-1) & (kpos <= pos[:, None])
    ks = fetch_b(kpos, 2 * G_NSA, G_NSA)
    vs = fetch_b(kpos, 3 * G_NSA, G_NSA)
    dist_s = (pos[:, None] - kpos).astype(jnp.float32)[:, :, None]
    s_s = jnp.einsum('bgrqd,bgqkd->bgrqk', qg, ks, preferred_element_type=jnp.float32) * scale - sl * dist_s
    p_s = masked_softmax(s_s, valid_s[:, :, None])
    o_s = jnp.einsum('bgrqk,bgqkd->bgrqd', p_s.astype(vs.dtype), vs)
    kwpos = pos[0] - WINDOW + jnp.arange(WINDOW + tb)
    dist_w = pos[:, None] - kwpos[None, :]
    valid_w = (dist_w >= 0) & (dist_w <= WINDOW) & (kwpos[None, :] >= 0)
    kw = fetch_w(kwpos[None, None], 0, G_NSA)
    vw = fetch_w(kwpos[None, None], G_NSA, G_NSA)
    s_w = jnp.einsum('bgrqd,bgkd->bgrqk', qg, kw, preferred_element_type=jnp.float32) * scale - sl * dist_w.astype(jnp.float32)
    p_w = masked_softmax(s_w, valid_w)
    o_w = jnp.einsum('bgrqk,bgkd->bgrqd', p_w.astype(vw.dtype), vw)
    gate = jax.nn.sigmoid(g.astype(jnp.float32)).reshape(b, tb, G_NSA, R_NSA, 3).transpose(0, 2, 3, 1, 4)
    o = gate[..., 0:1] * o_c + gate[..., 1:2] * o_s + gate[..., 2:3] * o_w
    return o.transpose(0, 3, 1, 2, 4).reshape(b, tb, H_NSA * D_HEAD).astype(q.dtype)


def sweep_queries(block_fn, q_a, q_b, g_b, p0):
    b, t = q_a.shape[:2]
    if t > QBLK and t % QBLK == 0:
        n = t // QBLK
        blk = lambda a: jnp.moveaxis(a.reshape((b, n, QBLK) + a.shape[2:]), 1, 0)
        starts = p0 + QBLK * jnp.arange(n, dtype=jnp.int32)
        o_a, o_b = lax.map(lambda xs: block_fn(*xs), (blk(q_a), blk(q_b), blk(g_b), starts))
        unblk = lambda o: jnp.moveaxis(o, 0, 1).reshape((b, t) + o.shape[3:])
        return unblk(o_a), unblk(o_b)
    return block_fn(q_a, q_b, g_b, jnp.asarray(p0, jnp.int32))


def swiglu(x, w_up, w_down):
    a, g = jnp.split(x @ w_up, 2, axis=-1)
    return (jax.nn.silu(a) * g) @ w_down


def routed_experts(h, idx, wts, w_up, w_down):
    m, d = h.shape
    e = w_up.shape[0]
    a = idx.size
    r = max(a // e, 1)
    rb = min(max(1 << (r.bit_length() - 1), 8), 128)
    nblk = -(-(a + e * (rb - 1)) // rb)
    e_flat = idx.reshape(-1)
    order = jnp.argsort(e_flat)
    e_sorted = e_flat[order]
    tok_sorted = (order // idx.shape[1]).astype(jnp.int32)
    w_sorted = wts.reshape(-1)[order]
    counts = jnp.bincount(e_flat, length=e)
    padded = (counts + rb - 1) // rb * rb
    p_end = jnp.cumsum(padded)
    p_start = p_end - padded
    c_start = jnp.cumsum(counts) - counts
    dest = p_start[e_sorted] + jnp.arange(a) - c_start[e_sorted]
    row_tok = jnp.full((nblk * rb,), m, jnp.int32).at[dest].set(tok_sorted)
    row_w = jnp.zeros((nblk * rb,), h.dtype).at[dest].set(w_sorted.astype(h.dtype))
    blk_e = jnp.clip(jnp.searchsorted(p_end, jnp.arange(nblk) * rb, side='right'), 0, e - 1)
    h_pad = jnp.concatenate([h, jnp.zeros((1, d), h.dtype)], axis=0)
    def expert_block(args):
        tok, w, ex = args
        return swiglu(h_pad[tok], w_up[ex], w_down[ex]) * w[:, None]
    y = lax.map(expert_block, (row_tok.reshape(nblk, rb), row_w.reshape(nblk, rb), blk_e))
    return jax.ops.segment_sum(y.reshape(nblk * rb, d), row_tok, num_segments=m + 1)[:m]


def moe_ffn(h, w_router, b_router, w_exp_up, w_exp_down, w_sh_up, w_sh_down):
    m = h.shape[0]
    s = jax.nn.sigmoid(jnp.dot(h, w_router, preferred_element_type=jnp.float32))
    sel = s + b_router.astype(jnp.float32)
    per = N_EXPERTS // N_EXPERT_GROUPS
    gscore = lax.top_k(sel.reshape(m, N_EXPERT_GROUPS, per), 2)[0].sum(-1)
    _, gi = lax.top_k(gscore, TOPK_GROUPS)
    gmask = jnp.any(jnp.arange(N_EXPERT_GROUPS)[None, None, :] == gi[:, :, None], axis=1)
    emask = jnp.repeat(gmask, per, axis=1)
    _, idx = lax.top_k(jnp.where(emask, sel, -jnp.inf), TOPK_EXPERTS)
    w = jnp.take_along_axis(s, idx, axis=1)
    w = w / jnp.sum(w, axis=-1, keepdims=True) * ROUTED_SCALE
    return routed_experts(h, idx, w, w_exp_up, w_exp_down) + swiglu(h, w_sh_up, w_sh_down)


def decoder_layer(x, c, p0, make_sources, w_ada, b_ada, g_mix_pre, g_mix_post, g_ffn_pre, g_ffn_post, w_in, g_out_moba, g_out_nsa, w_out, cmp_pe, cmp_w1, cmp_b1, cmp_w2, cmp_b2, w_router, b_router, w_exp_up, w_exp_down, w_sh_up, w_sh_down):
    b, t, d = x.shape
    mod = (jax.nn.silu(c) @ w_ada + b_ada).astype(x.dtype).reshape(b, 6, 1, d)
    shift1, scale1, gate1, shift2, scale2, gate2 = (mod[:, i] for i in range(6))
    h = rmsnorm(x, g_mix_pre) * (1 + scale1) + shift1
    proj = h @ w_in
    cuts = np.cumsum([COL_QA, COL_KVA, COL_QB, COL_KVB, COL_KVW]).tolist()
    q_a, kv_a, q_b, kv_b, kv_w, g_b = jnp.split(proj, cuts, axis=-1)
    q_a = q_a.reshape(b, t, H_MOBA, D_HEAD)
    kv_a = kv_a.reshape(b, t, 2 * H_MOBA, D_HEAD)
    q_b = q_b.reshape(b, t, H_NSA, D_HEAD)
    kv_b = kv_b.reshape(b, t, 4 * G_NSA, D_HEAD)
    kv_w = kv_w.reshape(b, t, 2 * G_NSA, D_HEAD)
    g_b = g_b.reshape(b, t, H_NSA, 3)
    fetch_a, fetch_b, fetch_w, length, new_state = make_sources(kv_a, kv_b, kv_w)
    slopes_a, slopes_b = alibi_slopes()
    kmean = moba_block_means(fetch_a, length)
    kc, vc = nsa_compress(fetch_b, length, cmp_pe, cmp_w1, cmp_b1, cmp_w2, cmp_b2)
    def block_fn(qa, qb, gb, start):
        pos = start + jnp.arange(qa.shape[1], dtype=jnp.int32)
        return (moba_attend(qa, pos, kmean, fetch_a, slopes_a), nsa_attend(qb, gb, pos, kc, vc, fetch_b, fetch_w, length, slopes_b))
    o_a, o_b = sweep_queries(block_fn, q_a, q_b, g_b, p0)
    o = jnp.concatenate([rmsnorm(o_a, g_out_moba), rmsnorm(o_b, g_out_nsa)], axis=-1) @ w_out
    x = x + gate1 * rmsnorm(o, g_mix_post)
    h = rmsnorm(x, g_ffn_pre) * (1 + scale2) + shift2
    f = moe_ffn(h.reshape(b * t, d), w_router, b_router, w_exp_up, w_exp_down, w_sh_up, w_sh_down).reshape(b, t, d)
    x = x + gate2 * rmsnorm(f, g_ffn_post)
    return x, new_state


def setup_inputs(seed: int = 0) -> dict:
    key = jax.random.key(seed)
    ks = jax.random.split(key, 32)
    def nrm(i, shape, scale=1.0):
        return scale * jax.random.normal(ks[i], shape, jnp.float32)
    n_pages = PAST_LEN // PAGE_SIZE
    n_used = DEC_BATCH * n_pages
    n_pool = n_used + (n_used + 3) // 4
    win_len = min(WINDOW, PAST_LEN)
    page_table = jax.random.permutation(ks[5], n_pool)[:n_used].reshape(DEC_BATCH, n_pages).astype(jnp.int32)
    L = DEPTH
    return {
        'x_prompt': nrm(0, (BATCH, SEQ, D_MODEL)),
        'x_sample': nrm(1, (DEC_BATCH, DEC_SEQ, D_MODEL)),
        'cache_moba': nrm(2, (L, n_pool, PAGE_SIZE, 2 * H_MOBA, D_HEAD)),
        'cache_nsa': nrm(3, (L, n_pool, PAGE_SIZE, 4 * G_NSA, D_HEAD)),
        'state_nsa_win': nrm(4, (L, DEC_BATCH, win_len, 2 * G_NSA, D_HEAD)),
        'page_table': page_table,
        'c_prompt': nrm(6, (BATCH, D_MODEL)),
        'c_sample': nrm(7, (DEC_BATCH, D_MODEL)),
        'w_ada': nrm(8, (L, D_MODEL, 6 * D_MODEL), 0.5 * D_MODEL ** -0.5),
        'b_ada': nrm(9, (L, 6 * D_MODEL), 0.02),
        'g_mix_pre': 1.0 + nrm(10, (L, D_MODEL), 0.05),
        'g_mix_post': 1.0 + nrm(11, (L, D_MODEL), 0.05),
        'g_ffn_pre': 1.0 + nrm(12, (L, D_MODEL), 0.05),
        'g_ffn_post': 1.0 + nrm(13, (L, D_MODEL), 0.05),
        'w_in': nrm(14, (L, D_MODEL, D_PROJ), D_MODEL ** -0.5),
        'g_out_moba': 1.0 + nrm(15, (L, H_MOBA * D_HEAD), 0.05),
        'g_out_nsa': 1.0 + nrm(16, (L, H_NSA * D_HEAD), 0.05),
        'w_out': nrm(17, (L, D_MIX, D_MODEL), D_MIX ** -0.5),
        'cmp_pe': nrm(18, (L, 2, CMP_LEN, D_HEAD), 0.5),
        'cmp_w1': nrm(19, (L, 2, CMP_LEN * D_HEAD, CMP_HIDDEN), (CMP_LEN * D_HEAD) ** -0.5),
        'cmp_b1': nrm(20, (L, 2, CMP_HIDDEN), 0.02),
        'cmp_w2': nrm(21, (L, 2, CMP_HIDDEN, D_HEAD), CMP_HIDDEN ** -0.5),
        'cmp_b2': nrm(22, (L, 2, D_HEAD), 0.02),
        'w_router': nrm(23, (L, D_MODEL, N_EXPERTS), D_MODEL ** -0.5),
        'b_router': nrm(24, (L, N_EXPERTS), 0.01),
        'w_exp_up': nrm(25, (L, N_EXPERTS, D_MODEL, 2 * D_EXPERT), D_MODEL ** -0.5),
        'w_exp_down': nrm(26, (L, N_EXPERTS, D_EXPERT, D_MODEL), D_EXPERT ** -0.5),
        'w_sh_up': nrm(27, (L, D_MODEL, 2 * D_SHARED), D_MODEL ** -0.5),
        'w_sh_down': nrm(28, (L, D_SHARED, D_MODEL), D_SHARED ** -0.5),
    }


def reference(x_prompt, x_sample, cache_moba, cache_nsa, state_nsa_win, page_table, c_prompt, c_sample, w_ada, b_ada, g_mix_pre, g_mix_post, g_ffn_pre, g_ffn_post, w_in, g_out_moba, g_out_nsa, w_out, cmp_pe, cmp_w1, cmp_b1, cmp_w2, cmp_b2, w_router, b_router, w_exp_up, w_exp_down, w_sh_up, w_sh_down):
    weights = (w_ada, b_ada, g_mix_pre, g_mix_post, g_ffn_pre, g_ffn_post, w_in, g_out_moba, g_out_nsa, w_out, cmp_pe, cmp_w1, cmp_b1, cmp_w2, cmp_b2, w_router, b_router, w_exp_up, w_exp_down, w_sh_up, w_sh_down)
    past = page_table.shape[1] * PAGE_SIZE
    win_len = state_nsa_win.shape[2]

    def prompt_sources(kv_a, kv_b, kv_w):
        length = kv_a.shape[1]
        return (functools.partial(take_rows, kv_a), functools.partial(take_rows, kv_b), functools.partial(take_rows, kv_w), length, (kv_a, kv_b, kv_w[:, -min(WINDOW, length):]))

    def sample_sources(layer):
        def make(kv_a, kv_b, kv_w):
            buf = jnp.concatenate([state_nsa_win[layer], kv_w], axis=1)
            offset = past - win_len
            fetch_w = lambda pos, h0, nh: take_rows(buf, pos - offset, h0, nh)
            length = past + kv_a.shape[1]
            return (functools.partial(take_paged, cache_moba, layer, page_table, kv_a), functools.partial(take_paged, cache_nsa, layer, page_table, kv_b), fetch_w, length, (kv_a, kv_b, buf[:, -min(WINDOW, length):]))
        return make

    y_p, y_s = x_prompt, x_sample
    st_p, st_s = [], []
    for layer in range(DEPTH):
        w_l = [w[layer] for w in weights]
        y_p, sp = decoder_layer(y_p, c_prompt, 0, prompt_sources, *w_l)
        y_s, ss = decoder_layer(y_s, c_sample, past, sample_sources(layer), *w_l)
        st_p.append(sp)
        st_s.append(ss)
    kv_moba_prompt = jnp.stack([s[0] for s in st_p])
    kv_nsa_prompt = jnp.stack([s[1] for s in st_p])
    win_prompt = jnp.stack([s[2] for s in st_p])
    kv_moba_sample = jnp.stack([s[0] for s in st_s])
    kv_nsa_sample = jnp.stack([s[1] for s in st_s])
    win_sample = jnp.stack([s[2] for s in st_s])
    return (y_p, y_s, kv_moba_prompt, kv_nsa_prompt, win_prompt, kv_moba_sample, kv_nsa_sample, win_sample)
```

```python
import functools

import numpy as np
import jax
import jax.numpy as jnp
from jax import lax
from jax.experimental import pallas as pl
from jax.experimental.pallas import tpu as pltpu

F32, BF16, I32 = jnp.float32, jnp.bfloat16, jnp.int32
NEG_INF = float("-inf")
POS_INF = float("inf")
M_INIT = -1e30

D_HEAD = 64
H_MOBA = 8
H_NSA = 8
G_NSA = 2
R_NSA = H_NSA // G_NSA
MOBA_BLOCK = 256
MOBA_TOPK = 3
CMP_STRIDE = 16
CMP_LEN = 2 * CMP_STRIDE
CMP_HIDDEN = 2 * D_HEAD
SLC_BLOCK = 64
SLC_TOPK = 16
SLC_INIT = 1
SLC_LOCAL = 2
WINDOW = 512
N_EXPERTS = 256
TOPK_EXPERTS = 8
N_EXPERT_GROUPS = 8
TOPK_GROUPS = 4
ROUTED_SCALE = 2.5
PAGE_SIZE = 128
EPS = 1e-6
Q_SCALE = D_HEAD ** -0.5

_SLOPES = [2.0 ** (-8.0 * i / (H_MOBA + H_NSA)) for i in range(1, H_MOBA + H_NSA + 1)]
SLOPES_A = _SLOPES[0::2][:H_MOBA]
SLOPES_B = _SLOPES[1::2][:H_NSA]

LANES = 128
TQ = 256
TK = 256
VMEM_LIMIT_BYTES = 56 * 1024 * 1024
ROW_BLOCK = 128
PAGES_PER_STEP = 16


def _cparams(*sem):
    return pltpu.CompilerParams(dimension_semantics=sem, vmem_limit_bytes=VMEM_LIMIT_BYTES)


def _rms(x, g):
    return x * lax.rsqrt(jnp.mean(x * x, axis=-1, keepdims=True) + EPS) * g


def _split_bf16(a):
    hi = a.astype(BF16)
    lo = (a - hi.astype(F32)).astype(BF16)
    return hi, lo


def _dot3(a, b, dims=None):
    ah, al = _split_bf16(a)
    bh, bl = _split_bf16(b)
    if dims is None:
        d = lambda x, y: jnp.dot(x, y, preferred_element_type=F32)
    else:
        d = lambda x, y: lax.dot_general(x, y, dims, preferred_element_type=F32)
    return d(ah, bh) + (d(ah, bl) + d(al, bh))


_NT = (((1,), (1,)), ((), ()))


def _div_pow2(x, n):
    assert n & (n - 1) == 0
    return jnp.right_shift(x, n.bit_length() - 1)


def _mod_pow2(x, n):
    assert n & (n - 1) == 0
    return jnp.bitwise_and(x, n - 1)


def _head_rows(q64, half):
    z = jnp.zeros_like(q64)
    return jnp.concatenate([z, q64] if half else [q64, z], axis=0)


def _ada_kernel(c_ref, w_ref, b_ref, o_ref):
    a = jax.nn.silu(c_ref[...])
    o_ref[...] = _dot3(a, w_ref[...]) + b_ref[...]


def _ada(c, w_ada, b_ada):
    r, d = c.shape
    n = w_ada.shape[1]
    tn = 768
    return pl.pallas_call(
        _ada_kernel,
        grid=(n // tn,),
        in_specs=[pl.BlockSpec((r, d), lambda j: (0, 0)),
                  pl.BlockSpec((d, tn), lambda j: (0, j)),
                  pl.BlockSpec((1, tn), lambda j: (0, j))],
        out_specs=pl.BlockSpec((r, tn), lambda j: (0, j)),
        out_shape=jax.ShapeDtypeStruct((r, n), F32),
        compiler_params=_cparams("arbitrary"),
        name="ada",
    )(c, w_ada, b_ada.reshape(1, n))


_PROJ_CUTS = (0, 512, 1536, 2048, 2560, 2816, 2944)


def _inproj_kernel(x_ref, sc_ref, sh_ref, g_ref, w_ref, *out_refs):
    h = _rms(x_ref[...], g_ref[...]) * (1.0 + sc_ref[...]) + sh_ref[...]
    hb = h.astype(BF16)
    for o_ref, a, b in zip(out_refs, _PROJ_CUTS[:-1], _PROJ_CUTS[1:]):
        o_ref[...] = jnp.dot(hb, w_ref[:, a:b], preferred_element_type=F32)


def _inproj(x, scale, shift, g, w_pad):
    r, d = x.shape
    tm = min(512, r)
    per_row = scale.shape[0] != 1
    mod_spec = pl.BlockSpec((tm, d), lambda i: (i, 0)) if per_row else pl.BlockSpec((1, d), lambda i: (0, 0))
    widths = [b - a for a, b in zip(_PROJ_CUTS[:-1], _PROJ_CUTS[1:])]
    return pl.pallas_call(
        _inproj_kernel,
        grid=(r // tm,),
        in_specs=[pl.BlockSpec((tm, d), lambda i: (i, 0)), mod_spec, mod_spec,
                  pl.BlockSpec((1, d), lambda i: (0, 0)),
                  pl.BlockSpec(w_pad.shape, lambda i: (0, 0))],
        out_specs=[pl.BlockSpec((tm, w), lambda i: (i, 0)) for w in widths],
        out_shape=[jax.ShapeDtypeStruct((r, w), F32) for w in widths],
        compiler_params=_cparams("arbitrary"),
        name="inproj",
    )(x, scale, shift, g.reshape(1, d), w_pad)


def _topk_axis0(score, k, idx):
    n = score.shape[0]
    sel = jnp.zeros(score.shape, F32)
    picks = []
    for _ in range(k):
        m = jnp.max(score, axis=0, keepdims=True)
        first = jnp.min(jnp.where(score == m, idx, n), axis=0, keepdims=True)
        pick = idx == first
        ok = m > NEG_INF
        sel = jnp.where(pick & ok, 1.0, sel)
        picks.append(jnp.where(ok, first, -1))
        score = jnp.where(pick, NEG_INF, score)
    return sel, picks


def _topk_axis1(score, k, idx):
    n = score.shape[1]
    out = []
    for _ in range(k):
        m = jnp.max(score, axis=1, keepdims=True)
        first = jnp.min(jnp.where(score == m, idx, n), axis=1, keepdims=True)
        out.append((first, m))
        score = jnp.where(idx == first, NEG_INF, score)
    return out


def _kmean_kernel(k_ref, o_ref):
    x = k_ref[...]
    n = x.shape[0] // MOBA_BLOCK
    o_ref[...] = jnp.sum(x.reshape(n, MOBA_BLOCK, x.shape[1]), axis=1) * (1.0 / MOBA_BLOCK)


def _kmean_prompt(kva):
    t = kva.shape[0]
    nb = t // MOBA_BLOCK
    per = 8
    return pl.pallas_call(
        _kmean_kernel,
        grid=(nb // per,),
        in_specs=[pl.BlockSpec((per * MOBA_BLOCK, 512), lambda i: (i, 0))],
        out_specs=pl.BlockSpec((per, 512), lambda i: (i, 0)),
        out_shape=jax.ShapeDtypeStruct((nb, 512), F32),
        compiler_params=_cparams("arbitrary"),
        name="kmean_prompt",
    )(kva)


def _moba_gate_kernel(qT_ref, km_ref, sel_ref, *, nb):
    own = pl.program_id(0)
    blk = lax.broadcasted_iota(I32, (nb, TQ), 0)
    for h in range(H_MOBA):
        p, half = divmod(h, 2)
        qh = _head_rows(qT_ref[h * 64:(h + 1) * 64, :], half)
        g = _dot3(km_ref[:, p * 128:(p + 1) * 128], qh)
        g = jnp.where(blk < own, g, NEG_INF)
        sel, _ = _topk_axis0(g, MOBA_TOPK, blk)
        sel_ref[h] = jnp.where(blk == own, 1.0, sel)


def _moba_gate_prompt(qT, kmean):
    t = qT.shape[1]
    nb = kmean.shape[0]
    return pl.pallas_call(
        functools.partial(_moba_gate_kernel, nb=nb),
        grid=(t // TQ,),
        in_specs=[pl.BlockSpec((512, TQ), lambda i: (0, i)),
                  pl.BlockSpec((nb, 512), lambda i: (0, 0))],
        out_specs=pl.BlockSpec((H_MOBA, nb, TQ), lambda i: (0, 0, i)),
        out_shape=jax.ShapeDtypeStruct((H_MOBA, nb, t), F32),
        compiler_params=_cparams("arbitrary"),
        name="moba_gate",
    )(qT, kmean)


def _flash_head(i, qh, slope, k_ref, vT_ref, sel_ref, set_idx, sub, half):
    kio = lax.broadcasted_iota(I32, (TK, TQ), 0)
    qio = lax.broadcasted_iota(I32, (TK, TQ), 1)
    sc = kio.astype(F32) * slope
    rows = TK // sub

    def step(j, carry, diag):
        m, l, acc = carry
        kj = k_ref[pl.ds(pl.multiple_of(j * TK, TK), TK), :]
        t = jnp.dot(kj, qh, preferred_element_type=F32) + sc
        if diag:
            t = jnp.where(kio <= qio, t, NEG_INF)
        off = slope * ((j - i) * TK).astype(F32)
        if sub == 1:
            selj = sel_ref[set_idx, pl.ds(j, 1), :]
        else:
            per8 = 8 // sub
            blk8 = sel_ref[set_idx, pl.ds(pl.multiple_of((j // per8) * 8, 8), 8), :]
            selj = blk8[0:sub]
            for u in range(1, per8):
                selj = jnp.where(j % per8 == u, blk8[u * sub:(u + 1) * sub], selj)
        selj = selj > 0.0
        t3 = t.reshape(sub, rows, TQ)
        mb = jnp.where(selj, jnp.max(t3, axis=1) + off, M_INIT)
        m_new = jnp.maximum(m, jnp.max(mb, axis=0, keepdims=True))
        mu = jnp.where(selj, m_new - off, POS_INF)
        p = jnp.exp(t3 - mu[:, None, :]).reshape(TK, TQ)
        alpha = jnp.exp(m - m_new)
        l = l * alpha + jnp.sum(p, axis=0, keepdims=True)
        pv = jnp.dot(vT_ref[j], p.astype(BF16), preferred_element_type=F32)
        acc = acc * alpha + pv[half * 64:(half + 1) * 64]
        return m_new, l, acc

    init = (jnp.full((1, TQ), M_INIT, F32), jnp.zeros((1, TQ), F32), jnp.zeros((64, TQ), F32))
    carry = lax.fori_loop(0, i, lambda j, c: step(j, c, False), init)
    m, l, acc = step(i, carry, True)
    return acc / jnp.maximum(l, 1e-30)


def _flash_kernel(slopes_ref, qT_ref, k_ref, vT_ref, sel_ref, o_ref, *, heads_per_step, heads_per_set, sub):
    p = pl.program_id(0)
    i = pl.program_id(1)
    for hh in range(heads_per_step):
        half = (hh // (heads_per_step // 2)) if heads_per_step > 2 else hh
        slope = slopes_ref[p * heads_per_step + hh]
        qh = _head_rows(qT_ref[hh * 64:(hh + 1) * 64, :], half).astype(BF16)
        o = _flash_head(i, qh, slope, k_ref, vT_ref.at[0], sel_ref, hh // heads_per_set, sub, half)
        o_ref[hh * 64:(hh + 1) * 64, :] = o


def _flash_select(qT, k, vTb, sel, slopes, *, heads_per_step, heads_per_set, sub):
    t = qT.shape[1]
    npair = k.shape[1] // 128
    nkt = t // TK
    sets_per_step = heads_per_step // heads_per_set
    rows = heads_per_step * 64
    return pl.pallas_call(
        functools.partial(_flash_kernel, heads_per_step=heads_per_step, heads_per_set=heads_per_set, sub=sub),
        grid=(npair, t // TQ),
        in_specs=[pl.BlockSpec(memory_space=pltpu.SMEM),
                  pl.BlockSpec((rows, TQ), lambda p, i: (p, i)),
                  pl.BlockSpec((t, 128), lambda p, i: (0, p)),
                  pl.BlockSpec((1, nkt, 128, TK), lambda p, i: (p, 0, 0, 0)),
                  pl.BlockSpec((sets_per_step, sel.shape[1], TQ), lambda p, i: (p, 0, i))],
        out_specs=pl.BlockSpec((rows, TQ), lambda p, i: (p, i)),
        out_shape=jax.ShapeDtypeStruct(qT.shape, F32),
        compiler_params=_cparams("arbitrary", "arbitrary"),
        name="flash_select",
    )(slopes, qT, k, vTb, sel)


def _win_kernel(qT_ref, k_ref, vT_ref, o_ref):
    i = pl.program_id(0)
    kio = lax.broadcasted_iota(I32, (TK, TQ), 0)
    qio = lax.broadcasted_iota(I32, (TK, TQ), 1)
    for h in range(H_NSA):
        half = h // R_NSA
        slope = SLOPES_B[h]
        qh = _head_rows(qT_ref[h * 64:(h + 1) * 64, :], half).astype(BF16)
        sc = kio.astype(F32) * slope
        m = jnp.full((1, TQ), M_INIT, F32)
        l = jnp.zeros((1, TQ), F32)
        acc = jnp.zeros((64, TQ), F32)
        for back in range(WINDOW // TK, -1, -1):
            j = i - back
            jc = jnp.maximum(j, 0)
            kj = k_ref[pl.ds(pl.multiple_of(jc * TK, TK), TK), :]
            dist = (qio - kio) + back * TK
            valid = (dist >= 0) & (dist <= WINDOW) & (j >= 0)
            t = jnp.dot(kj, qh, preferred_element_type=F32) + (sc - slope * (back * TK))
            t = jnp.where(valid, t, NEG_INF)
            m_new = jnp.maximum(m, jnp.max(t, axis=0, keepdims=True))
            p = jnp.exp(t - m_new)
            alpha = jnp.exp(m - m_new)
            l = l * alpha + jnp.sum(p, axis=0, keepdims=True)
            pv = jnp.dot(vT_ref[jc], p.astype(BF16), preferred_element_type=F32)
            acc = acc * alpha + pv[half * 64:(half + 1) * 64]
            m = m_new
        o_ref[h * 64:(h + 1) * 64, :] = acc / jnp.maximum(l, 1e-30)


def _win_prompt(qT, k, vTb):
    t = qT.shape[1]
    nkt = t // TK
    return pl.pallas_call(
        _win_kernel,
        grid=(t // TQ,),
        in_specs=[pl.BlockSpec((512, TQ), lambda i: (0, i)),
                  pl.BlockSpec((t, 128), lambda i: (0, 0)),
                  pl.BlockSpec((nkt, 128, TK), lambda i: (0, 0, 0))],
        out_specs=pl.BlockSpec((512, TQ), lambda i: (0, i)),
        out_shape=jax.ShapeDtypeStruct(qT.shape, F32),
        compiler_params=_cparams("arbitrary"),
        name="win_prompt",
    )(qT, k, vTb)


def _compress_kernel(*refs, paged, n_rows):
    if paged:
        pt_ref, x_hbm, pe_ref, w1_ref, b1_ref, wa_ref, wb_ref, w2_ref, b2_ref, o_ref, xs, sem = refs
    else:
        x_hbm, pe_ref, w1_ref, b1_ref, wa_ref, wb_ref, w2_ref, b2_ref, o_ref, xs, sem = refs
    nc = n_rows // CMP_STRIDE
    if paged:
        b = pl.program_id(0)
        n_pages = n_rows // PAGE_SIZE

        def page_copy(pg, c2):
            return pltpu.make_async_copy(
                x_hbm.at[pt_ref[b, pg], :, pl.ds(c2 * LANES, LANES)],
                xs.at[c2, pl.ds(pl.multiple_of(pg * PAGE_SIZE, PAGE_SIZE), PAGE_SIZE), :], sem.at[0])

        def start(pg, c):
            page_copy(pg, 0).start()
            page_copy(pg, 1).start()
            return c

        def wait(pg, c):
            page_copy(pg, 0).wait()
            page_copy(pg, 1).wait()
            return c

        lax.fori_loop(0, n_pages, start, 0)
        xs[:, pl.ds(n_rows, CMP_STRIDE), :] = jnp.zeros((2, CMP_STRIDE, LANES), F32)
        lax.fori_loop(0, n_pages, wait, 0)
    else:
        cps = [pltpu.make_async_copy(x_hbm.at[:, pl.ds(c2 * LANES, LANES)], xs.at[c2, pl.ds(0, n_rows), :], sem.at[0])
               for c2 in range(2)]
        for cp in cps:
            cp.start()
        xs[:, pl.ds(n_rows, CMP_STRIDE), :] = jnp.zeros((2, CMP_STRIDE, LANES), F32)
        for cp in cps:
            cp.wait()

    def rows(r):
        return jnp.concatenate([xs[c2, pl.ds(r, nc, stride=CMP_STRIDE), :] for c2 in range(2)], axis=1).astype(BF16)

    a = jnp.zeros((nc, 512), F32)
    for r in range(CMP_STRIDE):
        a = a + jnp.dot(rows(r), wa_ref[r], preferred_element_type=F32)
        a = a + jnp.dot(rows(CMP_STRIDE + r), wb_ref[r], preferred_element_type=F32)
    consts = []
    for j in range(2):
        cj = _dot3(pe_ref[j], w1_ref[j])[0:1] + b1_ref[j]
        consts += [cj, cj]
    hid = a + jnp.concatenate(consts, axis=1)
    act = jax.nn.gelu(hid)
    out = jnp.dot(act.astype(BF16), w2_ref[...], preferred_element_type=F32) + b2_ref[...]
    row = lax.broadcasted_iota(I32, out.shape, 0)
    o_ref[0] = jnp.where(row < nc - 1, out, 0.0)


def _compress(x, page_table, cw, *, n_rows, n_batch):
    paged = page_table is not None
    nc = n_rows // CMP_STRIDE
    const = lambda shape: pl.BlockSpec(shape, lambda *_: (0,) * len(shape))
    in_specs = [pl.BlockSpec(memory_space=pl.ANY), const((2, 8, 2048)), const((2, 2048, 128)), const((2, 1, 128)),
                const((16, 256, 512)), const((16, 256, 512)), const((512, 256)), const((1, 256))]
    grid_spec = pltpu.PrefetchScalarGridSpec(
        num_scalar_prefetch=1 if paged else 0,
        grid=(n_batch,),
        in_specs=in_specs,
        out_specs=pl.BlockSpec((1, nc, 256), lambda b, *_: (b, 0, 0)),
        scratch_shapes=[pltpu.VMEM((2, n_rows + CMP_STRIDE, LANES), F32), pltpu.SemaphoreType.DMA((1,))])
    args = ((page_table,) if paged else ()) + (x,) + cw
    return pl.pallas_call(
        functools.partial(_compress_kernel, paged=paged, n_rows=n_rows),
        grid_spec=grid_spec,
        out_shape=jax.ShapeDtypeStruct((n_batch, nc, 256), F32),
        compiler_params=_cparams("arbitrary"),
        name="compress_paged" if paged else "compress",
    )(*args)


def _compress_weights(cmp_pe, cmp_w1, cmp_b1, cmp_w2, cmp_b2):
    half = CMP_STRIDE * D_HEAD
    wa = jnp.zeros((16, 256, 512), F32)
    wb = jnp.zeros((16, 256, 512), F32)
    w2 = jnp.zeros((512, 256), F32)
    for jg in range(4):
        j = jg // 2
        wa = wa.at[:, jg * 64:(jg + 1) * 64, jg * 128:(jg + 1) * 128].set(cmp_w1[j, :half].reshape(16, 64, 128))
        wb = wb.at[:, jg * 64:(jg + 1) * 64, jg * 128:(jg + 1) * 128].set(cmp_w1[j, half:].reshape(16, 64, 128))
        w2 = w2.at[jg * 128:(jg + 1) * 128, jg * 64:(jg + 1) * 64].set(cmp_w2[j])
    pe = jnp.zeros((2, 8, 2048), F32).at[:, 0].set(cmp_pe.reshape(2, 2048))
    b2 = jnp.concatenate([cmp_b2[0], cmp_b2[0], cmp_b2[1], cmp_b2[1]]).reshape(1, 256)
    return (pe, cmp_w1, cmp_b1.reshape(2, 1, 128), wa.astype(BF16), wb.astype(BF16), w2.astype(BF16), b2)


def _overlap_matrix(ns, nc_pad):
    ratio = SLC_BLOCK // CMP_STRIDE
    lr = CMP_LEN // CMP_STRIDE
    w = np.zeros((ns, nc_pad), np.float32)
    for s in range(ns):
        for m in range(ratio):
            for n in range(lr):
                c = ratio * s + m - n
                if 0 <= c < nc_pad - 1:
                    w[s, c] += 1.0
    return w


def _cmp_attn_kernel(qT_ref, pos_ref, kc_ref, vcT_ref, wT_ref, oc_ref, sel_ref, idx_ref, *, k_sel, tq):
    nc_pad = kc_ref.shape[1]
    ns = wT_ref.shape[0]
    pos = pos_ref[...]
    cidx = lax.broadcasted_iota(I32, (nc_pad, tq), 0)
    cpos = cidx * CMP_STRIDE + (CMP_LEN - 1)
    valid = (cpos <= pos) & (cidx < nc_pad - 1)
    ndist = (cpos - pos).astype(F32)
    blk = lax.broadcasted_iota(I32, (ns, tq), 0)
    d = jnp.right_shift(pos, 6) - blk
    forced = (blk < SLC_INIT) | ((d >= 0) & (d < SLC_LOCAL))
    kc = kc_ref[0]
    vcT = vcT_ref[0]
    for g in range(G_NSA):
        imp = jnp.zeros((nc_pad, tq), F32)
        for r in range(R_NSA):
            h = g * R_NSA + r
            qh = _head_rows(qT_ref[h * 64:(h + 1) * 64, :], g).astype(BF16)
            s = jnp.dot(kc, qh, preferred_element_type=F32) + ndist * SLOPES_B[h]
            s = jnp.where(valid, s, NEG_INF)
            m = jnp.max(s, axis=0, keepdims=True)
            m = jnp.where(m > NEG_INF, m, 0.0)
            p = jnp.exp(s - m)
            p = p * (1.0 / jnp.maximum(jnp.sum(p, axis=0, keepdims=True), 1e-30))
            imp = imp + p
            oc = jnp.dot(vcT, p.astype(BF16), preferred_element_type=F32)
            oc_ref[h * 64:(h + 1) * 64, :] = oc[g * 64:(g + 1) * 64]
        hi, lo = _split_bf16(imp)
        wT = wT_ref[...]
        islc = jnp.dot(wT, hi, preferred_element_type=F32) + jnp.dot(wT, lo, preferred_element_type=F32)
        score = jnp.where(forced, POS_INF, jnp.where(d >= 0, islc, NEG_INF))
        sel, picks = _topk_axis0(score, k_sel, blk)
        sel_ref[g] = sel
        for r, pk in enumerate(picks):
            idx_ref[g, r:r + 1, :] = pk
        for r in range(len(picks), SLC_TOPK):
            idx_ref[g, r:r + 1, :] = jnp.full((1, tq), -1, I32)


def _cmp_attn(qT, pos, kc, vcT, ns, *, k_sel, tq, tiles_per_batch):
    n = qT.shape[1]
    nc_pad = kc.shape[1]
    wT = jnp.asarray(_overlap_matrix(ns, nc_pad), BF16)
    bmap = (lambda i: (i // tiles_per_batch, 0, 0)) if tiles_per_batch else (lambda i: (0, 0, 0))
    return pl.pallas_call(
        functools.partial(_cmp_attn_kernel, k_sel=k_sel, tq=tq),
        grid=(n // tq,),
        in_specs=[pl.BlockSpec((512, tq), lambda i: (0, i)),
                  pl.BlockSpec((1, tq), lambda i: (0, i)),
                  pl.BlockSpec((1, nc_pad, 128), bmap),
                  pl.BlockSpec((1, 128, nc_pad), bmap),
                  pl.BlockSpec((ns, nc_pad), lambda i: (0, 0))],
        out_specs=[pl.BlockSpec((512, tq), lambda i: (0, i)),
                   pl.BlockSpec((G_NSA, ns, tq), lambda i: (0, 0, i)),
                   pl.BlockSpec((G_NSA, SLC_TOPK, tq), lambda i: (0, 0, i))],
        out_shape=[jax.ShapeDtypeStruct((512, n), F32),
                   jax.ShapeDtypeStruct((G_NSA, ns, n), F32),
                   jax.ShapeDtypeStruct((G_NSA, SLC_TOPK, n), I32)],
        compiler_params=_cparams("arbitrary"),
        name="cmp_attn",
    )(qT, pos, kc, vcT, wT)


def _post_kernel(x_ref, oa_ref, oc_ref, os_ref, ow_ref, gb_ref, g1_ref, sc2_ref, sh2_ref,
                 goa_ref, gob_ref, gpost_ref, gffn_ref, e3_ref, wout_ref, x1_ref, h2_ref):
    gates = jax.nn.sigmoid(gb_ref[...])
    ob = (_dot3(gates, e3_ref[0]) * oc_ref[...] + _dot3(gates, e3_ref[1]) * os_ref[...]
          + _dot3(gates, e3_ref[2]) * ow_ref[...])
    oa_n = _rms(oa_ref[...], goa_ref[...]).astype(BF16)
    ob_n = _rms(ob, gob_ref[...]).astype(BF16)
    o = (jnp.dot(oa_n, wout_ref[0:512, :], preferred_element_type=F32)
         + jnp.dot(ob_n, wout_ref[512:1024, :], preferred_element_type=F32))
    x1 = x_ref[...] + g1_ref[...] * _rms(o, gpost_ref[...])
    x1_ref[...] = x1
    h2_ref[...] = _rms(x1, gffn_ref[...]) * (1.0 + sc2_ref[...]) + sh2_ref[...]


def _gate_expand():
    e = np.zeros((3, 128, 512), np.float32)
    for h in range(H_NSA):
        for j in range(3):
            e[j, 3 * h + j, h * 64:(h + 1) * 64] = 1.0
    return jnp.asarray(e)


def _post(x, oa, oc, os_, ow, gb, gate1, scale2, shift2, g_oa, g_ob, g_post, g_ffn, w_out_bf):
    r, d = x.shape
    tm = min(256, r)
    per_row = gate1.shape[0] != 1
    mod = pl.BlockSpec((tm, d), lambda i: (i, 0)) if per_row else pl.BlockSpec((1, d), lambda i: (0, 0))
    row = lambda w: pl.BlockSpec((tm, w), lambda i: (i, 0))
    const = lambda shape: pl.BlockSpec(shape, lambda i: (0,) * len(shape))
    return pl.pallas_call(
        _post_kernel,
        grid=(r // tm,),
        in_specs=[row(d), row(512), row(512), row(512), row(512), row(128), mod, mod, mod,
                  const((1, 512)), const((1, 512)), const((1, d)), const((1, d)),
                  const((3, 128, 512)), const((1024, d))],
        out_specs=[row(d), row(d)],
        out_shape=[jax.ShapeDtypeStruct((r, d), F32), jax.ShapeDtypeStruct((r, d), F32)],
        compiler_params=_cparams("arbitrary"),
        name="post_attn",
    )(x, oa, oc, os_, ow, gb, gate1, scale2, shift2, g_oa.reshape(1, 512), g_ob.reshape(1, 512),
      g_post.reshape(1, d), g_ffn.reshape(1, d), _gate_expand(), w_out_bf)


def _router_kernel(h_ref, wr_ref, br_ref, tri_ref, idx_ref, wt_ref, rank_ref, cnt_ref, run_ref, *, n_valid, tm):
    i = pl.program_id(0)

    @pl.when(i == 0)
    def _():
        run_ref[...] = jnp.zeros_like(run_ref)

    s = jax.nn.sigmoid(_dot3(h_ref[...], wr_ref[...]))
    sel = s + br_ref[...]
    lane = lax.broadcasted_iota(I32, (tm, N_EXPERTS), 1)
    per = N_EXPERTS // N_EXPERT_GROUPS
    grp = jnp.right_shift(lane, 5)
    lane_g = lax.broadcasted_iota(I32, (tm, LANES), 1)
    gscore = jnp.full((tm, LANES), NEG_INF, F32)
    for g in range(N_EXPERT_GROUPS):
        v = jnp.where(grp == g, sel, NEG_INF)
        (i1, m1), (_, m2) = _topk_axis1(v, 2, lane)
        gscore = jnp.where(lane_g == g, m1 + m2, gscore)
    emask = jnp.zeros((tm, N_EXPERTS), jnp.bool_)
    for first, _ in _topk_axis1(gscore, TOPK_GROUPS, lane_g):
        emask = emask | (grp == first)
    picks = _topk_axis1(jnp.where(emask, sel, NEG_INF), TOPK_EXPERTS, lane)
    row = lax.broadcasted_iota(I32, (tm, 1), 0) + i * tm
    row_ok = row < n_valid
    onehot = jnp.zeros((tm, N_EXPERTS), F32)
    ws = []
    for first, _ in picks:
        pick = lane == first
        ws.append(jnp.sum(jnp.where(pick, s, 0.0), axis=1, keepdims=True))
        onehot = jnp.where(pick & row_ok, 1.0, onehot)
    wsum = ws[0]
    for w in ws[1:]:
        wsum = wsum + w
    excl = jnp.dot(tri_ref[...], onehot.astype(BF16), preferred_element_type=F32) + run_ref[...]
    idx_out = jnp.zeros((tm, LANES), I32)
    wt_out = jnp.zeros((tm, LANES), F32)
    rank_out = jnp.zeros((tm, LANES), I32)
    for k, (first, _) in enumerate(picks):
        rk = jnp.sum(jnp.where(lane == first, excl, 0.0), axis=1, keepdims=True)
        idx_out = jnp.where(lane_g == k, first, idx_out)
        wt_out = jnp.where(lane_g == k, ws[k] / wsum * ROUTED_SCALE, wt_out)
        rank_out = jnp.where(lane_g == k, rk.astype(I32), rank_out)
    idx_ref[...] = idx_out
    wt_ref[...] = wt_out
    rank_ref[...] = rank_out
    run_ref[...] = run_ref[...] + jnp.sum(onehot, axis=0, keepdims=True)
    cnt_ref[...] = run_ref[...]


def _router(h_all, w_router, b_router, n_valid):
    n, d = h_all.shape
    tm = 128
    tri = jnp.asarray(np.tril(np.ones((tm, tm), np.float32), -1), BF16)
    const = lambda shape: pl.BlockSpec(shape, lambda i: (0,) * len(shape))
    row = lambda w: pl.BlockSpec((tm, w), lambda i: (i, 0))
    return pl.pallas_call(
        functools.partial(_router_kernel, n_valid=n_valid, tm=tm),
        grid=(n // tm,),
        in_specs=[row(d), const((d, N_EXPERTS)), const((1, N_EXPERTS)), const((tm, tm))],
        out_specs=[row(LANES), row(LANES), row(LANES), const((1, N_EXPERTS))],
        out_shape=[jax.ShapeDtypeStruct((n, LANES), I32), jax.ShapeDtypeStruct((n, LANES), F32),
                   jax.ShapeDtypeStruct((n, LANES), I32), jax.ShapeDtypeStruct((1, N_EXPERTS), F32)],
        scratch_shapes=[pltpu.VMEM((1, N_EXPERTS), F32)],
        compiler_params=_cparams("arbitrary"),
        name="router",
    )(h_all, w_router, b_router.reshape(1, N_EXPERTS), tri)


def _gather_rows(idx_hbm, idx_smem, isem, src_hbm, buf, sem, step, n_steps, n_rows):
    def idx_copy(k, slot):
        return pltpu.make_async_copy(idx_hbm.at[k], idx_smem.at[slot], isem.at[slot])

    def issue(slot):
        def body(r, c):
            pltpu.make_async_copy(src_hbm.at[idx_smem[slot, r]], buf.at[slot, r], sem.at[slot]).start()
            return c
        lax.fori_loop(0, n_rows, body, 0)

    @pl.when(step == 0)
    def _():
        idx_copy(0, 0).start()
        idx_copy(0, 0).wait()
        issue(0)

        @pl.when(1 < n_steps)
        def _():
            idx_copy(1, 1).start()

    nxt = (step + 1) % 2

    @pl.when(step + 1 < n_steps)
    def _():
        idx_copy(step + 1, nxt).wait()
        issue(nxt)

    @pl.when(step + 2 < n_steps)
    def _():
        idx_copy(step + 2, step % 2).start()

    slot = step % 2

    @pl.when(step < n_steps)
    def _():
        def wbody(r, c):
            pltpu.make_async_copy(src_hbm.at[0], buf.at[slot, r], sem.at[slot]).wait()
            return c
        lax.fori_loop(0, n_rows, wbody, 0)
    return slot


def _expert_kernel(blk_e_ref, nused_ref, tok_hbm, h_hbm, wup_ref, wdn_ref, o_ref, buf, idx_smem, sem, isem):
    i = pl.program_id(0)
    nused = nused_ref[0]
    slot = _gather_rows(tok_hbm, idx_smem, isem, h_hbm, buf, sem, i, nused, ROW_BLOCK)

    @pl.when(i < nused)
    def _():
        x = jnp.concatenate([buf[slot, :, s, :] for s in range(8)], axis=1).astype(BF16)
        up = jnp.dot(x, wup_ref[0].astype(BF16), preferred_element_type=F32)
        half = up.shape[1] // 2
        act = (jax.nn.silu(up[:, :half]) * up[:, half:]).astype(BF16)
        y = jnp.dot(act, wdn_ref[0].astype(BF16), preferred_element_type=F32)
        for s in range(8):
            o_ref[:, s, :] = y[:, s * 128:(s + 1) * 128]

    @pl.when(i >= nused)
    def _():
        o_ref[...] = jnp.zeros_like(o_ref)


def _experts(h3, blk_e, row_tok, nused, w_exp_up, w_exp_down):
    nblk = blk_e.shape[0]
    rb = ROW_BLOCK
    e, d, f2 = w_exp_up.shape
    grid_spec = pltpu.PrefetchScalarGridSpec(
        num_scalar_prefetch=2,
        grid=(nblk,),
        in_specs=[pl.BlockSpec(memory_space=pl.ANY), pl.BlockSpec(memory_space=pl.ANY),
                  pl.BlockSpec((1, d, f2), lambda i, be, nu: (be[i], 0, 0)),
                  pl.BlockSpec((1, f2 // 2, d), lambda i, be, nu: (be[i], 0, 0))],
        out_specs=pl.BlockSpec((rb, 8, 128), lambda i, be, nu: (i, 0, 0)),
        scratch_shapes=[pltpu.VMEM((2, rb, 8, 128), F32), pltpu.SMEM((2, rb), I32),
                        pltpu.SemaphoreType.DMA((2,)), pltpu.SemaphoreType.DMA((2,))])
    return pl.pallas_call(
        _expert_kernel,
        grid_spec=grid_spec,
        out_shape=jax.ShapeDtypeStruct((nblk * rb, 8, 128), F32),
        compiler_params=_cparams("arbitrary"),
        name="experts",
    )(blk_e, nused, row_tok.reshape(nblk, rb), h3, w_exp_up, w_exp_down)


def _combine_kernel(dest_hbm, y_hbm, wt_ref, h_ref, x1_ref, g2_ref, gpost_ref, wsu_ref, wsd_ref, o_ref,
                    buf, idx_smem, sem, isem, *, tm):
    i = pl.program_id(0)
    slot = _gather_rows(dest_hbm, idx_smem, isem, y_hbm, buf, sem, i, pl.num_programs(0), tm * TOPK_EXPERTS)
    wt = wt_ref[...]
    f = jnp.zeros((tm, h_ref.shape[1]), F32)
    for k in range(TOPK_EXPERTS):
        yk = jnp.concatenate([buf[slot, pl.ds(k, tm, stride=8), s, :] for s in range(8)], axis=1)
        f = f + yk * wt[:, k:k + 1]
    hb = h_ref[...].astype(BF16)
    up = jnp.dot(hb, wsu_ref[...], preferred_element_type=F32)
    half = up.shape[1] // 2
    act = (jax.nn.silu(up[:, :half]) * up[:, half:]).astype(BF16)
    f = f + jnp.dot(act, wsd_ref[...], preferred_element_type=F32)
    o_ref[...] = x1_ref[...] + g2_ref[...] * _rms(f, gpost_ref[...])


def _combine(dest_flat, y3, wt, h_all, x1_all, gate2_rows, g_ffn_post, w_sh_up_bf, w_sh_down_bf):
    n, d = h_all.shape
    tm = 128
    const = lambda shape: pl.BlockSpec(shape, lambda i: (0,) * len(shape))
    row = lambda w: pl.BlockSpec((tm, w), lambda i: (i, 0))
    na = tm * TOPK_EXPERTS
    return pl.pallas_call(
        functools.partial(_combine_kernel, tm=tm),
        grid=(n // tm,),
        in_specs=[pl.BlockSpec(memory_space=pl.ANY), pl.BlockSpec(memory_space=pl.ANY), row(LANES), row(d), row(d),
                  row(d), const((1, d)), const(w_sh_up_bf.shape), const(w_sh_down_bf.shape)],
        out_specs=row(d),
        scratch_shapes=[pltpu.VMEM((2, na, 8, 128), F32), pltpu.SMEM((2, na), I32),
                        pltpu.SemaphoreType.DMA((2,)), pltpu.SemaphoreType.DMA((2,))],
        out_shape=jax.ShapeDtypeStruct((n, d), F32),
        compiler_params=_cparams("arbitrary"),
        name="moe_combine",
    )(dest_flat.reshape(n // tm, na), y3, wt, h_all, x1_all, gate2_rows, g_ffn_post.reshape(1, d),
      w_sh_up_bf, w_sh_down_bf)


def _moe(h_all, x1_all, gate2_rows, n_valid, w_router, b_router, w_exp_up, w_exp_down, w_sh_up, w_sh_down, g_ffn_post):
    n, d = h_all.shape
    rb = ROW_BLOCK
    e = N_EXPERTS
    idx_p, wt_p, rank_p, cnt = _router(h_all, w_router, b_router, n_valid)
    idx = idx_p[:n_valid, :TOPK_EXPERTS]
    rank = rank_p[:n_valid, :TOPK_EXPERTS]
    counts = cnt[0].astype(I32)
    padded = (counts + rb - 1) // rb * rb
    p_end = jnp.cumsum(padded)
    p_start = p_end - padded
    dest = (p_start[idx] + rank).astype(I32)
    nblk = -(-(n_valid * TOPK_EXPERTS + e * (rb - 1)) // rb)
    tok = jnp.repeat(jnp.arange(n_valid, dtype=I32), TOPK_EXPERTS)
    row_tok = jnp.zeros((nblk * rb,), I32).at[dest.reshape(-1)].set(tok)
    blk_e = jnp.clip(jnp.searchsorted(p_end, jnp.arange(nblk, dtype=I32) * rb, side="right"), 0, e - 1).astype(I32)
    nused = (p_end[-1] // rb).astype(I32).reshape(1)
    y3 = _experts(h_all.reshape(n, 8, 128), blk_e, row_tok, nused, w_exp_up, w_exp_down)
    dest_pad = jnp.zeros((n, TOPK_EXPERTS), I32).at[:n_valid].set(dest).reshape(-1)
    return _combine(dest_pad, y3, wt_p, h_all, x1_all, gate2_rows, g_ffn_post,
                    w_sh_up.astype(BF16), w_sh_down.astype(BF16))


def _kmean_pages_kernel(pt_ref, *refs):
    pages, o_ref = refs[:-1], refs[-1]
    for u in range(len(pages) // 2):
        s = jnp.sum(pages[2 * u][0], axis=0, keepdims=True) + jnp.sum(pages[2 * u + 1][0], axis=0, keepdims=True)
        o_ref[0, u:u + 1, :] = s * (1.0 / MOBA_BLOCK)


def _kmean_pages(cache_a, page_table):
    bs, n_pages = page_table.shape
    pps = PAGES_PER_STEP
    in_specs = [pl.BlockSpec((1, PAGE_SIZE, 512), lambda b, s, pt, u=u: (pt[b, s * pps + u], 0, 0)) for u in range(pps)]
    grid_spec = pltpu.PrefetchScalarGridSpec(
        num_scalar_prefetch=1, grid=(bs, n_pages // pps), in_specs=in_specs,
        out_specs=pl.BlockSpec((1, pps // 2, 512), lambda b, s, pt: (b, s, 0)))
    return pl.pallas_call(
        _kmean_pages_kernel, grid_spec=grid_spec,
        out_shape=jax.ShapeDtypeStruct((bs, n_pages // 2, 512), F32),
        compiler_params=_cparams("arbitrary", "arbitrary"),
        name="kmean_pages",
    )(page_table, *([cache_a] * pps))


def _moba_sample_kernel(pt_ref, qbd_ref, km_ref, new_ref, *refs, nb, ts, past):
    pps = PAGES_PER_STEP
    pages = refs[:pps]
    o_ref, sel_scr, m_scr, l_scr, acc_scr = refs[pps:]
    s_idx = pl.program_id(1)
    rows = H_MOBA * ts
    qbd = qbd_ref[0]
    qb = qbd.astype(BF16)
    rowi = lax.broadcasted_iota(I32, (rows, 1), 0)
    slope = jnp.zeros((rows, 1), F32)
    for h in range(H_MOBA):
        slope = jnp.where(_div_pow2(rowi, ts) == h, SLOPES_A[h], slope)
    lane = lax.broadcasted_iota(I32, (rows, LANES), 1)

    @pl.when(s_idx == 0)
    def _():
        gate = _dot3(qbd, km_ref[0], _NT)
        gate = jnp.where(lane < nb, gate, NEG_INF)
        sel = jnp.zeros((rows, LANES), F32)
        for first, m in _topk_axis1(gate, MOBA_TOPK, lane):
            sel = jnp.where((lane == first) & (m > NEG_INF), 1.0, sel)
        sel_scr[...] = sel
        m_scr[...] = jnp.full_like(m_scr, M_INIT)
        l_scr[...] = jnp.zeros_like(l_scr)
        acc_scr[...] = jnp.zeros_like(acc_scr)

    def update(s, v_bf):
        m = m_scr[...]
        m_new = jnp.maximum(m, jnp.max(s, axis=1, keepdims=True))
        p = jnp.exp(s - m_new)
        alpha = jnp.exp(m - m_new)
        l_scr[...] = l_scr[...] * alpha + jnp.sum(p, axis=1, keepdims=True)
        acc_scr[...] = acc_scr[...] * alpha + jnp.dot(p.astype(BF16), v_bf, preferred_element_type=F32)
        m_scr[...] = m_new

    sel = sel_scr[...]
    for u in range(pps):
        page = s_idx * pps + u
        flag = jnp.sum(jnp.where(lane == page // 2, sel, 0.0), axis=1, keepdims=True) > 0.0
        kp = pages[u][0, :, 0:512].astype(BF16)
        vp = pages[u][0, :, 512:1024].astype(BF16)
        s = lax.dot_general(qb, kp, _NT, preferred_element_type=F32)
        kpos = page * PAGE_SIZE + lane - past
        s = jnp.where(flag, s + slope * kpos.astype(F32), NEG_INF)
        update(s, vp)

    @pl.when(s_idx == pl.num_programs(1) - 1)
    def _():
        kn = new_ref[0, :, 0:512].astype(BF16)
        vn = new_ref[0, :, 512:1024].astype(BF16)
        s = lax.dot_general(qb, kn, _NT, preferred_element_type=F32)
        ok = (lane < ts) & (lane <= _mod_pow2(rowi, ts))
        s = jnp.where(ok, s + slope * lane.astype(F32), NEG_INF)
        update(s, vn)
        o = acc_scr[...] / jnp.maximum(l_scr[...], 1e-30)
        col_h = _div_pow2(lax.broadcasted_iota(I32, (rows, 512), 1), D_HEAD)
        o = jnp.where(col_h == _div_pow2(rowi, ts), o, 0.0)
        pick_r = lax.broadcasted_iota(I32, (8, rows), 0)
        pick_c = lax.broadcasted_iota(I32, (8, rows), 1)
        gather = jnp.where(_mod_pow2(pick_c, ts) == pick_r, 1.0, 0.0)
        o_ref[0] = _dot3(gather, o)


def _moba_sample(qbd, km_pad, new_pad, cache_a, page_table, *, nb, ts, past):
    bs, n_pages = page_table.shape
    pps = PAGES_PER_STEP
    rows = H_MOBA * ts
    in_specs = [pl.BlockSpec((1, rows, 512), lambda b, s, pt: (b, 0, 0)),
                pl.BlockSpec((1, LANES, 512), lambda b, s, pt: (b, 0, 0)),
                pl.BlockSpec((1, LANES, 1024), lambda b, s, pt: (b, 0, 0))]
    in_specs += [pl.BlockSpec((1, PAGE_SIZE, 1024), lambda b, s, pt, u=u: (pt[b, s * pps + u], 0, 0)) for u in range(pps)]
    grid_spec = pltpu.PrefetchScalarGridSpec(
        num_scalar_prefetch=1, grid=(bs, n_pages // pps), in_specs=in_specs,
        out_specs=pl.BlockSpec((1, 8, 512), lambda b, s, pt: (b, 0, 0)),
        scratch_shapes=[pltpu.VMEM((rows, LANES), F32), pltpu.VMEM((rows, 1), F32), pltpu.VMEM((rows, 1), F32),
                        pltpu.VMEM((rows, 512), F32)])
    return pl.pallas_call(
        functools.partial(_moba_sample_kernel, nb=nb, ts=ts, past=past),
        grid_spec=grid_spec,
        out_shape=jax.ShapeDtypeStruct((bs, 8, 512), F32),
        compiler_params=_cparams("arbitrary", "arbitrary"),
        name="moba_sample",
    )(page_table, qbd, km_pad, new_pad, *([cache_a] * pps))


def _nsa_sample_kernel(pt_ref, idx_ref, qs_ref, qw_ref, kpos_ref, new_ref, win_ref, wnew_ref, cache_hbm,
                       os_ref, ow_ref, buf, sem, *, ts, past, n_slots):
    b = pl.program_id(0)
    ngq = G_NSA * ts
    copies = []
    for gq in range(ngq):
        for t in range(n_slots):
            blk = jnp.maximum(idx_ref[(b * ngq + gq) * SLC_TOPK + t], 0)
            page = pt_ref[b, blk // 2]
            r0 = pl.multiple_of((blk % 2) * SLC_BLOCK, SLC_BLOCK)
            cp = pltpu.make_async_copy(cache_hbm.at[page, pl.ds(r0, SLC_BLOCK), pl.ds(256, 256)],
                                       buf.at[gq, pl.ds(t * SLC_BLOCK, SLC_BLOCK), :], sem.at[0])
            cp.start()
            copies.append(cp)
    nk = (n_slots + 1) * SLC_BLOCK
    new = new_ref[0]
    for gq in range(ngq):
        buf[gq, pl.ds(n_slots * SLC_BLOCK, SLC_BLOCK), :] = new
    for cp in copies:
        cp.wait()

    rowi = lax.broadcasted_iota(I32, (8, 1), 0)
    for gq in range(ngq):
        g, q = divmod(gq, ts)
        slope = jnp.zeros((8, 1), F32)
        for r in range(R_NSA):
            slope = jnp.where(rowi == r, SLOPES_B[g * R_NSA + r], slope)
        kk = buf[gq, :, 0:128].astype(BF16)
        vv = buf[gq, :, 128:256].astype(BF16)
        s = lax.dot_general(qs_ref[0, gq].astype(BF16), kk, _NT, preferred_element_type=F32)
        kpos = kpos_ref[0, gq]
        ok = (kpos >= 0) & (kpos <= past + q)
        s = jnp.where(ok, s + slope * (kpos - past).astype(F32), NEG_INF)
        m = jnp.max(s, axis=1, keepdims=True)
        m = jnp.where(m > NEG_INF, m, 0.0)
        p = jnp.exp(s - m)
        p = p * (1.0 / jnp.maximum(jnp.sum(p, axis=1, keepdims=True), 1e-30))
        os_ref[0, gq] = jnp.dot(p.astype(BF16), vv, preferred_element_type=F32)

    nw = WINDOW + LANES
    kw_all = jnp.concatenate([win_ref[0], wnew_ref[0]], axis=0)
    kw = kw_all[:, 0:128].astype(BF16)
    vw = kw_all[:, 128:256].astype(BF16)
    wrow = lax.broadcasted_iota(I32, (R_NSA * ts, 1), 0)
    wlane = lax.broadcasted_iota(I32, (R_NSA * ts, nw), 1)
    dist = (WINDOW + _mod_pow2(wrow, ts)) - wlane
    okw = (dist >= 0) & (dist <= WINDOW) & (wlane < WINDOW + ts)
    for g in range(G_NSA):
        slope = jnp.zeros((R_NSA * ts, 1), F32)
        for r in range(R_NSA):
            slope = jnp.where(_div_pow2(wrow, ts) == r, SLOPES_B[g * R_NSA + r], slope)
        s = lax.dot_general(qw_ref[0, g].astype(BF16), kw, _NT, preferred_element_type=F32)
        s = jnp.where(okw, s - slope * dist.astype(F32), NEG_INF)
        m = jnp.max(s, axis=1, keepdims=True)
        m = jnp.where(m > NEG_INF, m, 0.0)
        p = jnp.exp(s - m)
        p = p * (1.0 / jnp.maximum(jnp.sum(p, axis=1, keepdims=True), 1e-30))
        ow_ref[0, g] = jnp.dot(p.astype(BF16), vw, preferred_element_type=F32)


def _nsa_sample(idx_flat, qs, qw, kpos, new_pad, win, wnew_pad, cache_b, page_table, *, ts, past, n_slots):
    bs = page_table.shape[0]
    ngq = G_NSA * ts
    nk = (n_slots + 1) * SLC_BLOCK
    m4 = lambda shape: pl.BlockSpec(shape, lambda b, pt, ix: (b,) + (0,) * (len(shape) - 1))
    grid_spec = pltpu.PrefetchScalarGridSpec(
        num_scalar_prefetch=2, grid=(bs,),
        in_specs=[m4((1, ngq, 8, 128)), m4((1, G_NSA, R_NSA * ts, 128)), m4((1, ngq, 1, nk)),
                  m4((1, SLC_BLOCK, 256)), m4((1, WINDOW, 256)), m4((1, LANES, 256)),
                  pl.BlockSpec(memory_space=pl.ANY)],
        out_specs=[m4((1, ngq, 8, 128)), m4((1, G_NSA, R_NSA * ts, 128))],
        scratch_shapes=[pltpu.VMEM((ngq, nk, 256), F32), pltpu.SemaphoreType.DMA((1,))])
    return pl.pallas_call(
        functools.partial(_nsa_sample_kernel, ts=ts, past=past, n_slots=n_slots),
        grid_spec=grid_spec,
        out_shape=[jax.ShapeDtypeStruct((bs, ngq, 8, 128), F32),
                   jax.ShapeDtypeStruct((bs, G_NSA, R_NSA * ts, 128), F32)],
        compiler_params=_cparams("arbitrary"),
        name="nsa_sample",
    )(page_table, idx_flat, qs, qw, kpos, new_pad, win, wnew_pad, cache_b)


def _vT_blocks(v):
    t, c = v.shape
    return v.astype(BF16).reshape(t // TK, TK, c // 128, 128).transpose(2, 0, 3, 1)


def _prompt_mixer(proj, cw):
    qa, kva, qb, kvb, kvw, _ = proj
    t = qa.shape[0]
    qaT = (qa * Q_SCALE).T
    qbT = (qb * Q_SCALE).T
    kmean = _kmean_prompt(kva)
    sel_a = _moba_gate_prompt(qaT, kmean)
    oaT = _flash_select(qaT, kva[:, :512].astype(BF16), _vT_blocks(kva[:, 512:]), sel_a,
                        jnp.asarray(SLOPES_A, F32), heads_per_step=2, heads_per_set=1, sub=1)
    cmp_out = _compress(kvb, None, cw, n_rows=t, n_batch=1)
    kc = cmp_out[:, :, 0:128].astype(BF16)
    vcT = cmp_out[:, :, 128:256].astype(BF16).transpose(0, 2, 1)
    pos = jnp.arange(t, dtype=I32).reshape(1, t)
    ocT, sel_b, _ = _cmp_attn(qbT, pos, kc, vcT, t // SLC_BLOCK, k_sel=SLC_TOPK, tq=TQ, tiles_per_batch=0)
    osT = _flash_select(qbT, kvb[:, 256:384].astype(BF16), _vT_blocks(kvb[:, 384:512]), sel_b,
                        jnp.asarray(SLOPES_B, F32), heads_per_step=8, heads_per_set=4, sub=TK // SLC_BLOCK)
    owT = _win_prompt(qbT, kvw[:, 0:128].astype(BF16), _vT_blocks(kvw[:, 128:256])[0])
    return oaT.T, ocT.T, osT.T, owT.T


def _sample_mixer(proj, cache_a, cache_b, win_state, page_table, cw, bs, ts):
    qa, kva, qb, kvb, kvw, _ = proj
    n_pages = page_table.shape[1]
    past = n_pages * PAGE_SIZE
    nb = past // MOBA_BLOCK
    km = _kmean_pages(cache_a, page_table)
    km_pad = jnp.zeros((bs, LANES, 512), F32).at[:, :nb].set(km)
    q4 = (qa * Q_SCALE).reshape(bs, ts, H_MOBA, D_HEAD).transpose(0, 2, 1, 3)
    eye = jnp.eye(H_MOBA, dtype=F32)
    qbd = (q4[:, :, :, None, :] * eye[None, :, None, :, None]).reshape(bs, H_MOBA * ts, 512)
    new_a = jnp.zeros((bs, LANES, 1024), F32).at[:, :ts].set(kva.reshape(bs, ts, 1024))
    oa = _moba_sample(qbd, km_pad, new_a, cache_a, page_table, nb=nb, ts=ts, past=past)[:, :ts]
    oa = oa.reshape(bs * ts, 512)
    cmp_out = _compress(cache_b, page_table, cw, n_rows=past, n_batch=bs)
    kc = cmp_out[:, :, 0:128].astype(BF16)
    vcT = cmp_out[:, :, 128:256].astype(BF16).transpose(0, 2, 1)
    qbs = (qb * Q_SCALE).reshape(bs, ts, 512)
    qT = jnp.zeros((bs, LANES, 512), F32).at[:, :ts].set(qbs).reshape(bs * LANES, 512).T
    pos = jnp.broadcast_to(past + jnp.minimum(jnp.arange(LANES, dtype=I32), ts - 1), (bs, LANES)).reshape(1, bs * LANES)
    n_slots = SLC_TOPK - 1
    ocT, _, idx = _cmp_attn(qT, pos, kc, vcT, past // SLC_BLOCK, k_sel=n_slots, tq=LANES, tiles_per_batch=1)
    oc = ocT.T.reshape(bs, LANES, 512)[:, :ts].reshape(bs * ts, 512)
    idx = idx.reshape(G_NSA, SLC_TOPK, bs, LANES)[:, :, :, :ts].transpose(2, 0, 3, 1)
    q5 = qbs.reshape(bs, ts, G_NSA, R_NSA, D_HEAD)
    lane_g = jnp.eye(G_NSA, dtype=F32)
    qsel = q5.transpose(0, 2, 1, 3, 4)[:, :, :, :, None, :] * lane_g[None, :, None, None, :, None]
    qs = jnp.zeros((bs, G_NSA, ts, 8, 128), F32).at[:, :, :, :R_NSA].set(qsel.reshape(bs, G_NSA, ts, R_NSA, 128))
    qs = qs.reshape(bs, G_NSA * ts, 8, 128)
    qwin = q5.transpose(0, 2, 3, 1, 4)[:, :, :, :, None, :] * lane_g[None, :, None, None, :, None]
    qw = qwin.reshape(bs, G_NSA, R_NSA * ts, 128)
    slot_pos = idx[..., :n_slots, None] * SLC_BLOCK + jnp.arange(SLC_BLOCK, dtype=I32)
    slot_pos = jnp.where(idx[..., :n_slots, None] >= 0, slot_pos, -1).reshape(bs, G_NSA, ts, n_slots * SLC_BLOCK)
    own = jnp.where(jnp.arange(SLC_BLOCK) < ts, past + jnp.arange(SLC_BLOCK, dtype=I32), -1)
    kpos = jnp.concatenate([slot_pos, jnp.broadcast_to(own, (bs, G_NSA, ts, SLC_BLOCK))], axis=-1)
    kpos = kpos.reshape(bs, G_NSA * ts, 1, (n_slots + 1) * SLC_BLOCK).astype(I32)
    new_b = jnp.zeros((bs, SLC_BLOCK, 256), F32).at[:, :ts].set(kvb.reshape(bs, ts, 512)[:, :, 256:])
    wnew = jnp.zeros((bs, LANES, 256), F32).at[:, :ts].set(kvw.reshape(bs, ts, 256))
    os_raw, ow_raw = _nsa_sample(idx.reshape(-1).astype(I32), qs, qw, kpos, new_b, win_state.reshape(bs, WINDOW, 256),
                                 wnew, cache_b, page_table, ts=ts, past=past, n_slots=n_slots)
    os5 = os_raw.reshape(bs, G_NSA, ts, 8, G_NSA, D_HEAD)[:, :, :, :R_NSA]
    os_ = jnp.stack([os5[:, g, :, :, g] for g in range(G_NSA)], axis=2).reshape(bs * ts, 512)
    ow5 = ow_raw.reshape(bs, G_NSA, R_NSA, ts, G_NSA, D_HEAD)
    ow = jnp.stack([ow5[:, g, :, :, g] for g in range(G_NSA)], axis=1)
    ow = ow.transpose(0, 3, 1, 2, 4).reshape(bs * ts, 512)
    return oa, oc, os_, ow


def _layer(x_p, x_s, cache_a, cache_b, win_state, page_table, c_p, c_s, w):
    (w_ada, b_ada, g_mix_pre, g_mix_post, g_ffn_pre, g_ffn_post, w_in, g_out_moba, g_out_nsa, w_out,
     cmp_pe, cmp_w1, cmp_b1, cmp_w2, cmp_b2, w_router, b_router, w_exp_up, w_exp_down, w_sh_up, w_sh_down) = w
    t, d = x_p.shape
    bs, ts, _ = x_s.shape
    assert c_p.shape[0] == 1 and t % (8 * MOBA_BLOCK) == 0 and win_state.shape[1] == WINDOW
    n_s = bs * ts
    rows = -(-(1 + bs) // 8) * 8
    c_all = jnp.zeros((rows, d), F32).at[0:1].set(c_p).at[1:1 + bs].set(c_s)
    mod = _ada(c_all, w_ada, b_ada).reshape(rows, 6, d)
    mod_p = [mod[0:1, i] for i in range(6)]
    mod_s = [jnp.repeat(mod[1:1 + bs, i], ts, axis=0) for i in range(6)]

    w_pad = jnp.zeros((d, _PROJ_CUTS[-1]), F32).at[:, :w_in.shape[1]].set(w_in).astype(BF16)
    cw = _compress_weights(cmp_pe, cmp_w1, cmp_b1, cmp_w2, cmp_b2)
    w_out_bf = w_out.astype(BF16)

    proj_p = _inproj(x_p, mod_p[1], mod_p[0], g_mix_pre, w_pad)
    proj_s = _inproj(x_s.reshape(n_s, d), mod_s[1], mod_s[0], g_mix_pre, w_pad)

    o_p = _prompt_mixer(proj_p, cw)
    o_s = _sample_mixer(proj_s, cache_a, cache_b, win_state, page_table, cw, bs, ts)

    x1_p, h2_p = _post(x_p, *o_p, proj_p[5], mod_p[2], mod_p[4], mod_p[3], g_out_moba, g_out_nsa, g_mix_post,
                       g_ffn_pre, w_out_bf)
    x1_s, h2_s = _post(x_s.reshape(n_s, d), *o_s, proj_s[5], mod_s[2], mod_s[4], mod_s[3], g_out_moba, g_out_nsa,
                       g_mix_post, g_ffn_pre, w_out_bf)

    n_valid = t + n_s
    n_all = -(-n_valid // 128) * 128
    pad = lambda a: jnp.concatenate([a, jnp.zeros((n_all - n_valid, d), F32)], axis=0) if n_all > n_valid else a
    h_all = pad(jnp.concatenate([h2_p, h2_s], axis=0))
    x1_all = pad(jnp.concatenate([x1_p, x1_s], axis=0))
    g2_rows = pad(jnp.concatenate([jnp.broadcast_to(mod_p[5], (t, d)), mod_s[5]], axis=0))
    y_all = _moe(h_all, x1_all, g2_rows, n_valid, w_router, b_router, w_exp_up, w_exp_down, w_sh_up, w_sh_down,
                 g_ffn_post)
    y_p = y_all[:t]
    y_s = y_all[t:n_valid].reshape(bs, ts, d)

    _, kva_p, _, kvb_p, kvw_p, _ = proj_p
    _, kva_s, _, kvb_s, kvw_s, _ = proj_s
    state_p = (kva_p.reshape(1, t, 2 * H_MOBA, D_HEAD), kvb_p.reshape(1, t, 4 * G_NSA, D_HEAD),
               kvw_p[t - min(WINDOW, t):].reshape(1, min(WINDOW, t), 2 * G_NSA, D_HEAD))
    win_new = jnp.concatenate([win_state, kvw_s.reshape(bs, ts, 2 * G_NSA, D_HEAD)], axis=1)[:, -WINDOW:]
    state_s = (kva_s.reshape(bs, ts, 2 * H_MOBA, D_HEAD), kvb_s.reshape(bs, ts, 4 * G_NSA, D_HEAD), win_new)
    return y_p, y_s, state_p, state_s


def kernel(x_prompt, x_sample, cache_moba, cache_nsa, state_nsa_win, page_table, c_prompt, c_sample, w_ada, b_ada, g_mix_pre, g_mix_post, g_ffn_pre, g_ffn_post, w_in, g_out_moba, g_out_nsa, w_out, cmp_pe, cmp_w1, cmp_b1, cmp_w2, cmp_b2, w_router, b_router, w_exp_up, w_exp_down, w_sh_up, w_sh_down):
    weights = (w_ada, b_ada, g_mix_pre, g_mix_post, g_ffn_pre, g_ffn_post, w_in, g_out_moba, g_out_nsa, w_out,
               cmp_pe, cmp_w1, cmp_b1, cmp_w2, cmp_b2, w_router, b_router, w_exp_up, w_exp_down, w_sh_up, w_sh_down)
    depth = w_ada.shape[0]
    n_pool = cache_moba.shape[1]
    y_p, y_s = x_prompt[0], x_sample
    st_p, st_s = [], []
    for layer in range(depth):
        w_l = tuple(w[layer] for w in weights)
        cache_a = cache_moba[layer].reshape(n_pool, PAGE_SIZE, 2 * H_MOBA * D_HEAD)
        cache_b = cache_nsa[layer].reshape(n_pool, PAGE_SIZE, 4 * G_NSA * D_HEAD)
        y_p, y_s, sp, ss = _layer(y_p, y_s, cache_a, cache_b, state_nsa_win[layer], page_table, c_prompt, c_sample, w_l)
        st_p.append(sp)
        st_s.append(ss)
    stack = lambda sts, i: jnp.stack([s[i] for s in sts])
    return (y_p[None], y_s, stack(st_p, 0), stack(st_p, 1), stack(st_p, 2), stack(st_s, 0), stack(st_s, 1), stack(st_s, 2))
```

```python
import functools

import numpy as np
import jax
import jax.numpy as jnp
from jax import lax
from jax.experimental import pallas as pl
from jax.experimental.pallas import tpu as pltpu

F32, BF16, I32 = jnp.float32, jnp.bfloat16, jnp.int32
NEG_INF = float("-inf")
POS_INF = float("inf")
M_INIT = -1e30

D_HEAD = 64
H_MOBA = 8
H_NSA = 8
G_NSA = 2
R_NSA = H_NSA // G_NSA
MOBA_BLOCK = 256
MOBA_TOPK = 3
CMP_STRIDE = 16
CMP_LEN = 2 * CMP_STRIDE
CMP_HIDDEN = 2 * D_HEAD
SLC_BLOCK = 64
SLC_TOPK = 16
SLC_INIT = 1
SLC_LOCAL = 2
WINDOW = 512
N_EXPERTS = 256
TOPK_EXPERTS = 8
N_EXPERT_GROUPS = 8
TOPK_GROUPS = 4
ROUTED_SCALE = 2.5
PAGE_SIZE = 128
EPS = 1e-6
Q_SCALE = D_HEAD ** -0.5

_SLOPES = [2.0 ** (-8.0 * i / (H_MOBA + H_NSA)) for i in range(1, H_MOBA + H_NSA + 1)]
SLOPES_A = _SLOPES[0::2][:H_MOBA]
SLOPES_B = _SLOPES[1::2][:H_NSA]

LANES = 128
TQ = 256
TK = 256
TKF = 128
VMEM_LIMIT_BYTES = 56 * 1024 * 1024
ROW_BLOCK = 128
PAGES_PER_STEP = 16


def _cparams(*sem):
    return pltpu.CompilerParams(dimension_semantics=sem, vmem_limit_bytes=VMEM_LIMIT_BYTES)


def _rms(x, g):
    return x * lax.rsqrt(jnp.mean(x * x, axis=-1, keepdims=True) + EPS) * g


def _split_bf16(a):
    hi = a.astype(BF16)
    lo = (a - hi.astype(F32)).astype(BF16)
    return hi, lo


def _dot3(a, b, dims=None):
    ah, al = _split_bf16(a)
    bh, bl = _split_bf16(b)
    if dims is None:
        d = lambda x, y: jnp.dot(x, y, preferred_element_type=F32)
    else:
        d = lambda x, y: lax.dot_general(x, y, dims, preferred_element_type=F32)
    return d(ah, bh) + (d(ah, bl) + d(al, bh))


_NT = (((1,), (1,)), ((), ()))


def _div_pow2(x, n):
    assert n & (n - 1) == 0
    return jnp.right_shift(x, n.bit_length() - 1)


def _mod_pow2(x, n):
    assert n & (n - 1) == 0
    return jnp.bitwise_and(x, n - 1)


def _head_rows(q64, half):
    z = jnp.zeros_like(q64)
    return jnp.concatenate([z, q64] if half else [q64, z], axis=0)


def _ada_kernel(c_ref, w_ref, b_ref, o_ref):
    a = jax.nn.silu(c_ref[...])
    o_ref[...] = _dot3(a, w_ref[...]) + b_ref[...]


def _ada(c, w_ada, b_ada):
    r, d = c.shape
    n = w_ada.shape[1]
    tn = 768
    return pl.pallas_call(
        _ada_kernel,
        grid=(n // tn,),
        in_specs=[pl.BlockSpec((r, d), lambda j: (0, 0)),
                  pl.BlockSpec((d, tn), lambda j: (0, j)),
                  pl.BlockSpec((1, tn), lambda j: (0, j))],
        out_specs=pl.BlockSpec((r, tn), lambda j: (0, j)),
        out_shape=jax.ShapeDtypeStruct((r, n), F32),
        compiler_params=_cparams("arbitrary"),
        name="ada",
    )(c, w_ada, b_ada.reshape(1, n))


_PROJ_CUTS = (0, 512, 1536, 2048, 2560, 2816, 2944)


def _inproj_kernel(x_ref, sc_ref, sh_ref, g_ref, w_ref, *out_refs):
    h = _rms(x_ref[...], g_ref[...]) * (1.0 + sc_ref[...]) + sh_ref[...]
    hb = h.astype(BF16)
    for o_ref, a, b in zip(out_refs, _PROJ_CUTS[:-1], _PROJ_CUTS[1:]):
        o_ref[...] = jnp.dot(hb, w_ref[:, a:b], preferred_element_type=F32)


def _inproj(x, scale, shift, g, w_pad):
    r, d = x.shape
    tm = min(512, r)
    per_row = scale.shape[0] != 1
    mod_spec = pl.BlockSpec((tm, d), lambda i: (i, 0)) if per_row else pl.BlockSpec((1, d), lambda i: (0, 0))
    widths = [b - a for a, b in zip(_PROJ_CUTS[:-1], _PROJ_CUTS[1:])]
    return pl.pallas_call(
        _inproj_kernel,
        grid=(r // tm,),
        in_specs=[pl.BlockSpec((tm, d), lambda i: (i, 0)), mod_spec, mod_spec,
                  pl.BlockSpec((1, d), lambda i: (0, 0)),
                  pl.BlockSpec(w_pad.shape, lambda i: (0, 0))],
        out_specs=[pl.BlockSpec((tm, w), lambda i: (i, 0)) for w in widths],
        out_shape=[jax.ShapeDtypeStruct((r, w), F32) for w in widths],
        compiler_params=_cparams("arbitrary"),
        name="inproj",
    )(x, scale, shift, g.reshape(1, d), w_pad)


def _topk_axis0(score, k, idx):
    n = score.shape[0]
    sel = jnp.zeros(score.shape, F32)
    picks = []
    for _ in range(k):
        m = jnp.max(score, axis=0, keepdims=True)
        first = jnp.min(jnp.where(score == m, idx, n), axis=0, keepdims=True)
        pick = idx == first
        ok = m > NEG_INF
        sel = jnp.where(pick & ok, 1.0, sel)
        picks.append(jnp.where(ok, first, -1))
        score = jnp.where(pick, NEG_INF, score)
    return sel, picks


def _topk_axis1(score, k, idx):
    n = score.shape[1]
    out = []
    for _ in range(k):
        m = jnp.max(score, axis=1, keepdims=True)
        first = jnp.min(jnp.where(score == m, idx, n), axis=1, keepdims=True)
        out.append((first, m))
        score = jnp.where(idx == first, NEG_INF, score)
    return out


def _kmean_kernel(k_ref, o_ref):
    x = k_ref[...]
    n = x.shape[0] // MOBA_BLOCK
    o_ref[...] = jnp.sum(x.reshape(n, MOBA_BLOCK, x.shape[1]), axis=1) * (1.0 / MOBA_BLOCK)


def _kmean_prompt(kva):
    t = kva.shape[0]
    nb = t // MOBA_BLOCK
    per = 8
    return pl.pallas_call(
        _kmean_kernel,
        grid=(nb // per,),
        in_specs=[pl.BlockSpec((per * MOBA_BLOCK, 512), lambda i: (i, 0))],
        out_specs=pl.BlockSpec((per, 512), lambda i: (i, 0)),
        out_shape=jax.ShapeDtypeStruct((nb, 512), F32),
        compiler_params=_cparams("arbitrary"),
        name="kmean_prompt",
    )(kva)


def _moba_gate_kernel(qT_ref, km_ref, sel_ref, *, nb):
    own = pl.program_id(0)
    blk = lax.broadcasted_iota(I32, (nb, TQ), 0)
    for h in range(H_MOBA):
        p, half = divmod(h, 2)
        qh = _head_rows(qT_ref[h * 64:(h + 1) * 64, :], half)
        g = _dot3(km_ref[:, p * 128:(p + 1) * 128], qh)
        g = jnp.where(blk < own, g, NEG_INF)
        sel, _ = _topk_axis0(g, MOBA_TOPK, blk)
        sel_ref[h] = jnp.where(blk == own, 1.0, sel)


def _moba_gate_prompt(qT, kmean):
    t = qT.shape[1]
    nb = kmean.shape[0]
    return pl.pallas_call(
        functools.partial(_moba_gate_kernel, nb=nb),
        grid=(t // TQ,),
        in_specs=[pl.BlockSpec((512, TQ), lambda i: (0, i)),
                  pl.BlockSpec((nb, 512), lambda i: (0, 0))],
        out_specs=pl.BlockSpec((H_MOBA, nb, TQ), lambda i: (0, 0, i)),
        out_shape=jax.ShapeDtypeStruct((H_MOBA, nb, t), F32),
        compiler_params=_cparams("arbitrary"),
        name="moba_gate",
    )(qT, kmean)


def _flash_kernel(qT_ref, k_ref, vT_ref, sel_ref, o_ref, qh_scr, sc_scr, *state, heads, sub, sel_div):
    nh = len(heads)
    m_scr, l_scr, acc_scr = state[:nh], state[nh:2 * nh], state[2 * nh:]
    i = pl.program_id(0)
    kio = lax.broadcasted_iota(I32, (TKF, TQ), 0)
    qio = lax.broadcasted_iota(I32, (TKF, TQ), 1)
    kiof = kio.astype(F32)
    rows = TKF // sub
    for h, (pair, half, set_idx, slope) in enumerate(heads):
        qh_scr[h] = _head_rows(qT_ref[h * 64:(h + 1) * 64, :], half).astype(BF16)
        sc_scr[h] = kiof * slope
        m_scr[h][...] = jnp.full((1, TQ), M_INIT, F32)
        l_scr[h][...] = jnp.zeros((1, TQ), F32)
        acc_scr[h][...] = jnp.zeros((64, TQ), F32)

    def sel_rows(set_idx, j):
        if sub == 1:
            return sel_ref[set_idx, pl.ds(j // sel_div, 1), :] > 0.0
        per8 = 8 // sub
        blk8 = sel_ref[set_idx, pl.ds(pl.multiple_of((j // per8) * 8, 8), 8), :]
        s = blk8[0:sub]
        for u in range(1, per8):
            s = jnp.where(j % per8 == u, blk8[u * sub:(u + 1) * sub], s)
        return s > 0.0

    def step(j, diag):
        off_base = (j * TKF - i * TQ).astype(F32)
        row0 = pl.multiple_of(j * TKF, TKF)
        sel_cache = {}
        for h, (pair, half, set_idx, slope) in enumerate(heads):
            if set_idx not in sel_cache:
                sel_cache[set_idx] = sel_rows(set_idx, j)
            selj = sel_cache[set_idx]
            kj = k_ref[pl.ds(row0, TKF), pair * 128:(pair + 1) * 128]
            t = jnp.dot(kj, qh_scr[h], preferred_element_type=F32) + sc_scr[h]
            if diag is not None:
                t = jnp.where(kio + diag * TKF <= qio, t, NEG_INF)
            off = slope * off_base
            t3 = t.reshape(sub, rows, TQ)
            mb = jnp.where(selj, jnp.max(t3, axis=1) + off, M_INIT)
            m = m_scr[h][...]
            m_new = jnp.maximum(m, jnp.max(mb, axis=0, keepdims=True))
            mu = jnp.where(selj, m_new - off, POS_INF)
            p = jnp.exp(t3 - mu[:, None, :]).reshape(TKF, TQ)
            alpha = jnp.exp(m - m_new)
            l_scr[h][...] = l_scr[h][...] * alpha + jnp.sum(p, axis=0, keepdims=True)
            pv = jnp.dot(vT_ref[pair, j], p.astype(BF16), preferred_element_type=F32)
            acc_scr[h][...] = acc_scr[h][...] * alpha + pv[half * 64:(half + 1) * 64]
            m_scr[h][...] = m_new

    per_q = TQ // TKF

    def body(jj, c):
        for u in range(per_q):
            step(jj * per_q + u, None)
        return c

    lax.fori_loop(0, i, body, 0)
    for u in range(per_q):
        step(i * per_q + u, u)
    for h in range(len(heads)):
        o_ref[h * 64:(h + 1) * 64, :] = acc_scr[h][...] / jnp.maximum(l_scr[h][...], 1e-30)


def _flash_select(qT, k, vTb, sel, heads, *, block):
    t = qT.shape[1]
    nh = len(heads)
    sub, sel_div = max(TKF // block, 1), max(block // TKF, 1)
    resident = lambda shape: pl.BlockSpec(shape, lambda i: (0,) * len(shape), pipeline_mode=pl.Buffered(1))
    return pl.pallas_call(
        functools.partial(_flash_kernel, heads=heads, sub=sub, sel_div=sel_div),
        grid=(t // TQ,),
        in_specs=[pl.BlockSpec((nh * 64, TQ), lambda i: (0, i)),
                  resident(k.shape), resident(vTb.shape),
                  pl.BlockSpec((sel.shape[0], sel.shape[1], TQ), lambda i: (0, 0, i))],
        out_specs=pl.BlockSpec((nh * 64, TQ), lambda i: (0, i)),
        out_shape=jax.ShapeDtypeStruct(qT.shape, F32),
        scratch_shapes=([pltpu.VMEM((nh, 128, TQ), BF16), pltpu.VMEM((nh, TKF, TQ), F32)]
                        + [pltpu.VMEM((1, TQ), F32)] * (2 * nh) + [pltpu.VMEM((64, TQ), F32)] * nh),
        compiler_params=_cparams("arbitrary"),
        name="flash_select",
    )(qT, k, vTb, sel)


HEADS_MOBA = tuple((h // 2, h % 2, h, SLOPES_A[h]) for h in range(H_MOBA))
HEADS_NSA = tuple((0, h // R_NSA, h // R_NSA, SLOPES_B[h]) for h in range(H_NSA))


def _win_kernel(qT_ref, k_ref, vT_ref, o_ref):
    i = pl.program_id(0)
    kio = lax.broadcasted_iota(I32, (TK, TQ), 0)
    qio = lax.broadcasted_iota(I32, (TK, TQ), 1)
    for h in range(H_NSA):
        half = h // R_NSA
        slope = SLOPES_B[h]
        qh = _head_rows(qT_ref[h * 64:(h + 1) * 64, :], half).astype(BF16)
        sc = kio.astype(F32) * slope
        m = jnp.full((1, TQ), M_INIT, F32)
        l = jnp.zeros((1, TQ), F32)
        acc = jnp.zeros((64, TQ), F32)
        for back in range(WINDOW // TK, -1, -1):
            j = i - back
            jc = jnp.maximum(j, 0)
            kj = k_ref[pl.ds(pl.multiple_of(jc * TK, TK), TK), :]
            dist = (qio - kio) + back * TK
            valid = (dist >= 0) & (dist <= WINDOW) & (j >= 0)
            t = jnp.dot(kj, qh, preferred_element_type=F32) + (sc - slope * (back * TK))
            t = jnp.where(valid, t, NEG_INF)
            m_new = jnp.maximum(m, jnp.max(t, axis=0, keepdims=True))
            p = jnp.exp(t - m_new)
            alpha = jnp.exp(m - m_new)
            l = l * alpha + jnp.sum(p, axis=0, keepdims=True)
            pv = jnp.dot(vT_ref[jc], p.astype(BF16), preferred_element_type=F32)
            acc = acc * alpha + pv[half * 64:(half + 1) * 64]
            m = m_new
        o_ref[h * 64:(h + 1) * 64, :] = acc / jnp.maximum(l, 1e-30)


def _win_prompt(qT, k, vTb):
    t = qT.shape[1]
    nkt = t // TK
    return pl.pallas_call(
        _win_kernel,
        grid=(t // TQ,),
        in_specs=[pl.BlockSpec((512, TQ), lambda i: (0, i)),
                  pl.BlockSpec((t, 128), lambda i: (0, 0)),
                  pl.BlockSpec((nkt, 128, TK), lambda i: (0, 0, 0))],
        out_specs=pl.BlockSpec((512, TQ), lambda i: (0, i)),
        out_shape=jax.ShapeDtypeStruct(qT.shape, F32),
        compiler_params=_cparams("arbitrary"),
        name="win_prompt",
    )(qT, k, vTb)


def _compress_kernel(*refs, paged, n_rows):
    if paged:
        pt_ref, x_hbm, pe_ref, w1_ref, b1_ref, wa_ref, wb_ref, w2_ref, b2_ref, o_ref, xs, sem, stage = refs
    else:
        x_hbm, pe_ref, w1_ref, b1_ref, wa_ref, wb_ref, w2_ref, b2_ref, o_ref, xs, sem = refs
    nc = n_rows // CMP_STRIDE
    if paged:
        b = pl.program_id(0)
        n_pages = n_rows // PAGE_SIZE
        ch = stage.shape[1]

        def page_copy(c, u):
            return pltpu.make_async_copy(x_hbm.at[pt_ref[b, c * ch + u], pl.ds(0, 256), :],
                                         stage.at[c % 2, u], sem.at[c % 2])

        def start_chunk(c):
            def body(u, carry):
                page_copy(c, u).start()
                return carry
            lax.fori_loop(0, ch, body, 0)

        def finish_chunk(c):
            def wait_body(u, carry):
                page_copy(c, u).wait()
                return carry
            lax.fori_loop(0, ch, wait_body, 0)

            def body(u, carry):
                row0 = pl.multiple_of((c * ch + u) * PAGE_SIZE, PAGE_SIZE)
                for c2 in range(2):
                    xs[c2, pl.ds(row0, PAGE_SIZE), :] = stage[c % 2, u, c2 * LANES:(c2 + 1) * LANES, :].T
                return carry
            lax.fori_loop(0, ch, body, 0)

        start_chunk(0)
        xs[:, pl.ds(n_rows, CMP_STRIDE), :] = jnp.zeros((2, CMP_STRIDE, LANES), F32)
        for c in range(n_pages // ch):
            if c + 1 < n_pages // ch:
                start_chunk(c + 1)
            finish_chunk(c)
    else:
        cps = [pltpu.make_async_copy(x_hbm.at[:, pl.ds(c2 * LANES, LANES)], xs.at[c2, pl.ds(0, n_rows), :], sem.at[0])
               for c2 in range(2)]
        for cp in cps:
            cp.start()
        xs[:, pl.ds(n_rows, CMP_STRIDE), :] = jnp.zeros((2, CMP_STRIDE, LANES), F32)
        for cp in cps:
            cp.wait()

    def rows(r):
        return jnp.concatenate([xs[c2, pl.ds(r, nc, stride=CMP_STRIDE), :] for c2 in range(2)], axis=1).astype(BF16)

    a = jnp.zeros((nc, 512), F32)
    for r in range(CMP_STRIDE):
        a = a + jnp.dot(rows(r), wa_ref[r], preferred_element_type=F32)
        a = a + jnp.dot(rows(CMP_STRIDE + r), wb_ref[r], preferred_element_type=F32)
    consts = []
    for j in range(2):
        cj = _dot3(pe_ref[j], w1_ref[j])[0:1] + b1_ref[j]
        consts += [cj, cj]
    hid = a + jnp.concatenate(consts, axis=1)
    act = jax.nn.gelu(hid)
    out = jnp.dot(act.astype(BF16), w2_ref[...], preferred_element_type=F32) + b2_ref[...]
    row = lax.broadcasted_iota(I32, out.shape, 0)
    o_ref[0] = jnp.where(row < nc - 1, out, 0.0)


def _compress(x, page_table, cw, *, n_rows, n_batch):
    paged = page_table is not None
    nc = n_rows // CMP_STRIDE
    const = lambda shape: pl.BlockSpec(shape, lambda *_: (0,) * len(shape), pipeline_mode=pl.Buffered(1))
    in_specs = [pl.BlockSpec(memory_space=pl.ANY), const((2, 8, 2048)), const((2, 2048, 128)), const((2, 1, 128)),
                const((16, 256, 512)), const((16, 256, 512)), const((512, 256)), const((1, 256))]
    scratch = [pltpu.VMEM((2, n_rows + CMP_STRIDE, LANES), F32), pltpu.SemaphoreType.DMA((2,))]
    if paged:
        chunk = min(32, n_rows // PAGE_SIZE)
        scratch.append(pltpu.VMEM((2, chunk, 256, PAGE_SIZE), F32))
    grid_spec = pltpu.PrefetchScalarGridSpec(
        num_scalar_prefetch=1 if paged else 0,
        grid=(n_batch,),
        in_specs=in_specs,
        out_specs=pl.BlockSpec((1, nc, 256), lambda b, *_: (b, 0, 0)),
        scratch_shapes=scratch)
    args = ((page_table,) if paged else ()) + (x,) + cw
    return pl.pallas_call(
        functools.partial(_compress_kernel, paged=paged, n_rows=n_rows),
        grid_spec=grid_spec,
        out_shape=jax.ShapeDtypeStruct((n_batch, nc, 256), F32),
        compiler_params=_cparams("arbitrary"),
        name="compress_paged" if paged else "compress",
    )(*args)


def _compress_weights(cmp_pe, cmp_w1, cmp_b1, cmp_w2, cmp_b2):
    half = CMP_STRIDE * D_HEAD
    wa = jnp.zeros((16, 256, 512), F32)
    wb = jnp.zeros((16, 256, 512), F32)
    w2 = jnp.zeros((512, 256), F32)
    for jg in range(4):
        j = jg // 2
        wa = wa.at[:, jg * 64:(jg + 1) * 64, jg * 128:(jg + 1) * 128].set(cmp_w1[j, :half].reshape(16, 64, 128))
        wb = wb.at[:, jg * 64:(jg + 1) * 64, jg * 128:(jg + 1) * 128].set(cmp_w1[j, half:].reshape(16, 64, 128))
        w2 = w2.at[jg * 128:(jg + 1) * 128, jg * 64:(jg + 1) * 64].set(cmp_w2[j])
    pe = jnp.zeros((2, 8, 2048), F32).at[:, 0].set(cmp_pe.reshape(2, 2048))
    b2 = jnp.concatenate([cmp_b2[0], cmp_b2[0], cmp_b2[1], cmp_b2[1]]).reshape(1, 256)
    return (pe, cmp_w1, cmp_b1.reshape(2, 1, 128), wa.astype(BF16), wb.astype(BF16), w2.astype(BF16), b2)


def _overlap_matrix(ns, nc_pad):
    ratio = SLC_BLOCK // CMP_STRIDE
    lr = CMP_LEN // CMP_STRIDE
    w = np.zeros((ns, nc_pad), np.float32)
    for s in range(ns):
        for m in range(ratio):
            for n in range(lr):
                c = ratio * s + m - n
                if 0 <= c < nc_pad - 1:
                    w[s, c] += 1.0
    return w


def _cmp_attn_kernel(qT_ref, pos_ref, kc_ref, vcT_ref, wT_ref, oc_ref, sel_ref, idx_ref, *, k_sel, tq):
    nc_pad = kc_ref.shape[1]
    ns = wT_ref.shape[0]
    pos = pos_ref[...]
    cidx = lax.broadcasted_iota(I32, (nc_pad, tq), 0)
    cpos = cidx * CMP_STRIDE + (CMP_LEN - 1)
    valid = (cpos <= pos) & (cidx < nc_pad - 1)
    ndist = (cpos - pos).astype(F32)
    blk = lax.broadcasted_iota(I32, (ns, tq), 0)
    d = jnp.right_shift(pos, 6) - blk
    forced = (blk < SLC_INIT) | ((d >= 0) & (d < SLC_LOCAL))
    kc = kc_ref[0]
    vcT = vcT_ref[0]
    for g in range(G_NSA):
        imp = jnp.zeros((nc_pad, tq), F32)
        for r in range(R_NSA):
            h = g * R_NSA + r
            qh = _head_rows(qT_ref[h * 64:(h + 1) * 64, :], g).astype(BF16)
            s = jnp.dot(kc, qh, preferred_element_type=F32) + ndist * SLOPES_B[h]
            s = jnp.where(valid, s, NEG_INF)
            m = jnp.max(s, axis=0, keepdims=True)
            m = jnp.where(m > NEG_INF, m, 0.0)
            p = jnp.exp(s - m)
            p = p * (1.0 / jnp.maximum(jnp.sum(p, axis=0, keepdims=True), 1e-30))
            imp = imp + p
            oc = jnp.dot(vcT, p.astype(BF16), preferred_element_type=F32)
            oc_ref[h * 64:(h + 1) * 64, :] = oc[g * 64:(g + 1) * 64]
        hi, lo = _split_bf16(imp)
        wT = wT_ref[...]
        islc = jnp.dot(wT, hi, preferred_element_type=F32) + jnp.dot(wT, lo, preferred_element_type=F32)
        score = jnp.where(forced, POS_INF, jnp.where(d >= 0, islc, NEG_INF))
        sel, picks = _topk_axis0(score, k_sel, blk)
        sel_ref[g] = sel
        for r, pk in enumerate(picks):
            idx_ref[g, r:r + 1, :] = pk
        for r in range(len(picks), SLC_TOPK):
            idx_ref[g, r:r + 1, :] = jnp.full((1, tq), -1, I32)


def _cmp_attn(qT, pos, kc, vcT, ns, *, k_sel, tq, tiles_per_batch):
    n = qT.shape[1]
    nc_pad = kc.shape[1]
    wT = jnp.asarray(_overlap_matrix(ns, nc_pad), BF16)
    bmap = (lambda i: (i // tiles_per_batch, 0, 0)) if tiles_per_batch else (lambda i: (0, 0, 0))
    return pl.pallas_call(
        functools.partial(_cmp_attn_kernel, k_sel=k_sel, tq=tq),
        grid=(n // tq,),
        in_specs=[pl.BlockSpec((512, tq), lambda i: (0, i)),
                  pl.BlockSpec((1, tq), lambda i: (0, i)),
                  pl.BlockSpec((1, nc_pad, 128), bmap),
                  pl.BlockSpec((1, 128, nc_pad), bmap),
                  pl.BlockSpec((ns, nc_pad), lambda i: (0, 0))],
        out_specs=[pl.BlockSpec((512, tq), lambda i: (0, i)),
                   pl.BlockSpec((G_NSA, ns, tq), lambda i: (0, 0, i)),
                   pl.BlockSpec((G_NSA, SLC_TOPK, tq), lambda i: (0, 0, i))],
        out_shape=[jax.ShapeDtypeStruct((512, n), F32),
                   jax.ShapeDtypeStruct((G_NSA, ns, n), F32),
                   jax.ShapeDtypeStruct((G_NSA, SLC_TOPK, n), I32)],
        compiler_params=_cparams("arbitrary"),
        name="cmp_attn",
    )(qT, pos, kc, vcT, wT)


def _post_kernel(x_ref, oa_ref, oc_ref, os_ref, ow_ref, gb_ref, g1_ref, sc2_ref, sh2_ref,
                 goa_ref, gob_ref, gpost_ref, gffn_ref, e3_ref, wout_ref, x1_ref, h2_ref):
    gates = jax.nn.sigmoid(gb_ref[...])
    ob = (_dot3(gates, e3_ref[0]) * oc_ref[...] + _dot3(gates, e3_ref[1]) * os_ref[...]
          + _dot3(gates, e3_ref[2]) * ow_ref[...])
    oa_n = _rms(oa_ref[...], goa_ref[...]).astype(BF16)
    ob_n = _rms(ob, gob_ref[...]).astype(BF16)
    o = (jnp.dot(oa_n, wout_ref[0:512, :], preferred_element_type=F32)
         + jnp.dot(ob_n, wout_ref[512:1024, :], preferred_element_type=F32))
    x1 = x_ref[...] + g1_ref[...] * _rms(o, gpost_ref[...])
    x1_ref[...] = x1
    h2_ref[...] = _rms(x1, gffn_ref[...]) * (1.0 + sc2_ref[...]) + sh2_ref[...]


def _gate_expand():
    e = np.zeros((3, 128, 512), np.float32)
    for h in range(H_NSA):
        for j in range(3):
            e[j, 3 * h + j, h * 64:(h + 1) * 64] = 1.0
    return jnp.asarray(e)


def _post(x, oa, oc, os_, ow, gb, gate1, scale2, shift2, g_oa, g_ob, g_post, g_ffn, w_out_bf):
    r, d = x.shape
    tm = min(256, r)
    per_row = gate1.shape[0] != 1
    mod = pl.BlockSpec((tm, d), lambda i: (i, 0)) if per_row else pl.BlockSpec((1, d), lambda i: (0, 0))
    row = lambda w: pl.BlockSpec((tm, w), lambda i: (i, 0))
    const = lambda shape: pl.BlockSpec(shape, lambda i: (0,) * len(shape))
    return pl.pallas_call(
        _post_kernel,
        grid=(r // tm,),
        in_specs=[row(d), row(512), row(512), row(512), row(512), row(128), mod, mod, mod,
                  const((1, 512)), const((1, 512)), const((1, d)), const((1, d)),
                  const((3, 128, 512)), const((1024, d))],
        out_specs=[row(d), row(d)],
        out_shape=[jax.ShapeDtypeStruct((r, d), F32), jax.ShapeDtypeStruct((r, d), F32)],
        compiler_params=_cparams("arbitrary"),
        name="post_attn",
    )(x, oa, oc, os_, ow, gb, gate1, scale2, shift2, g_oa.reshape(1, 512), g_ob.reshape(1, 512),
      g_post.reshape(1, d), g_ffn.reshape(1, d), _gate_expand(), w_out_bf)


def _router_kernel(h_ref, wr_ref, br_ref, tri_ref, idx_ref, wt_ref, rank_ref, cnt_ref, run_ref, *, n_valid, tm):
    i = pl.program_id(0)

    @pl.when(i == 0)
    def _():
        run_ref[...] = jnp.zeros_like(run_ref)

    s = jax.nn.sigmoid(_dot3(h_ref[...], wr_ref[...]))
    sel = s + br_ref[...]
    lane = lax.broadcasted_iota(I32, (tm, N_EXPERTS), 1)
    per = N_EXPERTS // N_EXPERT_GROUPS
    grp = jnp.right_shift(lane, 5)
    lane_g = lax.broadcasted_iota(I32, (tm, LANES), 1)
    gscore = jnp.full((tm, LANES), NEG_INF, F32)
    for g in range(N_EXPERT_GROUPS):
        v = jnp.where(grp == g, sel, NEG_INF)
        (i1, m1), (_, m2) = _topk_axis1(v, 2, lane)
        gscore = jnp.where(lane_g == g, m1 + m2, gscore)
    emask = jnp.zeros((tm, N_EXPERTS), jnp.bool_)
    for first, _ in _topk_axis1(gscore, TOPK_GROUPS, lane_g):
        emask = emask | (grp == first)
    picks = _topk_axis1(jnp.where(emask, sel, NEG_INF), TOPK_EXPERTS, lane)
    row = lax.broadcasted_iota(I32, (tm, 1), 0) + i * tm
    row_ok = row < n_valid
    onehot = jnp.zeros((tm, N_EXPERTS), F32)
    ws = []
    for first, _ in picks:
        pick = lane == first
        ws.append(jnp.sum(jnp.where(pick, s, 0.0), axis=1, keepdims=True))
        onehot = jnp.where(pick & row_ok, 1.0, onehot)
    wsum = ws[0]
    for w in ws[1:]:
        wsum = wsum + w
    excl = jnp.dot(tri_ref[...], onehot.astype(BF16), preferred_element_type=F32) + run_ref[...]
    idx_out = jnp.zeros((tm, LANES), I32)
    wt_out = jnp.zeros((tm, LANES), F32)
    rank_out = jnp.zeros((tm, LANES), I32)
    for k, (first, _) in enumerate(picks):
        rk = jnp.sum(jnp.where(lane == first, excl, 0.0), axis=1, keepdims=True)
        idx_out = jnp.where(lane_g == k, first, idx_out)
        wt_out = jnp.where(lane_g == k, ws[k] / wsum * ROUTED_SCALE, wt_out)
        rank_out = jnp.where(lane_g == k, rk.astype(I32), rank_out)
    idx_ref[...] = idx_out
    wt_ref[...] = wt_out
    rank_ref[...] = rank_out
    run_ref[...] = run_ref[...] + jnp.sum(onehot, axis=0, keepdims=True)
    cnt_ref[...] = run_ref[...]


def _router(h_all, w_router, b_router, n_valid):
    n, d = h_all.shape
    tm = 128
    tri = jnp.asarray(np.tril(np.ones((tm, tm), np.float32), -1), BF16)
    const = lambda shape: pl.BlockSpec(shape, lambda i: (0,) * len(shape))
    row = lambda w: pl.BlockSpec((tm, w), lambda i: (i, 0))
    return pl.pallas_call(
        functools.partial(_router_kernel, n_valid=n_valid, tm=tm),
        grid=(n // tm,),
        in_specs=[row(d), const((d, N_EXPERTS)), const((1, N_EXPERTS)), const((tm, tm))],
        out_specs=[row(LANES), row(LANES), row(LANES), const((1, N_EXPERTS))],
        out_shape=[jax.ShapeDtypeStruct((n, LANES), I32), jax.ShapeDtypeStruct((n, LANES), F32),
                   jax.ShapeDtypeStruct((n, LANES), I32), jax.ShapeDtypeStruct((1, N_EXPERTS), F32)],
        scratch_shapes=[pltpu.VMEM((1, N_EXPERTS), F32)],
        compiler_params=_cparams("arbitrary"),
        name="router",
    )(h_all, w_router, b_router.reshape(1, N_EXPERTS), tri)


def _gather_rows(idx_hbm, idx_smem, isem, src_hbm, buf, sem, step, n_steps, n_rows):
    def idx_copy(k, slot):
        return pltpu.make_async_copy(idx_hbm.at[k], idx_smem.at[slot], isem.at[slot])

    def issue(slot):
        def body(r, c):
            pltpu.make_async_copy(src_hbm.at[idx_smem[slot, r]], buf.at[slot, r], sem.at[slot]).start()
            return c
        lax.fori_loop(0, n_rows, body, 0, unroll=8)

    @pl.when(step == 0)
    def _():
        idx_copy(0, 0).start()
        idx_copy(0, 0).wait()
        issue(0)

        @pl.when(1 < n_steps)
        def _():
            idx_copy(1, 1).start()

    nxt = (step + 1) % 2

    @pl.when(step + 1 < n_steps)
    def _():
        idx_copy(step + 1, nxt).wait()
        issue(nxt)

    @pl.when(step + 2 < n_steps)
    def _():
        idx_copy(step + 2, step % 2).start()

    slot = step % 2

    @pl.when(step < n_steps)
    def _():
        pltpu.make_async_copy(src_hbm.at[pl.ds(0, n_rows)], buf.at[slot], sem.at[slot]).wait()
    return slot


def _expert_kernel(blk_e_ref, nused_ref, tok_hbm, h_hbm, wup_ref, wdn_ref, o_ref, buf, idx_smem, sem, isem,
                   wup_bf, wdn_bf):
    i = pl.program_id(0)
    nused = nused_ref[0]
    slot = _gather_rows(tok_hbm, idx_smem, isem, h_hbm, buf, sem, i, nused, ROW_BLOCK)

    new_expert = (i == 0) | (blk_e_ref[i] != blk_e_ref[jnp.maximum(i - 1, 0)])

    @pl.when(new_expert & (i < nused))
    def _():
        wup_bf[...] = wup_ref[0].astype(BF16)
        wdn_bf[...] = wdn_ref[0].astype(BF16)

    @pl.when(i < nused)
    def _():
        x = jnp.concatenate([buf[slot, :, s, :] for s in range(8)], axis=1).astype(BF16)
        up = jnp.dot(x, wup_bf[...], preferred_element_type=F32)
        half = up.shape[1] // 2
        act = (jax.nn.silu(up[:, :half]) * up[:, half:]).astype(BF16)
        y = jnp.dot(act, wdn_bf[...], preferred_element_type=F32)
        for s in range(8):
            o_ref[:, s, :] = y[:, s * 128:(s + 1) * 128]

    @pl.when(i >= nused)
    def _():
        o_ref[...] = jnp.zeros_like(o_ref)


def _experts(h3, blk_e, row_tok, nused, w_exp_up, w_exp_down):
    nblk = blk_e.shape[0]
    rb = ROW_BLOCK
    e, d, f2 = w_exp_up.shape
    grid_spec = pltpu.PrefetchScalarGridSpec(
        num_scalar_prefetch=2,
        grid=(nblk,),
        in_specs=[pl.BlockSpec(memory_space=pl.ANY), pl.BlockSpec(memory_space=pl.ANY),
                  pl.BlockSpec((1, d, f2), lambda i, be, nu: (be[i], 0, 0)),
                  pl.BlockSpec((1, f2 // 2, d), lambda i, be, nu: (be[i], 0, 0))],
        out_specs=pl.BlockSpec((rb, 8, 128), lambda i, be, nu: (i, 0, 0)),
        scratch_shapes=[pltpu.VMEM((2, rb, 8, 128), F32), pltpu.SMEM((2, rb), I32),
                        pltpu.SemaphoreType.DMA((2,)), pltpu.SemaphoreType.DMA((2,)),
                        pltpu.VMEM((d, f2), BF16), pltpu.VMEM((f2 // 2, d), BF16)])
    return pl.pallas_call(
        _expert_kernel,
        grid_spec=grid_spec,
        out_shape=jax.ShapeDtypeStruct((nblk * rb, 8, 128), F32),
        compiler_params=_cparams("arbitrary"),
        name="experts",
    )(blk_e, nused, row_tok.reshape(nblk, rb), h3, w_exp_up, w_exp_down)


def _combine_kernel(dest_hbm, y_hbm, wt_ref, h_ref, x1_ref, g2_ref, gpost_ref, wsu_ref, wsd_ref, o_ref,
                    buf, idx_smem, sem, isem, *, tm):
    i = pl.program_id(0)
    slot = _gather_rows(dest_hbm, idx_smem, isem, y_hbm, buf, sem, i, pl.num_programs(0), tm * TOPK_EXPERTS)
    wt = wt_ref[...]
    f = jnp.zeros((tm, h_ref.shape[1]), F32)
    for k in range(TOPK_EXPERTS):
        yk = jnp.concatenate([buf[slot, pl.ds(k, tm, stride=8), s, :] for s in range(8)], axis=1)
        f = f + yk * wt[:, k:k + 1]
    hb = h_ref[...].astype(BF16)
    up = jnp.dot(hb, wsu_ref[...], preferred_element_type=F32)
    half = up.shape[1] // 2
    act = (jax.nn.silu(up[:, :half]) * up[:, half:]).astype(BF16)
    f = f + jnp.dot(act, wsd_ref[...], preferred_element_type=F32)
    o_ref[...] = x1_ref[...] + g2_ref[...] * _rms(f, gpost_ref[...])


def _combine(dest_flat, y3, wt, h_all, x1_all, gate2_rows, g_ffn_post, w_sh_up_bf, w_sh_down_bf):
    n, d = h_all.shape
    tm = 128
    const = lambda shape: pl.BlockSpec(shape, lambda i: (0,) * len(shape))
    row = lambda w: pl.BlockSpec((tm, w), lambda i: (i, 0))
    na = tm * TOPK_EXPERTS
    return pl.pallas_call(
        functools.partial(_combine_kernel, tm=tm),
        grid=(n // tm,),
        in_specs=[pl.BlockSpec(memory_space=pl.ANY), pl.BlockSpec(memory_space=pl.ANY), row(LANES), row(d), row(d),
                  row(d), const((1, d)), const(w_sh_up_bf.shape), const(w_sh_down_bf.shape)],
        out_specs=row(d),
        scratch_shapes=[pltpu.VMEM((2, na, 8, 128), F32), pltpu.SMEM((2, na), I32),
                        pltpu.SemaphoreType.DMA((2,)), pltpu.SemaphoreType.DMA((2,))],
        out_shape=jax.ShapeDtypeStruct((n, d), F32),
        compiler_params=_cparams("arbitrary"),
        name="moe_combine",
    )(dest_flat.reshape(n // tm, na), y3, wt, h_all, x1_all, gate2_rows, g_ffn_post.reshape(1, d),
      w_sh_up_bf, w_sh_down_bf)


def _moe(h_all, x1_all, gate2_rows, n_valid, w_router, b_router, w_exp_up, w_exp_down, w_sh_up, w_sh_down, g_ffn_post):
    n, d = h_all.shape
    rb = ROW_BLOCK
    e = N_EXPERTS
    idx_p, wt_p, rank_p, cnt = _router(h_all, w_router, b_router, n_valid)
    idx = idx_p[:n_valid, :TOPK_EXPERTS]
    rank = rank_p[:n_valid, :TOPK_EXPERTS]
    counts = cnt[0].astype(I32)
    padded = (counts + rb - 1) // rb * rb
    p_end = jnp.cumsum(padded)
    p_start = p_end - padded
    dest = (p_start[idx] + rank).astype(I32)
    nblk = -(-(n_valid * TOPK_EXPERTS + e * (rb - 1)) // rb)
    tok = jnp.repeat(jnp.arange(n_valid, dtype=I32), TOPK_EXPERTS)
    row_tok = jnp.zeros((nblk * rb,), I32).at[dest.reshape(-1)].set(tok)
    blk_e = jnp.clip(jnp.searchsorted(p_end, jnp.arange(nblk, dtype=I32) * rb, side="right"), 0, e - 1).astype(I32)
    nused = (p_end[-1] // rb).astype(I32).reshape(1)
    y3 = _experts(h_all.reshape(n, 8, 128), blk_e, row_tok, nused, w_exp_up, w_exp_down)
    dest_pad = jnp.zeros((n, TOPK_EXPERTS), I32).at[:n_valid].set(dest).reshape(-1)
    return _combine(dest_pad, y3, wt_p, h_all, x1_all, gate2_rows, g_ffn_post,
                    w_sh_up.astype(BF16), w_sh_down.astype(BF16))


def _kmean_pages_kernel(pt_ref, *refs):
    pages, o_ref = refs[:-1], refs[-1]
    s_idx = pl.program_id(1)
    per_step = len(pages) // 2

    @pl.when(s_idx == 0)
    def _():
        o_ref[...] = jnp.zeros_like(o_ref)

    ones = jnp.ones((PAGE_SIZE, LANES), BF16)
    lane = lax.broadcasted_iota(I32, (512, LANES), 1)
    acc = o_ref[0]
    for u in range(per_step):
        tot = jnp.zeros((512, LANES), F32)
        for pg in (pages[2 * u], pages[2 * u + 1]):
            hi, lo = _split_bf16(pg[0])
            tot = tot + (jnp.dot(hi, ones, preferred_element_type=F32) + jnp.dot(lo, ones, preferred_element_type=F32))
        acc = jnp.where(lane == s_idx * per_step + u, tot * (1.0 / MOBA_BLOCK), acc)
    o_ref[0] = acc


def _kmean_pages(cache_a, page_table):
    bs, n_pages = page_table.shape
    pps = PAGES_PER_STEP
    assert n_pages // 2 <= LANES
    in_specs = [pl.BlockSpec((1, 512, PAGE_SIZE), lambda b, s, pt, u=u: (pt[b, s * pps + u], 0, 0)) for u in range(pps)]
    grid_spec = pltpu.PrefetchScalarGridSpec(
        num_scalar_prefetch=1, grid=(bs, n_pages // pps), in_specs=in_specs,
        out_specs=pl.BlockSpec((1, 512, LANES), lambda b, s, pt: (b, 0, 0)))
    return pl.pallas_call(
        _kmean_pages_kernel, grid_spec=grid_spec,
        out_shape=jax.ShapeDtypeStruct((bs, 512, LANES), F32),
        compiler_params=_cparams("arbitrary", "arbitrary"),
        name="kmean_pages",
    )(page_table, *([cache_a] * pps))


def _moba_sample_kernel(pt_ref, qbd_ref, km_ref, new_ref, *refs, nb, ts, past):
    pps = PAGES_PER_STEP
    pages = refs[:pps]
    o_ref, sel_scr, m_scr, l_scr, acc_scr = refs[pps:]
    s_idx = pl.program_id(1)
    rows = H_MOBA * ts
    qbd = qbd_ref[0]
    qb = qbd.astype(BF16)
    rowi = lax.broadcasted_iota(I32, (rows, 1), 0)
    slope = jnp.zeros((rows, 1), F32)
    for h in range(H_MOBA):
        slope = jnp.where(_div_pow2(rowi, ts) == h, SLOPES_A[h], slope)
    lane = lax.broadcasted_iota(I32, (rows, LANES), 1)

    @pl.when(s_idx == 0)
    def _():
        gate = _dot3(qbd, km_ref[0])
        gate = jnp.where(lane < nb, gate, NEG_INF)
        sel = jnp.zeros((rows, LANES), F32)
        for first, m in _topk_axis1(gate, MOBA_TOPK, lane):
            sel = jnp.where((lane == first) & (m > NEG_INF), 1.0, sel)
        sel_scr[...] = sel
        m_scr[...] = jnp.full_like(m_scr, M_INIT)
        l_scr[...] = jnp.zeros_like(l_scr)
        acc_scr[...] = jnp.zeros_like(acc_scr)

    def update(s, vT_bf):
        m = m_scr[...]
        m_new = jnp.maximum(m, jnp.max(s, axis=1, keepdims=True))
        p = jnp.exp(s - m_new)
        alpha = jnp.exp(m - m_new)
        l_scr[...] = l_scr[...] * alpha + jnp.sum(p, axis=1, keepdims=True)
        acc_scr[...] = acc_scr[...] * alpha + lax.dot_general(p.astype(BF16), vT_bf, _NT, preferred_element_type=F32)
        m_scr[...] = m_new

    sel = sel_scr[...]
    for u in range(pps):
        page = s_idx * pps + u
        flag = jnp.sum(jnp.where(lane == page // 2, sel, 0.0), axis=1, keepdims=True) > 0.0
        kT = pages[u][0, 0:512, :].astype(BF16)
        vT = pages[u][0, 512:1024, :].astype(BF16)
        s = jnp.dot(qb, kT, preferred_element_type=F32)
        kpos = page * PAGE_SIZE + lane - past
        s = jnp.where(flag, s + slope * kpos.astype(F32), NEG_INF)
        update(s, vT)

    @pl.when(s_idx == pl.num_programs(1) - 1)
    def _():
        kn = new_ref[0, 0:512, :].astype(BF16)
        vn = new_ref[0, 512:1024, :].astype(BF16)
        s = jnp.dot(qb, kn, preferred_element_type=F32)
        ok = (lane < ts) & (lane <= _mod_pow2(rowi, ts))
        s = jnp.where(ok, s + slope * lane.astype(F32), NEG_INF)
        update(s, vn)
        o = acc_scr[...] / jnp.maximum(l_scr[...], 1e-30)
        col_h = _div_pow2(lax.broadcasted_iota(I32, (rows, 512), 1), D_HEAD)
        o = jnp.where(col_h == _div_pow2(rowi, ts), o, 0.0)
        pick_r = lax.broadcasted_iota(I32, (8, rows), 0)
        pick_c = lax.broadcasted_iota(I32, (8, rows), 1)
        gather = jnp.where(_mod_pow2(pick_c, ts) == pick_r, 1.0, 0.0)
        o_ref[0] = _dot3(gather, o)


def _moba_sample(qbd, km_pad, new_pad, cache_a, page_table, *, nb, ts, past):
    bs, n_pages = page_table.shape
    pps = PAGES_PER_STEP
    rows = H_MOBA * ts
    in_specs = [pl.BlockSpec((1, rows, 512), lambda b, s, pt: (b, 0, 0)),
                pl.BlockSpec((1, 512, LANES), lambda b, s, pt: (b, 0, 0)),
                pl.BlockSpec((1, 1024, LANES), lambda b, s, pt: (b, 0, 0))]
    in_specs += [pl.BlockSpec((1, 1024, PAGE_SIZE), lambda b, s, pt, u=u: (pt[b, s * pps + u], 0, 0)) for u in range(pps)]
    grid_spec = pltpu.PrefetchScalarGridSpec(
        num_scalar_prefetch=1, grid=(bs, n_pages // pps), in_specs=in_specs,
        out_specs=pl.BlockSpec((1, 8, 512), lambda b, s, pt: (b, 0, 0)),
        scratch_shapes=[pltpu.VMEM((rows, LANES), F32), pltpu.VMEM((rows, 1), F32), pltpu.VMEM((rows, 1), F32),
                        pltpu.VMEM((rows, 512), F32)])
    return pl.pallas_call(
        functools.partial(_moba_sample_kernel, nb=nb, ts=ts, past=past),
        grid_spec=grid_spec,
        out_shape=jax.ShapeDtypeStruct((bs, 8, 512), F32),
        compiler_params=_cparams("arbitrary", "arbitrary"),
        name="moba_sample",
    )(page_table, qbd, km_pad, new_pad, *([cache_a] * pps))


def _nsa_sample_kernel(pt_ref, idx_ref, qs_ref, qw_ref, kpos_ref, new_ref, win_ref, wnew_ref, cache_hbm,
                       os_ref, ow_ref, buf, sem, *, ts, past, n_slots):
    b = pl.program_id(0)
    ngq = G_NSA * ts
    copies = []
    for gq in range(ngq):
        for t in range(n_slots):
            blk = jnp.maximum(idx_ref[(b * ngq + gq) * SLC_TOPK + t], 0)
            page = pt_ref[b, blk // 2]
            cp = pltpu.make_async_copy(cache_hbm.at[page, pl.ds(256, 256), :],
                                       buf.at[gq, :, pl.ds(t * PAGE_SIZE, PAGE_SIZE)], sem.at[0])
            cp.start()
            copies.append(cp)
    new = new_ref[0]
    for gq in range(ngq):
        buf[gq, :, pl.ds(n_slots * PAGE_SIZE, PAGE_SIZE)] = new
    for cp in copies:
        cp.wait()

    rowi = lax.broadcasted_iota(I32, (8, 1), 0)
    for gq in range(ngq):
        g, q = divmod(gq, ts)
        slope = jnp.zeros((8, 1), F32)
        for r in range(R_NSA):
            slope = jnp.where(rowi == r, SLOPES_B[g * R_NSA + r], slope)
        kT = buf[gq, 0:128, :].astype(BF16)
        vT = buf[gq, 128:256, :].astype(BF16)
        s = jnp.dot(qs_ref[0, gq].astype(BF16), kT, preferred_element_type=F32)
        kpos = kpos_ref[0, gq]
        ok = (kpos >= 0) & (kpos <= past + q)
        s = jnp.where(ok, s + slope * (kpos - past).astype(F32), NEG_INF)
        m = jnp.max(s, axis=1, keepdims=True)
        m = jnp.where(m > NEG_INF, m, 0.0)
        p = jnp.exp(s - m)
        p = p * (1.0 / jnp.maximum(jnp.sum(p, axis=1, keepdims=True), 1e-30))
        os_ref[0, gq] = lax.dot_general(p.astype(BF16), vT, _NT, preferred_element_type=F32)

    nw = WINDOW + LANES
    kw_all = jnp.concatenate([win_ref[0], wnew_ref[0]], axis=1)
    kw = kw_all[0:128, :].astype(BF16)
    vw = kw_all[128:256, :].astype(BF16)
    wrow = lax.broadcasted_iota(I32, (R_NSA * ts, 1), 0)
    wlane = lax.broadcasted_iota(I32, (R_NSA * ts, nw), 1)
    dist = (WINDOW + _mod_pow2(wrow, ts)) - wlane
    okw = (dist >= 0) & (dist <= WINDOW) & (wlane < WINDOW + ts)
    for g in range(G_NSA):
        slope = jnp.zeros((R_NSA * ts, 1), F32)
        for r in range(R_NSA):
            slope = jnp.where(_div_pow2(wrow, ts) == r, SLOPES_B[g * R_NSA + r], slope)
        s = jnp.dot(qw_ref[0, g].astype(BF16), kw, preferred_element_type=F32)
        s = jnp.where(okw, s - slope * dist.astype(F32), NEG_INF)
        m = jnp.max(s, axis=1, keepdims=True)
        m = jnp.where(m > NEG_INF, m, 0.0)
        p = jnp.exp(s - m)
        p = p * (1.0 / jnp.maximum(jnp.sum(p, axis=1, keepdims=True), 1e-30))
        ow_ref[0, g] = lax.dot_general(p.astype(BF16), vw, _NT, preferred_element_type=F32)


def _nsa_sample(idx_flat, qs, qw, kpos, new_pad, win, wnew_pad, cache_b, page_table, *, ts, past, n_slots):
    bs = page_table.shape[0]
    ngq = G_NSA * ts
    nk = (n_slots + 1) * PAGE_SIZE
    m4 = lambda shape: pl.BlockSpec(shape, lambda b, pt, ix: (b,) + (0,) * (len(shape) - 1))
    grid_spec = pltpu.PrefetchScalarGridSpec(
        num_scalar_prefetch=2, grid=(bs,),
        in_specs=[m4((1, ngq, 8, 128)), m4((1, G_NSA, R_NSA * ts, 128)), m4((1, ngq, 1, nk)),
                  m4((1, 256, LANES)), m4((1, 256, WINDOW)), m4((1, 256, LANES)),
                  pl.BlockSpec(memory_space=pl.ANY)],
        out_specs=[m4((1, ngq, 8, 128)), m4((1, G_NSA, R_NSA * ts, 128))],
        scratch_shapes=[pltpu.VMEM((ngq, 256, nk), F32), pltpu.SemaphoreType.DMA((1,))])
    return pl.pallas_call(
        functools.partial(_nsa_sample_kernel, ts=ts, past=past, n_slots=n_slots),
        grid_spec=grid_spec,
        out_shape=[jax.ShapeDtypeStruct((bs, ngq, 8, 128), F32),
                   jax.ShapeDtypeStruct((bs, G_NSA, R_NSA * ts, 128), F32)],
        compiler_params=_cparams("arbitrary"),
        name="nsa_sample",
    )(page_table, idx_flat, qs, qw, kpos, new_pad, win, wnew_pad, cache_b)


def _vT_blocks(v, tk):
    t, c = v.shape
    return v.astype(BF16).reshape(t // tk, tk, c // 128, 128).transpose(2, 0, 3, 1)


def _prompt_mixer(proj, cw):
    qa, kva, qb, kvb, kvw, _ = proj
    t = qa.shape[0]
    qaT = (qa * Q_SCALE).T
    qbT = (qb * Q_SCALE).T
    kmean = _kmean_prompt(kva)
    sel_a = _moba_gate_prompt(qaT, kmean)
    oaT = _flash_select(qaT, kva[:, :512].astype(BF16), _vT_blocks(kva[:, 512:], TKF), sel_a, HEADS_MOBA,
                        block=MOBA_BLOCK)
    cmp_out = _compress(kvb, None, cw, n_rows=t, n_batch=1)
    kc = cmp_out[:, :, 0:128].astype(BF16)
    vcT = cmp_out[:, :, 128:256].astype(BF16).transpose(0, 2, 1)
    pos = jnp.arange(t, dtype=I32).reshape(1, t)
    ocT, sel_b, _ = _cmp_attn(qbT, pos, kc, vcT, t // SLC_BLOCK, k_sel=SLC_TOPK, tq=TQ, tiles_per_batch=0)
    osT = _flash_select(qbT, kvb[:, 256:384].astype(BF16), _vT_blocks(kvb[:, 384:512], TKF), sel_b, HEADS_NSA,
                        block=SLC_BLOCK)
    owT = _win_prompt(qbT, kvw[:, 0:128].astype(BF16), _vT_blocks(kvw[:, 128:256], TK)[0])
    return oaT.T, ocT.T, osT.T, owT.T


def _sample_mixer(proj, cache_a, cache_b, win_state, page_table, cw, bs, ts):
    qa, kva, qb, kvb, kvw, _ = proj
    n_pages = page_table.shape[1]
    past = n_pages * PAGE_SIZE
    nb = past // MOBA_BLOCK
    new_rows_T = lambda a: jnp.zeros((bs, LANES, a.shape[-1]), F32).at[:, :ts].set(a).transpose(0, 2, 1)
    km = _kmean_pages(cache_a, page_table)
    q4 = (qa * Q_SCALE).reshape(bs, ts, H_MOBA, D_HEAD).transpose(0, 2, 1, 3)
    eye = jnp.eye(H_MOBA, dtype=F32)
    qbd = (q4[:, :, :, None, :] * eye[None, :, None, :, None]).reshape(bs, H_MOBA * ts, 512)
    new_a = new_rows_T(kva.reshape(bs, ts, 1024))
    oa = _moba_sample(qbd, km, new_a, cache_a, page_table, nb=nb, ts=ts, past=past)[:, :ts]
    oa = oa.reshape(bs * ts, 512)
    cmp_out = _compress(cache_b, page_table, cw, n_rows=past, n_batch=bs)
    kc = cmp_out[:, :, 0:128].astype(BF16)
    vcT = cmp_out[:, :, 128:256].astype(BF16).transpose(0, 2, 1)
    qbs = (qb * Q_SCALE).reshape(bs, ts, 512)
    qT = jnp.zeros((bs, LANES, 512), F32).at[:, :ts].set(qbs).reshape(bs * LANES, 512).T
    pos = jnp.broadcast_to(past + jnp.minimum(jnp.arange(LANES, dtype=I32), ts - 1), (bs, LANES)).reshape(1, bs * LANES)
    n_slots = SLC_TOPK - 1
    ocT, _, idx = _cmp_attn(qT, pos, kc, vcT, past // SLC_BLOCK, k_sel=n_slots, tq=LANES, tiles_per_batch=1)
    oc = ocT.T.reshape(bs, LANES, 512)[:, :ts].reshape(bs * ts, 512)
    idx = idx.reshape(G_NSA, SLC_TOPK, bs, LANES)[:, :, :, :ts].transpose(2, 0, 3, 1)
    q5 = qbs.reshape(bs, ts, G_NSA, R_NSA, D_HEAD)
    lane_g = jnp.eye(G_NSA, dtype=F32)
    qsel = q5.transpose(0, 2, 1, 3, 4)[:, :, :, :, None, :] * lane_g[None, :, None, None, :, None]
    qs = jnp.zeros((bs, G_NSA, ts, 8, 128), F32).at[:, :, :, :R_NSA].set(qsel.reshape(bs, G_NSA, ts, R_NSA, 128))
    qs = qs.reshape(bs, G_NSA * ts, 8, 128)
    qwin = q5.transpose(0, 2, 3, 1, 4)[:, :, :, :, None, :] * lane_g[None, :, None, None, :, None]
    qw = qwin.reshape(bs, G_NSA, R_NSA * ts, 128)
    sl = idx[..., :n_slots, None]
    lane = jnp.arange(PAGE_SIZE, dtype=I32)
    in_blk = (sl >= 0) & ((lane // SLC_BLOCK) == (sl % 2))
    slot_pos = jnp.where(in_blk, (sl // 2) * PAGE_SIZE + lane, -1).reshape(bs, G_NSA, ts, n_slots * PAGE_SIZE)
    own = jnp.where(lane < ts, past + lane, -1)
    kpos = jnp.concatenate([slot_pos, jnp.broadcast_to(own, (bs, G_NSA, ts, PAGE_SIZE))], axis=-1)
    kpos = kpos.reshape(bs, G_NSA * ts, 1, (n_slots + 1) * PAGE_SIZE).astype(I32)
    new_b = new_rows_T(kvb.reshape(bs, ts, 512)[:, :, 256:])
    wnew = new_rows_T(kvw.reshape(bs, ts, 256))
    os_raw, ow_raw = _nsa_sample(idx.reshape(-1).astype(I32), qs, qw, kpos, new_b, win_state, wnew, cache_b,
                                 page_table, ts=ts, past=past, n_slots=n_slots)
    os5 = os_raw.reshape(bs, G_NSA, ts, 8, G_NSA, D_HEAD)[:, :, :, :R_NSA]
    os_ = jnp.stack([os5[:, g, :, :, g] for g in range(G_NSA)], axis=2).reshape(bs * ts, 512)
    ow5 = ow_raw.reshape(bs, G_NSA, R_NSA, ts, G_NSA, D_HEAD)
    ow = jnp.stack([ow5[:, g, :, :, g] for g in range(G_NSA)], axis=1)
    ow = ow.transpose(0, 3, 1, 2, 4).reshape(bs * ts, 512)
    return oa, oc, os_, ow


def _layer(x_p, x_s, cache_a, cache_b, win_state, page_table, c_p, c_s, w):
    (w_ada, b_ada, g_mix_pre, g_mix_post, g_ffn_pre, g_ffn_post, w_in, g_out_moba, g_out_nsa, w_out,
     cmp_pe, cmp_w1, cmp_b1, cmp_w2, cmp_b2, w_router, b_router, w_exp_up, w_exp_down, w_sh_up, w_sh_down) = w
    t, d = x_p.shape
    bs, ts, _ = x_s.shape
    assert c_p.shape[0] == 1 and t % (8 * MOBA_BLOCK) == 0 and win_state.shape[1] == WINDOW
    n_s = bs * ts
    rows = -(-(1 + bs) // 8) * 8
    c_all = jnp.zeros((rows, d), F32).at[0:1].set(c_p).at[1:1 + bs].set(c_s)
    mod = _ada(c_all, w_ada, b_ada).reshape(rows, 6, d)
    mod_p = [mod[0:1, i] for i in range(6)]
    mod_s = [jnp.repeat(mod[1:1 + bs, i], ts, axis=0) for i in range(6)]

    w_pad = jnp.zeros((d, _PROJ_CUTS[-1]), F32).at[:, :w_in.shape[1]].set(w_in).astype(BF16)
    cw = _compress_weights(cmp_pe, cmp_w1, cmp_b1, cmp_w2, cmp_b2)
    w_out_bf = w_out.astype(BF16)

    proj_p = _inproj(x_p, mod_p[1], mod_p[0], g_mix_pre, w_pad)
    proj_s = _inproj(x_s.reshape(n_s, d), mod_s[1], mod_s[0], g_mix_pre, w_pad)

    o_p = _prompt_mixer(proj_p, cw)
    win_t = win_state.transpose(0, 2, 3, 1).reshape(bs, 2 * G_NSA * D_HEAD, WINDOW)
    o_s = _sample_mixer(proj_s, cache_a, cache_b, win_t, page_table, cw, bs, ts)

    x1_p, h2_p = _post(x_p, *o_p, proj_p[5], mod_p[2], mod_p[4], mod_p[3], g_out_moba, g_out_nsa, g_mix_post,
                       g_ffn_pre, w_out_bf)
    x1_s, h2_s = _post(x_s.reshape(n_s, d), *o_s, proj_s[5], mod_s[2], mod_s[4], mod_s[3], g_out_moba, g_out_nsa,
                       g_mix_post, g_ffn_pre, w_out_bf)

    n_valid = t + n_s
    n_all = -(-n_valid // 128) * 128
    pad = lambda a: jnp.concatenate([a, jnp.zeros((n_all - n_valid, d), F32)], axis=0) if n_all > n_valid else a
    h_all = pad(jnp.concatenate([h2_p, h2_s], axis=0))
    x1_all = pad(jnp.concatenate([x1_p, x1_s], axis=0))
    g2_rows = pad(jnp.concatenate([jnp.broadcast_to(mod_p[5], (t, d)), mod_s[5]], axis=0))
    y_all = _moe(h_all, x1_all, g2_rows, n_valid, w_router, b_router, w_exp_up, w_exp_down, w_sh_up, w_sh_down,
                 g_ffn_post)
    y_p = y_all[:t]
    y_s = y_all[t:n_valid].reshape(bs, ts, d)

    _, kva_p, _, kvb_p, kvw_p, _ = proj_p
    _, kva_s, _, kvb_s, kvw_s, _ = proj_s
    state_p = (kva_p.reshape(1, t, 2 * H_MOBA, D_HEAD), kvb_p.reshape(1, t, 4 * G_NSA, D_HEAD),
               kvw_p[t - min(WINDOW, t):].reshape(1, min(WINDOW, t), 2 * G_NSA, D_HEAD))
    win_new = jnp.concatenate([win_state, kvw_s.reshape(bs, ts, 2 * G_NSA, D_HEAD)], axis=1)[:, -WINDOW:]
    state_s = (kva_s.reshape(bs, ts, 2 * H_MOBA, D_HEAD), kvb_s.reshape(bs, ts, 4 * G_NSA, D_HEAD), win_new)
    return y_p, y_s, state_p, state_s


def kernel(x_prompt, x_sample, cache_moba, cache_nsa, state_nsa_win, page_table, c_prompt, c_sample, w_ada, b_ada, g_mix_pre, g_mix_post, g_ffn_pre, g_ffn_post, w_in, g_out_moba, g_out_nsa, w_out, cmp_pe, cmp_w1, cmp_b1, cmp_w2, cmp_b2, w_router, b_router, w_exp_up, w_exp_down, w_sh_up, w_sh_down):
    weights = (w_ada, b_ada, g_mix_pre, g_mix_post, g_ffn_pre, g_ffn_post, w_in, g_out_moba, g_out_nsa, w_out,
               cmp_pe, cmp_w1, cmp_b1, cmp_w2, cmp_b2, w_router, b_router, w_exp_up, w_exp_down, w_sh_up, w_sh_down)
    depth = w_ada.shape[0]
    n_pool = cache_moba.shape[1]
    y_p, y_s = x_prompt[0], x_sample
    st_p, st_s = [], []
    for layer in range(depth):
        w_l = tuple(w[layer] for w in weights)
        cache_a = cache_moba[layer].transpose(0, 2, 3, 1).reshape(n_pool, 2 * H_MOBA * D_HEAD, PAGE_SIZE)
        cache_b = cache_nsa[layer].transpose(0, 2, 3, 1).reshape(n_pool, 4 * G_NSA * D_HEAD, PAGE_SIZE)
        y_p, y_s, sp, ss = _layer(y_p, y_s, cache_a, cache_b, state_nsa_win[layer], page_table, c_prompt, c_sample, w_l)
        st_p.append(sp)
        st_s.append(ss)
    stack = lambda sts, i: jnp.stack([s[i] for s in sts])
    return (y_p[None], y_s, stack(st_p, 0), stack(st_p, 1), stack(st_p, 2), stack(st_s, 0), stack(st_s, 1), stack(st_s, 2))
```

```python
import functools

import numpy as np
import jax
import jax.numpy as jnp
from jax import lax
from jax.experimental import pallas as pl
from jax.experimental.pallas import tpu as pltpu

F32, BF16, I32 = jnp.float32, jnp.bfloat16, jnp.int32
NEG_INF = float("-inf")
POS_INF = float("inf")
M_INIT = -1e30

D_HEAD = 64
H_MOBA = 8
H_NSA = 8
G_NSA = 2
R_NSA = H_NSA // G_NSA
MOBA_BLOCK = 256
MOBA_TOPK = 3
CMP_STRIDE = 16
CMP_LEN = 2 * CMP_STRIDE
CMP_HIDDEN = 2 * D_HEAD
SLC_BLOCK = 64
SLC_TOPK = 16
SLC_INIT = 1
SLC_LOCAL = 2
WINDOW = 512
N_EXPERTS = 256
TOPK_EXPERTS = 8
N_EXPERT_GROUPS = 8
TOPK_GROUPS = 4
ROUTED_SCALE = 2.5
PAGE_SIZE = 128
EPS = 1e-6
Q_SCALE = D_HEAD ** -0.5
LOG2E = 1.4426950408889634

_SLOPES = [2.0 ** (-8.0 * i / (H_MOBA + H_NSA)) for i in range(1, H_MOBA + H_NSA + 1)]
SLOPES_A = _SLOPES[0::2][:H_MOBA]
SLOPES_B = _SLOPES[1::2][:H_NSA]

LANES = 128
TQ = 256
TK = 256
TKF = 128
VMEM_LIMIT_BYTES = 56 * 1024 * 1024
ROW_BLOCK = 128
PAGES_PER_STEP = 16


def _cparams(*sem):
    return pltpu.CompilerParams(dimension_semantics=sem, vmem_limit_bytes=VMEM_LIMIT_BYTES)


def _rms(x, g):
    return x * lax.rsqrt(jnp.mean(x * x, axis=-1, keepdims=True) + EPS) * g


def _split_bf16(a):
    hi = a.astype(BF16)
    lo = (a - hi.astype(F32)).astype(BF16)
    return hi, lo


def _dot3(a, b, dims=None):
    ah, al = _split_bf16(a)
    bh, bl = _split_bf16(b)
    if dims is None:
        d = lambda x, y: jnp.dot(x, y, preferred_element_type=F32)
    else:
        d = lambda x, y: lax.dot_general(x, y, dims, preferred_element_type=F32)
    return d(ah, bh) + (d(ah, bl) + d(al, bh))


_NT = (((1,), (1,)), ((), ()))


def _div_pow2(x, n):
    assert n & (n - 1) == 0
    return jnp.right_shift(x, n.bit_length() - 1)


def _mod_pow2(x, n):
    assert n & (n - 1) == 0
    return jnp.bitwise_and(x, n - 1)


def _head_rows(q64, half):
    z = jnp.zeros_like(q64)
    return jnp.concatenate([z, q64] if half else [q64, z], axis=0)


def _ada_kernel(c_ref, w_ref, b_ref, o_ref):
    a = jax.nn.silu(c_ref[...])
    o_ref[...] = _dot3(a, w_ref[...]) + b_ref[...]


def _ada(c, w_ada, b_ada):
    r, d = c.shape
    n = w_ada.shape[1]
    tn = 768
    return pl.pallas_call(
        _ada_kernel,
        grid=(n // tn,),
        in_specs=[pl.BlockSpec((r, d), lambda j: (0, 0)),
                  pl.BlockSpec((d, tn), lambda j: (0, j)),
                  pl.BlockSpec((1, tn), lambda j: (0, j))],
        out_specs=pl.BlockSpec((r, tn), lambda j: (0, j)),
        out_shape=jax.ShapeDtypeStruct((r, n), F32),
        compiler_params=_cparams("arbitrary"),
        name="ada",
    )(c, w_ada, b_ada.reshape(1, n))


_PROJ_CUTS = (0, 512, 1536, 2048, 2560, 2816, 2944)


def _inproj_kernel(x_ref, sc_ref, sh_ref, g_ref, w_ref, *out_refs):
    h = _rms(x_ref[...], g_ref[...]) * (1.0 + sc_ref[...]) + sh_ref[...]
    hb = h.astype(BF16)
    for o_ref, a, b in zip(out_refs, _PROJ_CUTS[:-1], _PROJ_CUTS[1:]):
        o_ref[...] = jnp.dot(hb, w_ref[:, a:b], preferred_element_type=F32)


def _inproj(x, scale, shift, g, w_pad):
    r, d = x.shape
    tm = min(512, r)
    per_row = scale.shape[0] != 1
    mod_spec = pl.BlockSpec((tm, d), lambda i: (i, 0)) if per_row else pl.BlockSpec((1, d), lambda i: (0, 0))
    widths = [b - a for a, b in zip(_PROJ_CUTS[:-1], _PROJ_CUTS[1:])]
    return pl.pallas_call(
        _inproj_kernel,
        grid=(r // tm,),
        in_specs=[pl.BlockSpec((tm, d), lambda i: (i, 0)), mod_spec, mod_spec,
                  pl.BlockSpec((1, d), lambda i: (0, 0)),
                  pl.BlockSpec(w_pad.shape, lambda i: (0, 0))],
        out_specs=[pl.BlockSpec((tm, w), lambda i: (i, 0)) for w in widths],
        out_shape=[jax.ShapeDtypeStruct((r, w), F32) for w in widths],
        compiler_params=_cparams("arbitrary"),
        name="inproj",
    )(x, scale, shift, g.reshape(1, d), w_pad)


def _topk_axis0(score, k, idx):
    n = score.shape[0]
    sel = jnp.zeros(score.shape, F32)
    picks = []
    for _ in range(k):
        m = jnp.max(score, axis=0, keepdims=True)
        first = jnp.min(jnp.where(score == m, idx, n), axis=0, keepdims=True)
        pick = idx == first
        ok = m > NEG_INF
        sel = jnp.where(pick & ok, 1.0, sel)
        picks.append(jnp.where(ok, first, -1))
        score = jnp.where(pick, NEG_INF, score)
    return sel, picks


def _topk_axis1(score, k, idx):
    n = score.shape[1]
    out = []
    for _ in range(k):
        m = jnp.max(score, axis=1, keepdims=True)
        first = jnp.min(jnp.where(score == m, idx, n), axis=1, keepdims=True)
        out.append((first, m))
        score = jnp.where(idx == first, NEG_INF, score)
    return out


def _kmean_kernel(k_ref, o_ref):
    x = k_ref[...]
    n = x.shape[0] // MOBA_BLOCK
    o_ref[...] = jnp.sum(x.reshape(n, MOBA_BLOCK, x.shape[1]), axis=1) * (1.0 / MOBA_BLOCK)


def _kmean_prompt(kva):
    t = kva.shape[0]
    nb = t // MOBA_BLOCK
    per = 8
    return pl.pallas_call(
        _kmean_kernel,
        grid=(nb // per,),
        in_specs=[pl.BlockSpec((per * MOBA_BLOCK, 512), lambda i: (i, 0))],
        out_specs=pl.BlockSpec((per, 512), lambda i: (i, 0)),
        out_shape=jax.ShapeDtypeStruct((nb, 512), F32),
        compiler_params=_cparams("arbitrary"),
        name="kmean_prompt",
    )(kva)


def _moba_gate_kernel(qT_ref, km_ref, sel_ref, *, nb):
    own = pl.program_id(0)
    blk = lax.broadcasted_iota(I32, (nb, TQ), 0)
    for h in range(H_MOBA):
        p, half = divmod(h, 2)
        qh = _head_rows(qT_ref[h * 64:(h + 1) * 64, :], half)
        g = _dot3(km_ref[:, p * 128:(p + 1) * 128], qh)
        g = jnp.where(blk < own, g, NEG_INF)
        sel, _ = _topk_axis0(g, MOBA_TOPK, blk)
        sel_ref[h] = jnp.where(blk == own, 1.0, sel)


def _moba_gate_prompt(qT, kmean):
    t = qT.shape[1]
    nb = kmean.shape[0]
    return pl.pallas_call(
        functools.partial(_moba_gate_kernel, nb=nb),
        grid=(t // TQ,),
        in_specs=[pl.BlockSpec((512, TQ), lambda i: (0, i)),
                  pl.BlockSpec((nb, 512), lambda i: (0, 0))],
        out_specs=pl.BlockSpec((H_MOBA, nb, TQ), lambda i: (0, 0, i)),
        out_shape=jax.ShapeDtypeStruct((H_MOBA, nb, t), F32),
        compiler_params=_cparams("arbitrary"),
        name="moba_gate",
    )(qT, kmean)


def _flash_kernel(qT_ref, k_ref, vT_ref, sel_ref, o_ref, qh_scr, sc_scr, *state, heads, sub, sel_div):
    nh = len(heads)
    m_scr, l_scr, acc_scr = state[:nh], state[nh:2 * nh], state[2 * nh:]
    i = pl.program_id(0)
    kio = lax.broadcasted_iota(I32, (TKF, TQ), 0)
    qio = lax.broadcasted_iota(I32, (TKF, TQ), 1)
    kiof = kio.astype(F32)
    rows = TKF // sub
    for h, (pair, half, set_idx, slope) in enumerate(heads):
        slope = slope * LOG2E
        qh_scr[h] = _head_rows(qT_ref[h * 64:(h + 1) * 64, :] * LOG2E, half).astype(BF16)
        sc_scr[h] = kiof * slope
        m_scr[h][...] = jnp.full((1, TQ), M_INIT, F32)
        l_scr[h][...] = jnp.zeros((1, TQ), F32)
        acc_scr[h][...] = jnp.zeros((64, TQ), F32)

    def sel_rows(set_idx, j):
        if sub == 1:
            return sel_ref[set_idx, pl.ds(j // sel_div, 1), :] > 0.0
        per8 = 8 // sub
        blk8 = sel_ref[set_idx, pl.ds(pl.multiple_of((j // per8) * 8, 8), 8), :]
        s = blk8[0:sub]
        for u in range(1, per8):
            s = jnp.where(j % per8 == u, blk8[u * sub:(u + 1) * sub], s)
        return s > 0.0

    def step(j, diag):
        off_base = (j * TKF - i * TQ).astype(F32)
        row0 = pl.multiple_of(j * TKF, TKF)
        sel_cache = {}
        for h, (pair, half, set_idx, slope) in enumerate(heads):
            if set_idx not in sel_cache:
                sel_cache[set_idx] = sel_rows(set_idx, j)
            selj = sel_cache[set_idx]
            kj = k_ref[pl.ds(row0, TKF), pair * 128:(pair + 1) * 128]
            t = jnp.dot(kj, qh_scr[h], preferred_element_type=F32) + sc_scr[h]
            if diag is not None:
                t = jnp.where(kio + diag * TKF <= qio, t, NEG_INF)
            off = (slope * LOG2E) * off_base
            t3 = t.reshape(sub, rows, TQ)
            mb = jnp.where(selj, jnp.max(t3, axis=1) + off, M_INIT)
            m = m_scr[h][...]
            m_new = jnp.maximum(m, jnp.max(mb, axis=0, keepdims=True))
            mu = jnp.where(selj, m_new - off, POS_INF)
            p = jnp.exp2(t3 - mu[:, None, :]).reshape(TKF, TQ)
            alpha = jnp.exp2(m - m_new)
            l_scr[h][...] = l_scr[h][...] * alpha + jnp.sum(p, axis=0, keepdims=True)
            pv = jnp.dot(vT_ref[pair, j], p.astype(BF16), preferred_element_type=F32)
            acc_scr[h][...] = acc_scr[h][...] * alpha + pv[half * 64:(half + 1) * 64]
            m_scr[h][...] = m_new

    per_q = TQ // TKF

    def body(jj, c):
        for u in range(per_q):
            step(jj * per_q + u, None)
        return c

    lax.fori_loop(0, i, body, 0)
    for u in range(per_q):
        step(i * per_q + u, u)
    for h in range(len(heads)):
        o_ref[h * 64:(h + 1) * 64, :] = acc_scr[h][...] / jnp.maximum(l_scr[h][...], 1e-30)


def _flash_select(qT, k, vTb, sel, heads, *, block):
    t = qT.shape[1]
    nh = len(heads)
    sub, sel_div = max(TKF // block, 1), max(block // TKF, 1)
    resident = lambda shape: pl.BlockSpec(shape, lambda i: (0,) * len(shape), pipeline_mode=pl.Buffered(1))
    return pl.pallas_call(
        functools.partial(_flash_kernel, heads=heads, sub=sub, sel_div=sel_div),
        grid=(t // TQ,),
        in_specs=[pl.BlockSpec((nh * 64, TQ), lambda i: (0, i)),
                  resident(k.shape), resident(vTb.shape),
                  pl.BlockSpec((sel.shape[0], sel.shape[1], TQ), lambda i: (0, 0, i))],
        out_specs=pl.BlockSpec((nh * 64, TQ), lambda i: (0, i)),
        out_shape=jax.ShapeDtypeStruct(qT.shape, F32),
        scratch_shapes=([pltpu.VMEM((nh, 128, TQ), BF16), pltpu.VMEM((nh, TKF, TQ), F32)]
                        + [pltpu.VMEM((1, TQ), F32)] * (2 * nh) + [pltpu.VMEM((64, TQ), F32)] * nh),
        compiler_params=_cparams("arbitrary"),
        name="flash_select",
    )(qT, k, vTb, sel)


HEADS_MOBA = tuple((h // 2, h % 2, h, SLOPES_A[h]) for h in range(H_MOBA))
HEADS_NSA = tuple((0, h // R_NSA, h // R_NSA, SLOPES_B[h]) for h in range(H_NSA))


def _win_kernel(qT_ref, k_ref, vT_ref, o_ref):
    i = pl.program_id(0)
    kio = lax.broadcasted_iota(I32, (TK, TQ), 0)
    qio = lax.broadcasted_iota(I32, (TK, TQ), 1)
    for h in range(H_NSA):
        half = h // R_NSA
        slope = SLOPES_B[h]
        qh = _head_rows(qT_ref[h * 64:(h + 1) * 64, :], half).astype(BF16)
        sc = kio.astype(F32) * slope
        m = jnp.full((1, TQ), M_INIT, F32)
        l = jnp.zeros((1, TQ), F32)
        acc = jnp.zeros((64, TQ), F32)
        for back in range(WINDOW // TK, -1, -1):
            j = i - back
            jc = jnp.maximum(j, 0)
            kj = k_ref[pl.ds(pl.multiple_of(jc * TK, TK), TK), :]
            dist = (qio - kio) + back * TK
            valid = (dist >= 0) & (dist <= WINDOW) & (j >= 0)
            t = jnp.dot(kj, qh, preferred_element_type=F32) + (sc - slope * (back * TK))
            t = jnp.where(valid, t, NEG_INF)
            m_new = jnp.maximum(m, jnp.max(t, axis=0, keepdims=True))
            p = jnp.exp(t - m_new)
            alpha = jnp.exp(m - m_new)
            l = l * alpha + jnp.sum(p, axis=0, keepdims=True)
            pv = jnp.dot(vT_ref[jc], p.astype(BF16), preferred_element_type=F32)
            acc = acc * alpha + pv[half * 64:(half + 1) * 64]
            m = m_new
        o_ref[h * 64:(h + 1) * 64, :] = acc / jnp.maximum(l, 1e-30)


def _win_prompt(qT, k, vTb):
    t = qT.shape[1]
    nkt = t // TK
    return pl.pallas_call(
        _win_kernel,
        grid=(t // TQ,),
        in_specs=[pl.BlockSpec((512, TQ), lambda i: (0, i)),
                  pl.BlockSpec((t, 128), lambda i: (0, 0)),
                  pl.BlockSpec((nkt, 128, TK), lambda i: (0, 0, 0))],
        out_specs=pl.BlockSpec((512, TQ), lambda i: (0, i)),
        out_shape=jax.ShapeDtypeStruct(qT.shape, F32),
        compiler_params=_cparams("arbitrary"),
        name="win_prompt",
    )(qT, k, vTb)


def _compress_kernel(*refs, paged, n_rows):
    if paged:
        pt_ref, x_hbm, pe_ref, w1_ref, b1_ref, wa_ref, wb_ref, w2_ref, b2_ref, o_ref, xs, sem, stage = refs
    else:
        x_hbm, pe_ref, w1_ref, b1_ref, wa_ref, wb_ref, w2_ref, b2_ref, o_ref, xs, sem = refs
    nc = n_rows // CMP_STRIDE
    if paged:
        b = pl.program_id(0)
        n_pages = n_rows // PAGE_SIZE
        ch = stage.shape[1]

        def page_copy(c, u):
            return pltpu.make_async_copy(x_hbm.at[pt_ref[b, c * ch + u], pl.ds(0, 256), :],
                                         stage.at[c % 2, u], sem.at[c % 2])

        def start_chunk(c):
            def body(u, carry):
                page_copy(c, u).start()
                return carry
            lax.fori_loop(0, ch, body, 0)

        def finish_chunk(c):
            def wait_body(u, carry):
                page_copy(c, u).wait()
                return carry
            lax.fori_loop(0, ch, wait_body, 0)

            def body(u, carry):
                row0 = pl.multiple_of((c * ch + u) * PAGE_SIZE, PAGE_SIZE)
                for c2 in range(2):
                    xs[c2, pl.ds(row0, PAGE_SIZE), :] = stage[c % 2, u, c2 * LANES:(c2 + 1) * LANES, :].T
                return carry
            lax.fori_loop(0, ch, body, 0)

        start_chunk(0)
        xs[:, pl.ds(n_rows, CMP_STRIDE), :] = jnp.zeros((2, CMP_STRIDE, LANES), F32)
        for c in range(n_pages // ch):
            if c + 1 < n_pages // ch:
                start_chunk(c + 1)
            finish_chunk(c)
    else:
        cps = [pltpu.make_async_copy(x_hbm.at[:, pl.ds(c2 * LANES, LANES)], xs.at[c2, pl.ds(0, n_rows), :], sem.at[0])
               for c2 in range(2)]
        for cp in cps:
            cp.start()
        xs[:, pl.ds(n_rows, CMP_STRIDE), :] = jnp.zeros((2, CMP_STRIDE, LANES), F32)
        for cp in cps:
            cp.wait()

    def rows(r):
        return jnp.concatenate([xs[c2, pl.ds(r, nc, stride=CMP_STRIDE), :] for c2 in range(2)], axis=1).astype(BF16)

    a = jnp.zeros((nc, 512), F32)
    for r in range(CMP_STRIDE):
        a = a + jnp.dot(rows(r), wa_ref[r], preferred_element_type=F32)
        a = a + jnp.dot(rows(CMP_STRIDE + r), wb_ref[r], preferred_element_type=F32)
    consts = []
    for j in range(2):
        cj = _dot3(pe_ref[j], w1_ref[j])[0:1] + b1_ref[j]
        consts += [cj, cj]
    hid = a + jnp.concatenate(consts, axis=1)
    act = jax.nn.gelu(hid)
    out = jnp.dot(act.astype(BF16), w2_ref[...], preferred_element_type=F32) + b2_ref[...]
    row = lax.broadcasted_iota(I32, out.shape, 0)
    o_ref[0] = jnp.where(row < nc - 1, out, 0.0)


def _compress(x, page_table, cw, *, n_rows, n_batch):
    paged = page_table is not None
    nc = n_rows // CMP_STRIDE
    const = lambda shape: pl.BlockSpec(shape, lambda *_: (0,) * len(shape), pipeline_mode=pl.Buffered(1))
    in_specs = [pl.BlockSpec(memory_space=pl.ANY), const((2, 8, 2048)), const((2, 2048, 128)), const((2, 1, 128)),
                const((16, 256, 512)), const((16, 256, 512)), const((512, 256)), const((1, 256))]
    scratch = [pltpu.VMEM((2, n_rows + CMP_STRIDE, LANES), F32), pltpu.SemaphoreType.DMA((2,))]
    if paged:
        chunk = min(32, n_rows // PAGE_SIZE)
        scratch.append(pltpu.VMEM((2, chunk, 256, PAGE_SIZE), F32))
    grid_spec = pltpu.PrefetchScalarGridSpec(
        num_scalar_prefetch=1 if paged else 0,
        grid=(n_batch,),
        in_specs=in_specs,
        out_specs=pl.BlockSpec((1, nc, 256), lambda b, *_: (b, 0, 0)),
        scratch_shapes=scratch)
    args = ((page_table,) if paged else ()) + (x,) + cw
    return pl.pallas_call(
        functools.partial(_compress_kernel, paged=paged, n_rows=n_rows),
        grid_spec=grid_spec,
        out_shape=jax.ShapeDtypeStruct((n_batch, nc, 256), F32),
        compiler_params=_cparams("arbitrary"),
        name="compress_paged" if paged else "compress",
    )(*args)


def _compress_weights(cmp_pe, cmp_w1, cmp_b1, cmp_w2, cmp_b2):
    half = CMP_STRIDE * D_HEAD
    wa = jnp.zeros((16, 256, 512), F32)
    wb = jnp.zeros((16, 256, 512), F32)
    w2 = jnp.zeros((512, 256), F32)
    for jg in range(4):
        j = jg // 2
        wa = wa.at[:, jg * 64:(jg + 1) * 64, jg * 128:(jg + 1) * 128].set(cmp_w1[j, :half].reshape(16, 64, 128))
        wb = wb.at[:, jg * 64:(jg + 1) * 64, jg * 128:(jg + 1) * 128].set(cmp_w1[j, half:].reshape(16, 64, 128))
        w2 = w2.at[jg * 128:(jg + 1) * 128, jg * 64:(jg + 1) * 64].set(cmp_w2[j])
    pe = jnp.zeros((2, 8, 2048), F32).at[:, 0].set(cmp_pe.reshape(2, 2048))
    b2 = jnp.concatenate([cmp_b2[0], cmp_b2[0], cmp_b2[1], cmp_b2[1]]).reshape(1, 256)
    return (pe, cmp_w1, cmp_b1.reshape(2, 1, 128), wa.astype(BF16), wb.astype(BF16), w2.astype(BF16), b2)


def _overlap_matrix(ns, nc_pad):
    ratio = SLC_BLOCK // CMP_STRIDE
    lr = CMP_LEN // CMP_STRIDE
    w = np.zeros((ns, nc_pad), np.float32)
    for s in range(ns):
        for m in range(ratio):
            for n in range(lr):
                c = ratio * s + m - n
                if 0 <= c < nc_pad - 1:
                    w[s, c] += 1.0
    return w


def _cmp_attn_kernel(qT_ref, pos_ref, kc_ref, vcT_ref, wT_ref, oc_ref, sel_ref, idx_ref, *, k_sel, tq):
    nc_pad = kc_ref.shape[1]
    ns = wT_ref.shape[0]
    pos = pos_ref[...]
    cidx = lax.broadcasted_iota(I32, (nc_pad, tq), 0)
    cpos = cidx * CMP_STRIDE + (CMP_LEN - 1)
    valid = (cpos <= pos) & (cidx < nc_pad - 1)
    ndist = (cpos - pos).astype(F32)
    blk = lax.broadcasted_iota(I32, (ns, tq), 0)
    d = jnp.right_shift(pos, 6) - blk
    forced = (blk < SLC_INIT) | ((d >= 0) & (d < SLC_LOCAL))
    kc = kc_ref[0]
    vcT = vcT_ref[0]
    for g in range(G_NSA):
        imp = jnp.zeros((nc_pad, tq), F32)
        for r in range(R_NSA):
            h = g * R_NSA + r
            qh = _head_rows(qT_ref[h * 64:(h + 1) * 64, :], g).astype(BF16)
            s = jnp.dot(kc, qh, preferred_element_type=F32) + ndist * SLOPES_B[h]
            s = jnp.where(valid, s, NEG_INF)
            m = jnp.max(s, axis=0, keepdims=True)
            m = jnp.where(m > NEG_INF, m, 0.0)
            p = jnp.exp(s - m)
            p = p * (1.0 / jnp.maximum(jnp.sum(p, axis=0, keepdims=True), 1e-30))
            imp = imp + p
            oc = jnp.dot(vcT, p.astype(BF16), preferred_element_type=F32)
            oc_ref[h * 64:(h + 1) * 64, :] = oc[g * 64:(g + 1) * 64]
        hi, lo = _split_bf16(imp)
        wT = wT_ref[...]
        islc = jnp.dot(wT, hi, preferred_element_type=F32) + jnp.dot(wT, lo, preferred_element_type=F32)
        score = jnp.where(forced, POS_INF, jnp.where(d >= 0, islc, NEG_INF))
        sel, picks = _topk_axis0(score, k_sel, blk)
        sel_ref[g] = sel
        for r, pk in enumerate(picks):
            idx_ref[g, r:r + 1, :] = pk
        for r in range(len(picks), SLC_TOPK):
            idx_ref[g, r:r + 1, :] = jnp.full((1, tq), -1, I32)


def _cmp_attn(qT, pos, kc, vcT, ns, *, k_sel, tq, tiles_per_batch):
    n = qT.shape[1]
    nc_pad = kc.shape[1]
    wT = jnp.asarray(_overlap_matrix(ns, nc_pad), BF16)
    bmap = (lambda i: (i // tiles_per_batch, 0, 0)) if tiles_per_batch else (lambda i: (0, 0, 0))
    return pl.pallas_call(
        functools.partial(_cmp_attn_kernel, k_sel=k_sel, tq=tq),
        grid=(n // tq,),
        in_specs=[pl.BlockSpec((512, tq), lambda i: (0, i)),
                  pl.BlockSpec((1, tq), lambda i: (0, i)),
                  pl.BlockSpec((1, nc_pad, 128), bmap),
                  pl.BlockSpec((1, 128, nc_pad), bmap),
                  pl.BlockSpec((ns, nc_pad), lambda i: (0, 0))],
        out_specs=[pl.BlockSpec((512, tq), lambda i: (0, i)),
                   pl.BlockSpec((G_NSA, ns, tq), lambda i: (0, 0, i)),
                   pl.BlockSpec((G_NSA, SLC_TOPK, tq), lambda i: (0, 0, i))],
        out_shape=[jax.ShapeDtypeStruct((512, n), F32),
                   jax.ShapeDtypeStruct((G_NSA, ns, n), F32),
                   jax.ShapeDtypeStruct((G_NSA, SLC_TOPK, n), I32)],
        compiler_params=_cparams("arbitrary"),
        name="cmp_attn",
    )(qT, pos, kc, vcT, wT)


def _post_kernel(x_ref, oa_ref, oc_ref, os_ref, ow_ref, gb_ref, g1_ref, sc2_ref, sh2_ref,
                 goa_ref, gob_ref, gpost_ref, gffn_ref, e3_ref, wout_ref, x1_ref, h2_ref):
    gates = jax.nn.sigmoid(gb_ref[...])
    ob = (_dot3(gates, e3_ref[0]) * oc_ref[...] + _dot3(gates, e3_ref[1]) * os_ref[...]
          + _dot3(gates, e3_ref[2]) * ow_ref[...])
    oa_n = _rms(oa_ref[...], goa_ref[...]).astype(BF16)
    ob_n = _rms(ob, gob_ref[...]).astype(BF16)
    o = (jnp.dot(oa_n, wout_ref[0:512, :], preferred_element_type=F32)
         + jnp.dot(ob_n, wout_ref[512:1024, :], preferred_element_type=F32))
    x1 = x_ref[...] + g1_ref[...] * _rms(o, gpost_ref[...])
    x1_ref[...] = x1
    h2_ref[...] = _rms(x1, gffn_ref[...]) * (1.0 + sc2_ref[...]) + sh2_ref[...]


def _gate_expand():
    e = np.zeros((3, 128, 512), np.float32)
    for h in range(H_NSA):
        for j in range(3):
            e[j, 3 * h + j, h * 64:(h + 1) * 64] = 1.0
    return jnp.asarray(e)


def _post(x, oa, oc, os_, ow, gb, gate1, scale2, shift2, g_oa, g_ob, g_post, g_ffn, w_out_bf):
    r, d = x.shape
    tm = min(256, r)
    per_row = gate1.shape[0] != 1
    mod = pl.BlockSpec((tm, d), lambda i: (i, 0)) if per_row else pl.BlockSpec((1, d), lambda i: (0, 0))
    row = lambda w: pl.BlockSpec((tm, w), lambda i: (i, 0))
    const = lambda shape: pl.BlockSpec(shape, lambda i: (0,) * len(shape))
    return pl.pallas_call(
        _post_kernel,
        grid=(r // tm,),
        in_specs=[row(d), row(512), row(512), row(512), row(512), row(128), mod, mod, mod,
                  const((1, 512)), const((1, 512)), const((1, d)), const((1, d)),
                  const((3, 128, 512)), const((1024, d))],
        out_specs=[row(d), row(d)],
        out_shape=[jax.ShapeDtypeStruct((r, d), F32), jax.ShapeDtypeStruct((r, d), F32)],
        compiler_params=_cparams("arbitrary"),
        name="post_attn",
    )(x, oa, oc, os_, ow, gb, gate1, scale2, shift2, g_oa.reshape(1, 512), g_ob.reshape(1, 512),
      g_post.reshape(1, d), g_ffn.reshape(1, d), _gate_expand(), w_out_bf)


def _router_kernel(h_ref, wr_ref, br_ref, tri_ref, idx_ref, wt_ref, rank_ref, cnt_ref, run_ref, *, n_valid, tm):
    i = pl.program_id(0)

    @pl.when(i == 0)
    def _():
        run_ref[...] = jnp.zeros_like(run_ref)

    s = jax.nn.sigmoid(_dot3(h_ref[...], wr_ref[...]))
    sel = s + br_ref[...]
    lane = lax.broadcasted_iota(I32, (tm, N_EXPERTS), 1)
    per = N_EXPERTS // N_EXPERT_GROUPS
    grp = jnp.right_shift(lane, 5)
    lane_g = lax.broadcasted_iota(I32, (tm, LANES), 1)
    gscore = jnp.full((tm, LANES), NEG_INF, F32)
    for g in range(N_EXPERT_GROUPS):
        v = jnp.where(grp == g, sel, NEG_INF)
        (i1, m1), (_, m2) = _topk_axis1(v, 2, lane)
        gscore = jnp.where(lane_g == g, m1 + m2, gscore)
    emask = jnp.zeros((tm, N_EXPERTS), jnp.bool_)
    for first, _ in _topk_axis1(gscore, TOPK_GROUPS, lane_g):
        emask = emask | (grp == first)
    picks = _topk_axis1(jnp.where(emask, sel, NEG_INF), TOPK_EXPERTS, lane)
    row = lax.broadcasted_iota(I32, (tm, 1), 0) + i * tm
    row_ok = row < n_valid
    onehot = jnp.zeros((tm, N_EXPERTS), F32)
    ws = []
    for first, _ in picks:
        pick = lane == first
        ws.append(jnp.sum(jnp.where(pick, s, 0.0), axis=1, keepdims=True))
        onehot = jnp.where(pick & row_ok, 1.0, onehot)
    wsum = ws[0]
    for w in ws[1:]:
        wsum = wsum + w
    excl = jnp.dot(tri_ref[...], onehot.astype(BF16), preferred_element_type=F32) + run_ref[...]
    idx_out = jnp.zeros((tm, LANES), I32)
    wt_out = jnp.zeros((tm, LANES), F32)
    rank_out = jnp.zeros((tm, LANES), I32)
    for k, (first, _) in enumerate(picks):
        rk = jnp.sum(jnp.where(lane == first, excl, 0.0), axis=1, keepdims=True)
        idx_out = jnp.where(lane_g == k, first, idx_out)
        wt_out = jnp.where(lane_g == k, ws[k] / wsum * ROUTED_SCALE, wt_out)
        rank_out = jnp.where(lane_g == k, rk.astype(I32), rank_out)
    idx_ref[...] = idx_out
    wt_ref[...] = wt_out
    rank_ref[...] = rank_out
    run_ref[...] = run_ref[...] + jnp.sum(onehot, axis=0, keepdims=True)
    cnt_ref[...] = run_ref[...]


def _router(h_all, w_router, b_router, n_valid):
    n, d = h_all.shape
    tm = 128
    tri = jnp.asarray(np.tril(np.ones((tm, tm), np.float32), -1), BF16)
    const = lambda shape: pl.BlockSpec(shape, lambda i: (0,) * len(shape))
    row = lambda w: pl.BlockSpec((tm, w), lambda i: (i, 0))
    return pl.pallas_call(
        functools.partial(_router_kernel, n_valid=n_valid, tm=tm),
        grid=(n // tm,),
        in_specs=[row(d), const((d, N_EXPERTS)), const((1, N_EXPERTS)), const((tm, tm))],
        out_specs=[row(LANES), row(LANES), row(LANES), const((1, N_EXPERTS))],
        out_shape=[jax.ShapeDtypeStruct((n, LANES), I32), jax.ShapeDtypeStruct((n, LANES), F32),
                   jax.ShapeDtypeStruct((n, LANES), I32), jax.ShapeDtypeStruct((1, N_EXPERTS), F32)],
        scratch_shapes=[pltpu.VMEM((1, N_EXPERTS), F32)],
        compiler_params=_cparams("arbitrary"),
        name="router",
    )(h_all, w_router, b_router.reshape(1, N_EXPERTS), tri)


def _gather_rows(idx_hbm, idx_smem, isem, src_hbm, buf, sem, step, n_steps, n_rows):
    def idx_copy(k, slot):
        return pltpu.make_async_copy(idx_hbm.at[k], idx_smem.at[slot], isem.at[slot])

    def issue(slot):
        def body(r, c):
            pltpu.make_async_copy(src_hbm.at[idx_smem[slot, r]], buf.at[slot, r], sem.at[slot]).start()
            return c
        lax.fori_loop(0, n_rows, body, 0, unroll=8)

    @pl.when(step == 0)
    def _():
        idx_copy(0, 0).start()
        idx_copy(0, 0).wait()
        issue(0)

        @pl.when(1 < n_steps)
        def _():
            idx_copy(1, 1).start()

    nxt = (step + 1) % 2

    @pl.when(step + 1 < n_steps)
    def _():
        idx_copy(step + 1, nxt).wait()
        issue(nxt)

    @pl.when(step + 2 < n_steps)
    def _():
        idx_copy(step + 2, step % 2).start()

    slot = step % 2

    @pl.when(step < n_steps)
    def _():
        pltpu.make_async_copy(src_hbm.at[pl.ds(0, n_rows)], buf.at[slot], sem.at[slot]).wait()
    return slot


def _expert_kernel(blk_e_ref, nused_ref, tok_hbm, h_hbm, wup_ref, wdn_ref, o_ref, buf0, buf1, idx_smem, sem, isem,
                   wup_bf, wdn_bf):
    i = pl.program_id(0)
    nused = nused_ref[0]
    rb = ROW_BLOCK
    bufs = (buf0, buf1)

    def idx_copy(k, slot):
        return pltpu.make_async_copy(tok_hbm.at[k], idx_smem.at[slot], isem.at[slot])

    def issue_rows(slot):
        for r in range(rb):
            pltpu.make_async_copy(h_hbm.at[idx_smem[slot, r]], bufs[slot].at[r], sem.at[slot]).start()

    def wait_rows(slot):
        pltpu.make_async_copy(h_hbm.at[pl.ds(0, rb)], bufs[slot], sem.at[slot]).wait()

    def compute(slot):
        buf = bufs[slot]
        x = jnp.concatenate([buf[:, s, :] for s in range(8)], axis=1).astype(BF16)
        up = jnp.dot(x, wup_bf[...], preferred_element_type=F32)
        half = up.shape[1] // 2
        act = (jax.nn.silu(up[:, :half]) * up[:, half:]).astype(BF16)
        y = jnp.dot(act, wdn_bf[...], preferred_element_type=F32)
        for s in range(8):
            o_ref[:, s, :] = y[:, s * 128:(s + 1) * 128]

    @pl.when(i == 0)
    def _():
        idx_copy(0, 0).start()
        idx_copy(0, 0).wait()
        issue_rows(0)

        @pl.when(1 < nused)
        def _():
            idx_copy(1, 1).start()

    @pl.when(i + 2 < nused)
    def _():
        idx_copy(i + 2, i % 2).start()

    new_expert = (i == 0) | (blk_e_ref[i] != blk_e_ref[jnp.maximum(i - 1, 0)])

    @pl.when(new_expert & (i < nused))
    def _():
        wup_bf[...] = wup_ref[0].astype(BF16)
        wdn_bf[...] = wdn_ref[0].astype(BF16)

    has_next = i + 1 < nused
    for slot in range(2):
        mine = (i % 2) == slot

        @pl.when(mine & has_next)
        def _():
            idx_copy(i + 1, 1 - slot).wait()
            issue_rows(1 - slot)
            wait_rows(slot)
            compute(slot)

        @pl.when(mine & (i < nused) & jnp.logical_not(has_next))
        def _():
            wait_rows(slot)
            compute(slot)

    @pl.when(i >= nused)
    def _():
        o_ref[...] = jnp.zeros_like(o_ref)


def _experts(h3, blk_e, row_tok, nused, w_exp_up, w_exp_down):
    nblk = blk_e.shape[0]
    rb = ROW_BLOCK
    e, d, f2 = w_exp_up.shape
    grid_spec = pltpu.PrefetchScalarGridSpec(
        num_scalar_prefetch=2,
        grid=(nblk,),
        in_specs=[pl.BlockSpec(memory_space=pl.ANY), pl.BlockSpec(memory_space=pl.ANY),
                  pl.BlockSpec((1, d, f2), lambda i, be, nu: (be[i], 0, 0)),
                  pl.BlockSpec((1, f2 // 2, d), lambda i, be, nu: (be[i], 0, 0))],
        out_specs=pl.BlockSpec((rb, 8, 128), lambda i, be, nu: (i, 0, 0)),
        scratch_shapes=[pltpu.VMEM((rb, 8, 128), F32), pltpu.VMEM((rb, 8, 128), F32), pltpu.SMEM((2, rb), I32),
                        pltpu.SemaphoreType.DMA((2,)), pltpu.SemaphoreType.DMA((2,)),
                        pltpu.VMEM((d, f2), BF16), pltpu.VMEM((f2 // 2, d), BF16)])
    return pl.pallas_call(
        _expert_kernel,
        grid_spec=grid_spec,
        out_shape=jax.ShapeDtypeStruct((nblk * rb, 8, 128), F32),
        compiler_params=_cparams("arbitrary"),
        name="experts",
    )(blk_e, nused, row_tok.reshape(nblk, rb), h3, w_exp_up, w_exp_down)


def _combine_kernel(dest_hbm, y_hbm, wt_ref, h_ref, x1_ref, g2_ref, gpost_ref, wsu_ref, wsd_ref, o_ref,
                    buf, idx_smem, sem, isem, *, tm):
    i = pl.program_id(0)
    slot = _gather_rows(dest_hbm, idx_smem, isem, y_hbm, buf, sem, i, pl.num_programs(0), tm * TOPK_EXPERTS)
    wt = wt_ref[...]
    f = jnp.zeros((tm, h_ref.shape[1]), F32)
    for k in range(TOPK_EXPERTS):
        yk = jnp.concatenate([buf[slot, pl.ds(k, tm, stride=8), s, :] for s in range(8)], axis=1)
        f = f + yk * wt[:, k:k + 1]
    hb = h_ref[...].astype(BF16)
    up = jnp.dot(hb, wsu_ref[...], preferred_element_type=F32)
    half = up.shape[1] // 2
    act = (jax.nn.silu(up[:, :half]) * up[:, half:]).astype(BF16)
    f = f + jnp.dot(act, wsd_ref[...], preferred_element_type=F32)
    o_ref[...] = x1_ref[...] + g2_ref[...] * _rms(f, gpost_ref[...])


def _combine(dest, y, wt, h_all, x1_all, gate2_tab, n_prompt, g_ffn_post, w_sh_up_bf, w_sh_down_bf):
    n, d = h_all.shape
    tm = 128
    assert n_prompt % tm == 0
    const = lambda shape: pl.BlockSpec(shape, lambda i: (0,) * len(shape))
    row = lambda w: pl.BlockSpec((tm, w), lambda i: (i, 0))
    gate_spec = pl.BlockSpec((tm, d), lambda i: (jnp.maximum(i - (n_prompt // tm - 1), 0), 0))
    na = tm * TOPK_EXPERTS
    return pl.pallas_call(
        functools.partial(_combine_kernel, tm=tm),
        grid=(n // tm,),
        in_specs=[pl.BlockSpec(memory_space=pl.ANY), pl.BlockSpec(memory_space=pl.ANY), row(LANES), row(d), row(d),
                  gate_spec, const((1, d)), const(w_sh_up_bf.shape), const(w_sh_down_bf.shape)],
        out_specs=row(d),
        scratch_shapes=[pltpu.VMEM((2, na, 8, 128), F32), pltpu.SMEM((2, na), I32),
                        pltpu.SemaphoreType.DMA((2,)), pltpu.SemaphoreType.DMA((2,))],
        out_shape=jax.ShapeDtypeStruct((n, d), F32),
        compiler_params=_cparams("arbitrary"),
        name="moe_combine",
    )(dest.reshape(n // tm, na), y, wt, h_all, x1_all, gate2_tab, g_ffn_post.reshape(1, d),
      w_sh_up_bf, w_sh_down_bf)


def _dest_kernel(idx_ref, rank_ref, ps_ref, o_ref):
    tm = idx_ref.shape[0]
    idxf = idx_ref[...].astype(F32)
    ps = ps_ref[...].astype(F32)
    lane_g = lax.broadcasted_iota(I32, (tm, LANES), 1)
    lane = lax.broadcasted_iota(I32, (tm, N_EXPERTS), 1).astype(F32)
    base = jnp.zeros((tm, LANES), F32)
    for k in range(TOPK_EXPERTS):
        idx_k = jnp.sum(jnp.where(lane_g == k, idxf, 0.0), axis=1, keepdims=True)
        ps_k = jnp.sum(jnp.where(lane == idx_k, ps, 0.0), axis=1, keepdims=True)
        base = jnp.where(lane_g == k, ps_k, base)
    o_ref[...] = base.astype(I32) + rank_ref[...]


def _moe_dest(idx_p, rank_p, p_start):
    n = idx_p.shape[0]
    tm = 128
    row = pl.BlockSpec((tm, LANES), lambda i: (i, 0))
    return pl.pallas_call(
        _dest_kernel,
        grid=(n // tm,),
        in_specs=[row, row, pl.BlockSpec((1, N_EXPERTS), lambda i: (0, 0))],
        out_specs=row,
        out_shape=jax.ShapeDtypeStruct((n, LANES), I32),
        compiler_params=_cparams("arbitrary"),
        name="moe_dest",
    )(idx_p, rank_p, p_start.reshape(1, N_EXPERTS))


def _moe(h_all, x1_all, gate2_tab, n_prompt, n_valid, w_router, b_router, w_exp_up, w_exp_down, w_sh_up, w_sh_down,
         g_ffn_post):
    n, d = h_all.shape
    rb = ROW_BLOCK
    e = N_EXPERTS
    idx_p, wt_p, rank_p, cnt = _router(h_all, w_router, b_router, n_valid)
    counts = cnt[0].astype(I32)
    padded = (counts + rb - 1) // rb * rb
    p_end = jnp.cumsum(padded)
    p_start = p_end - padded
    dest = _moe_dest(idx_p, rank_p, p_start)[:, :TOPK_EXPERTS]
    nblk = -(-(n_valid * TOPK_EXPERTS + e * (rb - 1)) // rb)
    tok = jnp.repeat(jnp.arange(n_valid, dtype=I32), TOPK_EXPERTS)
    row_tok = jnp.zeros((nblk * rb,), I32).at[dest[:n_valid].reshape(-1)].set(tok)
    blk_start = jnp.arange(nblk, dtype=I32) * rb
    blk_e = jnp.minimum(jnp.sum((p_end[None, :] <= blk_start[:, None]).astype(I32), axis=1), e - 1)
    nused = (p_end[-1] // rb).astype(I32).reshape(1)
    y = _experts(h_all.reshape(n, 8, 128), blk_e, row_tok, nused, w_exp_up, w_exp_down)
    dest = jnp.where(jnp.arange(n, dtype=I32)[:, None] < n_valid, dest, 0)
    return _combine(dest, y, wt_p, h_all, x1_all, gate2_tab, n_prompt, g_ffn_post,
                    w_sh_up.astype(BF16), w_sh_down.astype(BF16))


def _kmean_pages_kernel(pt_ref, *refs):
    pages, o_ref = refs[:-1], refs[-1]
    s_idx = pl.program_id(1)
    per_step = len(pages) // 2

    @pl.when(s_idx == 0)
    def _():
        o_ref[...] = jnp.zeros_like(o_ref)

    ones = jnp.ones((PAGE_SIZE, LANES), BF16)
    lane = lax.broadcasted_iota(I32, (512, LANES), 1)
    acc = o_ref[0]
    for u in range(per_step):
        tot = jnp.zeros((512, LANES), F32)
        for pg in (pages[2 * u], pages[2 * u + 1]):
            hi, lo = _split_bf16(pg[0])
            tot = tot + (jnp.dot(hi, ones, preferred_element_type=F32) + jnp.dot(lo, ones, preferred_element_type=F32))
        acc = jnp.where(lane == s_idx * per_step + u, tot * (1.0 / MOBA_BLOCK), acc)
    o_ref[0] = acc


def _kmean_pages(cache_a, page_table):
    bs, n_pages = page_table.shape
    pps = PAGES_PER_STEP
    assert n_pages // 2 <= LANES
    in_specs = [pl.BlockSpec((1, 512, PAGE_SIZE), lambda b, s, pt, u=u: (pt[b, s * pps + u], 0, 0)) for u in range(pps)]
    grid_spec = pltpu.PrefetchScalarGridSpec(
        num_scalar_prefetch=1, grid=(bs, n_pages // pps), in_specs=in_specs,
        out_specs=pl.BlockSpec((1, 512, LANES), lambda b, s, pt: (b, 0, 0)))
    return pl.pallas_call(
        _kmean_pages_kernel, grid_spec=grid_spec,
        out_shape=jax.ShapeDtypeStruct((bs, 512, LANES), F32),
        compiler_params=_cparams("arbitrary", "arbitrary"),
        name="kmean_pages",
    )(page_table, *([cache_a] * pps))


def _moba_sample_kernel(pt_ref, qbd_ref, km_ref, new_ref, *refs, nb, ts, past):
    pps = PAGES_PER_STEP
    pages = refs[:pps]
    o_ref, sel_scr, m_scr, l_scr, acc_scr = refs[pps:]
    s_idx = pl.program_id(1)
    rows = H_MOBA * ts
    qbd = qbd_ref[0]
    qb = qbd.astype(BF16)
    rowi = lax.broadcasted_iota(I32, (rows, 1), 0)
    slope = jnp.zeros((rows, 1), F32)
    for h in range(H_MOBA):
        slope = jnp.where(_div_pow2(rowi, ts) == h, SLOPES_A[h], slope)
    lane = lax.broadcasted_iota(I32, (rows, LANES), 1)

    @pl.when(s_idx == 0)
    def _():
        gate = _dot3(qbd, km_ref[0])
        gate = jnp.where(lane < nb, gate, NEG_INF)
        sel = jnp.zeros((rows, LANES), F32)
        for first, m in _topk_axis1(gate, MOBA_TOPK, lane):
            sel = jnp.where((lane == first) & (m > NEG_INF), 1.0, sel)
        sel_scr[...] = sel
        m_scr[...] = jnp.full_like(m_scr, M_INIT)
        l_scr[...] = jnp.zeros_like(l_scr)
        acc_scr[...] = jnp.zeros_like(acc_scr)

    def update(s, vT_bf):
        m = m_scr[...]
        m_new = jnp.maximum(m, jnp.max(s, axis=1, keepdims=True))
        p = jnp.exp(s - m_new)
        alpha = jnp.exp(m - m_new)
        l_scr[...] = l_scr[...] * alpha + jnp.sum(p, axis=1, keepdims=True)
        acc_scr[...] = acc_scr[...] * alpha + lax.dot_general(p.astype(BF16), vT_bf, _NT, preferred_element_type=F32)
        m_scr[...] = m_new

    sel = sel_scr[...]
    for u in range(pps):
        page = s_idx * pps + u
        flag = jnp.sum(jnp.where(lane == page // 2, sel, 0.0), axis=1, keepdims=True) > 0.0
        kT = pages[u][0, 0:512, :].astype(BF16)
        vT = pages[u][0, 512:1024, :].astype(BF16)
        s = jnp.dot(qb, kT, preferred_element_type=F32)
        kpos = page * PAGE_SIZE + lane - past
        s = jnp.where(flag, s + slope * kpos.astype(F32), NEG_INF)
        update(s, vT)

    @pl.when(s_idx == pl.num_programs(1) - 1)
    def _():
        kn = new_ref[0, 0:512, :].astype(BF16)
        vn = new_ref[0, 512:1024, :].astype(BF16)
        s = jnp.dot(qb, kn, preferred_element_type=F32)
        ok = (lane < ts) & (lane <= _mod_pow2(rowi, ts))
        s = jnp.where(ok, s + slope * lane.astype(F32), NEG_INF)
        update(s, vn)
        o = acc_scr[...] / jnp.maximum(l_scr[...], 1e-30)
        col_h = _div_pow2(lax.broadcasted_iota(I32, (rows, 512), 1), D_HEAD)
        o = jnp.where(col_h == _div_pow2(rowi, ts), o, 0.0)
        pick_r = lax.broadcasted_iota(I32, (8, rows), 0)
        pick_c = lax.broadcasted_iota(I32, (8, rows), 1)
        gather = jnp.where(_mod_pow2(pick_c, ts) == pick_r, 1.0, 0.0)
        o_ref[0] = _dot3(gather, o)


def _moba_sample(qbd, km_pad, new_pad, cache_a, page_table, *, nb, ts, past):
    bs, n_pages = page_table.shape
    pps = PAGES_PER_STEP
    rows = H_MOBA * ts
    in_specs = [pl.BlockSpec((1, rows, 512), lambda b, s, pt: (b, 0, 0)),
                pl.BlockSpec((1, 512, LANES), lambda b, s, pt: (b, 0, 0)),
                pl.BlockSpec((1, 1024, LANES), lambda b, s, pt: (b, 0, 0))]
    in_specs += [pl.BlockSpec((1, 1024, PAGE_SIZE), lambda b, s, pt, u=u: (pt[b, s * pps + u], 0, 0)) for u in range(pps)]
    grid_spec = pltpu.PrefetchScalarGridSpec(
        num_scalar_prefetch=1, grid=(bs, n_pages // pps), in_specs=in_specs,
        out_specs=pl.BlockSpec((1, 8, 512), lambda b, s, pt: (b, 0, 0)),
        scratch_shapes=[pltpu.VMEM((rows, LANES), F32), pltpu.VMEM((rows, 1), F32), pltpu.VMEM((rows, 1), F32),
                        pltpu.VMEM((rows, 512), F32)])
    return pl.pallas_call(
        functools.partial(_moba_sample_kernel, nb=nb, ts=ts, past=past),
        grid_spec=grid_spec,
        out_shape=jax.ShapeDtypeStruct((bs, 8, 512), F32),
        compiler_params=_cparams("arbitrary", "arbitrary"),
        name="moba_sample",
    )(page_table, qbd, km_pad, new_pad, *([cache_a] * pps))


def _nsa_sample_kernel(pt_ref, idx_ref, qs_ref, qw_ref, kpos_ref, new_ref, win_ref, wnew_ref, cache_hbm,
                       os_ref, ow_ref, buf, sem, *, ts, past, n_slots):
    b = pl.program_id(0)
    ngq = G_NSA * ts
    copies = []
    for gq in range(ngq):
        for t in range(n_slots):
            blk = jnp.maximum(idx_ref[(b * ngq + gq) * SLC_TOPK + t], 0)
            page = pt_ref[b, blk // 2]
            cp = pltpu.make_async_copy(cache_hbm.at[page, pl.ds(256, 256), :],
                                       buf.at[gq, :, pl.ds(t * PAGE_SIZE, PAGE_SIZE)], sem.at[0])
            cp.start()
            copies.append(cp)
    new = new_ref[0]
    for gq in range(ngq):
        buf[gq, :, pl.ds(n_slots * PAGE_SIZE, PAGE_SIZE)] = new
    for cp in copies:
        cp.wait()

    rowi = lax.broadcasted_iota(I32, (8, 1), 0)
    for gq in range(ngq):
        g, q = divmod(gq, ts)
        slope = jnp.zeros((8, 1), F32)
        for r in range(R_NSA):
            slope = jnp.where(rowi == r, SLOPES_B[g * R_NSA + r], slope)
        kT = buf[gq, 0:128, :].astype(BF16)
        vT = buf[gq, 128:256, :].astype(BF16)
        s = jnp.dot(qs_ref[0, gq].astype(BF16), kT, preferred_element_type=F32)
        kpos = kpos_ref[0, gq]
        ok = (kpos >= 0) & (kpos <= past + q)
        s = jnp.where(ok, s + slope * (kpos - past).astype(F32), NEG_INF)
        m = jnp.max(s, axis=1, keepdims=True)
        m = jnp.where(m > NEG_INF, m, 0.0)
        p = jnp.exp(s - m)
        p = p * (1.0 / jnp.maximum(jnp.sum(p, axis=1, keepdims=True), 1e-30))
        os_ref[0, gq] = lax.dot_general(p.astype(BF16), vT, _NT, preferred_element_type=F32)

    nw = WINDOW + LANES
    kw_all = jnp.concatenate([win_ref[0], wnew_ref[0]], axis=1)
    kw = kw_all[0:128, :].astype(BF16)
    vw = kw_all[128:256, :].astype(BF16)
    wrow = lax.broadcasted_iota(I32, (R_NSA * ts, 1), 0)
    wlane = lax.broadcasted_iota(I32, (R_NSA * ts, nw), 1)
    dist = (WINDOW + _mod_pow2(wrow, ts)) - wlane
    okw = (dist >= 0) & (dist <= WINDOW) & (wlane < WINDOW + ts)
    for g in range(G_NSA):
        slope = jnp.zeros((R_NSA * ts, 1), F32)
        for r in range(R_NSA):
            slope = jnp.where(_div_pow2(wrow, ts) == r, SLOPES_B[g * R_NSA + r], slope)
        s = jnp.dot(qw_ref[0, g].astype(BF16), kw, preferred_element_type=F32)
        s = jnp.where(okw, s - slope * dist.astype(F32), NEG_INF)
        m = jnp.max(s, axis=1, keepdims=True)
        m = jnp.where(m > NEG_INF, m, 0.0)
        p = jnp.exp(s - m)
        p = p * (1.0 / jnp.maximum(jnp.sum(p, axis=1, keepdims=True), 1e-30))
        ow_ref[0, g] = lax.dot_general(p.astype(BF16), vw, _NT, preferred_element_type=F32)


def _nsa_sample(idx_flat, qs, qw, kpos, new_pad, win, wnew_pad, cache_b, page_table, *, ts, past, n_slots):
    bs = page_table.shape[0]
    ngq = G_NSA * ts
    nk = (n_slots + 1) * PAGE_SIZE
    m4 = lambda shape: pl.BlockSpec(shape, lambda b, pt, ix: (b,) + (0,) * (len(shape) - 1))
    grid_spec = pltpu.PrefetchScalarGridSpec(
        num_scalar_prefetch=2, grid=(bs,),
        in_specs=[m4((1, ngq, 8, 128)), m4((1, G_NSA, R_NSA * ts, 128)), m4((1, ngq, 1, nk)),
                  m4((1, 256, LANES)), m4((1, 256, WINDOW)), m4((1, 256, LANES)),
                  pl.BlockSpec(memory_space=pl.ANY)],
        out_specs=[m4((1, ngq, 8, 128)), m4((1, G_NSA, R_NSA * ts, 128))],
        scratch_shapes=[pltpu.VMEM((ngq, 256, nk), F32), pltpu.SemaphoreType.DMA((1,))])
    return pl.pallas_call(
        functools.partial(_nsa_sample_kernel, ts=ts, past=past, n_slots=n_slots),
        grid_spec=grid_spec,
        out_shape=[jax.ShapeDtypeStruct((bs, ngq, 8, 128), F32),
                   jax.ShapeDtypeStruct((bs, G_NSA, R_NSA * ts, 128), F32)],
        compiler_params=_cparams("arbitrary"),
        name="nsa_sample",
    )(page_table, idx_flat, qs, qw, kpos, new_pad, win, wnew_pad, cache_b)


def _vT_blocks(v, tk):
    t, c = v.shape
    return v.astype(BF16).reshape(t // tk, tk, c // 128, 128).transpose(2, 0, 3, 1)


def _prompt_mixer(proj, cw):
    qa, kva, qb, kvb, kvw, _ = proj
    t = qa.shape[0]
    qaT = (qa * Q_SCALE).T
    qbT = (qb * Q_SCALE).T
    kmean = _kmean_prompt(kva)
    sel_a = _moba_gate_prompt(qaT, kmean)
    oaT = _flash_select(qaT, kva[:, :512].astype(BF16), _vT_blocks(kva[:, 512:], TKF), sel_a, HEADS_MOBA,
                        block=MOBA_BLOCK)
    cmp_out = _compress(kvb, None, cw, n_rows=t, n_batch=1)
    kc = cmp_out[:, :, 0:128].astype(BF16)
    vcT = cmp_out[:, :, 128:256].astype(BF16).transpose(0, 2, 1)
    pos = jnp.arange(t, dtype=I32).reshape(1, t)
    ocT, sel_b, _ = _cmp_attn(qbT, pos, kc, vcT, t // SLC_BLOCK, k_sel=SLC_TOPK, tq=TQ, tiles_per_batch=0)
    osT = _flash_select(qbT, kvb[:, 256:384].astype(BF16), _vT_blocks(kvb[:, 384:512], TKF), sel_b, HEADS_NSA,
                        block=SLC_BLOCK)
    owT = _win_prompt(qbT, kvw[:, 0:128].astype(BF16), _vT_blocks(kvw[:, 128:256], TK)[0])
    return oaT.T, ocT.T, osT.T, owT.T


def _sample_mixer(proj, cache_a, cache_b, win_state, page_table, cw, bs, ts):
    qa, kva, qb, kvb, kvw, _ = proj
    n_pages = page_table.shape[1]
    past = n_pages * PAGE_SIZE
    nb = past // MOBA_BLOCK
    new_rows_T = lambda a: jnp.zeros((bs, LANES, a.shape[-1]), F32).at[:, :ts].set(a).transpose(0, 2, 1)
    km = _kmean_pages(cache_a, page_table)
    q4 = (qa * Q_SCALE).reshape(bs, ts, H_MOBA, D_HEAD).transpose(0, 2, 1, 3)
    eye = jnp.eye(H_MOBA, dtype=F32)
    qbd = (q4[:, :, :, None, :] * eye[None, :, None, :, None]).reshape(bs, H_MOBA * ts, 512)
    new_a = new_rows_T(kva.reshape(bs, ts, 1024))
    oa = _moba_sample(qbd, km, new_a, cache_a, page_table, nb=nb, ts=ts, past=past)[:, :ts]
    oa = oa.reshape(bs * ts, 512)
    cmp_out = _compress(cache_b, page_table, cw, n_rows=past, n_batch=bs)
    kc = cmp_out[:, :, 0:128].astype(BF16)
    vcT = cmp_out[:, :, 128:256].astype(BF16).transpose(0, 2, 1)
    qbs = (qb * Q_SCALE).reshape(bs, ts, 512)
    qT = jnp.zeros((bs, LANES, 512), F32).at[:, :ts].set(qbs).reshape(bs * LANES, 512).T
    pos = jnp.broadcast_to(past + jnp.minimum(jnp.arange(LANES, dtype=I32), ts - 1), (bs, LANES)).reshape(1, bs * LANES)
    n_slots = SLC_TOPK - 1
    ocT, _, idx = _cmp_attn(qT, pos, kc, vcT, past // SLC_BLOCK, k_sel=n_slots, tq=LANES, tiles_per_batch=1)
    oc = ocT.T.reshape(bs, LANES, 512)[:, :ts].reshape(bs * ts, 512)
    idx = idx.reshape(G_NSA, SLC_TOPK, bs, LANES)[:, :, :, :ts].transpose(2, 0, 3, 1)
    q5 = qbs.reshape(bs, ts, G_NSA, R_NSA, D_HEAD)
    lane_g = jnp.eye(G_NSA, dtype=F32)
    qsel = q5.transpose(0, 2, 1, 3, 4)[:, :, :, :, None, :] * lane_g[None, :, None, None, :, None]
    qs = jnp.zeros((bs, G_NSA, ts, 8, 128), F32).at[:, :, :, :R_NSA].set(qsel.reshape(bs, G_NSA, ts, R_NSA, 128))
    qs = qs.reshape(bs, G_NSA * ts, 8, 128)
    qwin = q5.transpose(0, 2, 3, 1, 4)[:, :, :, :, None, :] * lane_g[None, :, None, None, :, None]
    qw = qwin.reshape(bs, G_NSA, R_NSA * ts, 128)
    sl = idx[..., :n_slots, None]
    lane = jnp.arange(PAGE_SIZE, dtype=I32)
    in_blk = (sl >= 0) & ((lane // SLC_BLOCK) == (sl % 2))
    slot_pos = jnp.where(in_blk, (sl // 2) * PAGE_SIZE + lane, -1).reshape(bs, G_NSA, ts, n_slots * PAGE_SIZE)
    own = jnp.where(lane < ts, past + lane, -1)
    kpos = jnp.concatenate([slot_pos, jnp.broadcast_to(own, (bs, G_NSA, ts, PAGE_SIZE))], axis=-1)
    kpos = kpos.reshape(bs, G_NSA * ts, 1, (n_slots + 1) * PAGE_SIZE).astype(I32)
    new_b = new_rows_T(kvb.reshape(bs, ts, 512)[:, :, 256:])
    wnew = new_rows_T(kvw.reshape(bs, ts, 256))
    os_raw, ow_raw = _nsa_sample(idx.reshape(-1).astype(I32), qs, qw, kpos, new_b, win_state, wnew, cache_b,
                                 page_table, ts=ts, past=past, n_slots=n_slots)
    os5 = os_raw.reshape(bs, G_NSA, ts, 8, G_NSA, D_HEAD)[:, :, :, :R_NSA]
    os_ = jnp.stack([os5[:, g, :, :, g] for g in range(G_NSA)], axis=2).reshape(bs * ts, 512)
    ow5 = ow_raw.reshape(bs, G_NSA, R_NSA, ts, G_NSA, D_HEAD)
    ow = jnp.stack([ow5[:, g, :, :, g] for g in range(G_NSA)], axis=1)
    ow = ow.transpose(0, 3, 1, 2, 4).reshape(bs * ts, 512)
    return oa, oc, os_, ow


def _layer(x_p, x_s, cache_a, cache_b, win_state, page_table, c_p, c_s, w):
    (w_ada, b_ada, g_mix_pre, g_mix_post, g_ffn_pre, g_ffn_post, w_in, g_out_moba, g_out_nsa, w_out,
     cmp_pe, cmp_w1, cmp_b1, cmp_w2, cmp_b2, w_router, b_router, w_exp_up, w_exp_down, w_sh_up, w_sh_down) = w
    t, d = x_p.shape
    bs, ts, _ = x_s.shape
    assert c_p.shape[0] == 1 and t % (8 * MOBA_BLOCK) == 0 and win_state.shape[1] == WINDOW
    n_s = bs * ts
    rows = -(-(1 + bs) // 8) * 8
    c_all = jnp.zeros((rows, d), F32).at[0:1].set(c_p).at[1:1 + bs].set(c_s)
    mod = _ada(c_all, w_ada, b_ada).reshape(rows, 6, d)
    mod_p = [mod[0:1, i] for i in range(6)]
    mod_s = [jnp.repeat(mod[1:1 + bs, i], ts, axis=0) for i in range(6)]

    w_pad = jnp.zeros((d, _PROJ_CUTS[-1]), F32).at[:, :w_in.shape[1]].set(w_in).astype(BF16)
    cw = _compress_weights(cmp_pe, cmp_w1, cmp_b1, cmp_w2, cmp_b2)
    w_out_bf = w_out.astype(BF16)

    proj_p = _inproj(x_p, mod_p[1], mod_p[0], g_mix_pre, w_pad)
    proj_s = _inproj(x_s.reshape(n_s, d), mod_s[1], mod_s[0], g_mix_pre, w_pad)

    o_p = _prompt_mixer(proj_p, cw)
    win_t = win_state.transpose(0, 2, 3, 1).reshape(bs, 2 * G_NSA * D_HEAD, WINDOW)
    o_s = _sample_mixer(proj_s, cache_a, cache_b, win_t, page_table, cw, bs, ts)

    x1_p, h2_p = _post(x_p, *o_p, proj_p[5], mod_p[2], mod_p[4], mod_p[3], g_out_moba, g_out_nsa, g_mix_post,
                       g_ffn_pre, w_out_bf)
    x1_s, h2_s = _post(x_s.reshape(n_s, d), *o_s, proj_s[5], mod_s[2], mod_s[4], mod_s[3], g_out_moba, g_out_nsa,
                       g_mix_post, g_ffn_pre, w_out_bf)

    n_valid = t + n_s
    n_all = -(-n_valid // 128) * 128
    pad = lambda a: jnp.concatenate([a, jnp.zeros((n_all - n_valid, d), F32)], axis=0) if n_all > n_valid else a
    h_all = pad(jnp.concatenate([h2_p, h2_s], axis=0))
    x1_all = pad(jnp.concatenate([x1_p, x1_s], axis=0))
    g2_tab = jnp.concatenate([jnp.broadcast_to(mod_p[5], (128, d)), mod_s[5],
                              jnp.zeros((n_all - n_valid, d), F32)], axis=0)
    y_all = _moe(h_all, x1_all, g2_tab, t, n_valid, w_router, b_router, w_exp_up, w_exp_down, w_sh_up, w_sh_down,
                 g_ffn_post)
    y_p = y_all[:t]
    y_s = y_all[t:n_valid].reshape(bs, ts, d)

    _, kva_p, _, kvb_p, kvw_p, _ = proj_p
    _, kva_s, _, kvb_s, kvw_s, _ = proj_s
    state_p = (kva_p.reshape(1, t, 2 * H_MOBA, D_HEAD), kvb_p.reshape(1, t, 4 * G_NSA, D_HEAD),
               kvw_p[t - min(WINDOW, t):].reshape(1, min(WINDOW, t), 2 * G_NSA, D_HEAD))
    win_new = jnp.concatenate([win_state, kvw_s.reshape(bs, ts, 2 * G_NSA, D_HEAD)], axis=1)[:, -WINDOW:]
    state_s = (kva_s.reshape(bs, ts, 2 * H_MOBA, D_HEAD), kvb_s.reshape(bs, ts, 4 * G_NSA, D_HEAD), win_new)
    return y_p, y_s, state_p, state_s


def kernel(x_prompt, x_sample, cache_moba, cache_nsa, state_nsa_win, page_table, c_prompt, c_sample, w_ada, b_ada, g_mix_pre, g_mix_post, g_ffn_pre, g_ffn_post, w_in, g_out_moba, g_out_nsa, w_out, cmp_pe, cmp_w1, cmp_b1, cmp_w2, cmp_b2, w_router, b_router, w_exp_up, w_exp_down, w_sh_up, w_sh_down):
    weights = (w_ada, b_ada, g_mix_pre, g_mix_post, g_ffn_pre, g_ffn_post, w_in, g_out_moba, g_out_nsa, w_out,
               cmp_pe, cmp_w1, cmp_b1, cmp_w2, cmp_b2, w_router, b_router, w_exp_up, w_exp_down, w_sh_up, w_sh_down)
    depth = w_ada.shape[0]
    n_pool = cache_moba.shape[1]
    y_p, y_s = x_prompt[0], x_sample
    st_p, st_s = [], []
    for layer in range(depth):
        w_l = tuple(w[layer] for w in weights)
        cache_a = cache_moba[layer].transpose(0, 2, 3, 1).reshape(n_pool, 2 * H_MOBA * D_HEAD, PAGE_SIZE)
        cache_b = cache_nsa[layer].transpose(0, 2, 3, 1).reshape(n_pool, 4 * G_NSA * D_HEAD, PAGE_SIZE)
        y_p, y_s, sp, ss = _layer(y_p, y_s, cache_a, cache_b, state_nsa_win[layer], page_table, c_prompt, c_sample, w_l)
        st_p.append(sp)
        st_s.append(ss)
    stack = lambda sts, i: jnp.stack([s[i] for s in sts])
    return (y_p[None], y_s, stack(st_p, 0), stack(st_p, 1), stack(st_p, 2), stack(st_s, 0), stack(st_s, 1), stack(st_s, 2))
```

```python
import functools

import numpy as np
import jax
import jax.numpy as jnp
from jax import lax
from jax.experimental import pallas as pl
from jax.experimental.pallas import tpu as pltpu

F32, BF16, I32 = jnp.float32, jnp.bfloat16, jnp.int32
NEG_INF = float("-inf")
POS_INF = float("inf")
M_INIT = -1e30

D_HEAD = 64
H_MOBA = 8
H_NSA = 8
G_NSA = 2
R_NSA = H_NSA // G_NSA
MOBA_BLOCK = 256
MOBA_TOPK = 3
CMP_STRIDE = 16
CMP_LEN = 2 * CMP_STRIDE
CMP_HIDDEN = 2 * D_HEAD
SLC_BLOCK = 64
SLC_TOPK = 16
SLC_INIT = 1
SLC_LOCAL = 2
WINDOW = 512
N_EXPERTS = 256
TOPK_EXPERTS = 8
N_EXPERT_GROUPS = 8
TOPK_GROUPS = 4
ROUTED_SCALE = 2.5
PAGE_SIZE = 128
EPS = 1e-6
Q_SCALE = D_HEAD ** -0.5
LOG2E = 1.4426950408889634

_SLOPES = [2.0 ** (-8.0 * i / (H_MOBA + H_NSA)) for i in range(1, H_MOBA + H_NSA + 1)]
SLOPES_A = _SLOPES[0::2][:H_MOBA]
SLOPES_B = _SLOPES[1::2][:H_NSA]

LANES = 128
TQ = 256
TKF = 128
VMEM_LIMIT_BYTES = 56 * 1024 * 1024
ROW_BLOCK = 128
PAGES_PER_STEP = 16


def _cparams(*sem):
    return pltpu.CompilerParams(dimension_semantics=sem, vmem_limit_bytes=VMEM_LIMIT_BYTES)


def _rms(x, g):
    return x * lax.rsqrt(jnp.mean(x * x, axis=-1, keepdims=True) + EPS) * g


def _split_bf16(a):
    hi = a.astype(BF16)
    lo = (a - hi.astype(F32)).astype(BF16)
    return hi, lo


def _dot3(a, b, dims=None):
    ah, al = _split_bf16(a)
    bh, bl = _split_bf16(b)
    if dims is None:
        d = lambda x, y: jnp.dot(x, y, preferred_element_type=F32)
    else:
        d = lambda x, y: lax.dot_general(x, y, dims, preferred_element_type=F32)
    return d(ah, bh) + (d(ah, bl) + d(al, bh))


_NT = (((1,), (1,)), ((), ()))


def _div_pow2(x, n):
    assert n & (n - 1) == 0
    return jnp.right_shift(x, n.bit_length() - 1)


def _mod_pow2(x, n):
    assert n & (n - 1) == 0
    return jnp.bitwise_and(x, n - 1)


def _head_rows(q64, half):
    z = jnp.zeros_like(q64)
    return jnp.concatenate([z, q64] if half else [q64, z], axis=0)


def _ada_kernel(c_ref, w_ref, b_ref, o_ref):
    a = jax.nn.silu(c_ref[...])
    o_ref[...] = _dot3(a, w_ref[...]) + b_ref[...]


def _ada(c, w_ada, b_ada):
    r, d = c.shape
    n = w_ada.shape[1]
    tn = 768
    return pl.pallas_call(
        _ada_kernel,
        grid=(n // tn,),
        in_specs=[pl.BlockSpec((r, d), lambda j: (0, 0)),
                  pl.BlockSpec((d, tn), lambda j: (0, j)),
                  pl.BlockSpec((1, tn), lambda j: (0, j))],
        out_specs=pl.BlockSpec((r, tn), lambda j: (0, j)),
        out_shape=jax.ShapeDtypeStruct((r, n), F32),
        compiler_params=_cparams("arbitrary"),
        name="ada",
    )(c, w_ada, b_ada.reshape(1, n))


_PROJ_CUTS = (0, 512, 1536, 2048, 2560, 2816, 2944)


def _inproj_kernel(x_ref, sc_ref, sh_ref, g_ref, w_ref, *out_refs):
    h = _rms(x_ref[...], g_ref[...]) * (1.0 + sc_ref[...]) + sh_ref[...]
    hb = h.astype(BF16)
    for o_ref, a, b in zip(out_refs, _PROJ_CUTS[:-1], _PROJ_CUTS[1:]):
        o_ref[...] = jnp.dot(hb, w_ref[:, a:b], preferred_element_type=F32)


def _inproj(x, scale, shift, g, w_pad):
    r, d = x.shape
    tm = min(512, r)
    per_row = scale.shape[0] != 1
    mod_spec = pl.BlockSpec((tm, d), lambda i: (i, 0)) if per_row else pl.BlockSpec((1, d), lambda i: (0, 0))
    widths = [b - a for a, b in zip(_PROJ_CUTS[:-1], _PROJ_CUTS[1:])]
    return pl.pallas_call(
        _inproj_kernel,
        grid=(r // tm,),
        in_specs=[pl.BlockSpec((tm, d), lambda i: (i, 0)), mod_spec, mod_spec,
                  pl.BlockSpec((1, d), lambda i: (0, 0)),
                  pl.BlockSpec(w_pad.shape, lambda i: (0, 0))],
        out_specs=[pl.BlockSpec((tm, w), lambda i: (i, 0)) for w in widths],
        out_shape=[jax.ShapeDtypeStruct((r, w), F32) for w in widths],
        compiler_params=_cparams("arbitrary"),
        name="inproj",
    )(x, scale, shift, g.reshape(1, d), w_pad)


def _topk_axis0(score, k, idx):
    n = score.shape[0]
    sel = jnp.zeros(score.shape, F32)
    picks = []
    for _ in range(k):
        m = jnp.max(score, axis=0, keepdims=True)
        first = jnp.min(jnp.where(score == m, idx, n), axis=0, keepdims=True)
        pick = idx == first
        ok = m > NEG_INF
        sel = jnp.where(pick & ok, 1.0, sel)
        picks.append(jnp.where(ok, first, -1))
        score = jnp.where(pick, NEG_INF, score)
    return sel, picks


def _topk_axis1(score, k, idx):
    n = score.shape[1]
    out = []
    for _ in range(k):
        m = jnp.max(score, axis=1, keepdims=True)
        first = jnp.min(jnp.where(score == m, idx, n), axis=1, keepdims=True)
        out.append((first, m))
        score = jnp.where(idx == first, NEG_INF, score)
    return out


def _kmean_kernel(k_ref, o_ref):
    x = k_ref[...]
    n = x.shape[0] // MOBA_BLOCK
    o_ref[...] = jnp.sum(x.reshape(n, MOBA_BLOCK, x.shape[1]), axis=1) * (1.0 / MOBA_BLOCK)


def _kmean_prompt(kva):
    t = kva.shape[0]
    nb = t // MOBA_BLOCK
    per = 8
    return pl.pallas_call(
        _kmean_kernel,
        grid=(nb // per,),
        in_specs=[pl.BlockSpec((per * MOBA_BLOCK, 512), lambda i: (i, 0))],
        out_specs=pl.BlockSpec((per, 512), lambda i: (i, 0)),
        out_shape=jax.ShapeDtypeStruct((nb, 512), F32),
        compiler_params=_cparams("arbitrary"),
        name="kmean_prompt",
    )(kva)


def _moba_gate_kernel(qT_ref, km_ref, sel_ref, *, nb):
    own = pl.program_id(0)
    blk = lax.broadcasted_iota(I32, (nb, TQ), 0)
    for h in range(H_MOBA):
        p, half = divmod(h, 2)
        qh = _head_rows(qT_ref[h * 64:(h + 1) * 64, :], half)
        g = _dot3(km_ref[:, p * 128:(p + 1) * 128], qh)
        g = jnp.where(blk < own, g, NEG_INF)
        sel, _ = _topk_axis0(g, MOBA_TOPK, blk)
        sel_ref[h] = jnp.where(blk == own, 1.0, sel)


def _moba_gate_prompt(qT, kmean):
    t = qT.shape[1]
    nb = kmean.shape[0]
    return pl.pallas_call(
        functools.partial(_moba_gate_kernel, nb=nb),
        grid=(t // TQ,),
        in_specs=[pl.BlockSpec((512, TQ), lambda i: (0, i)),
                  pl.BlockSpec((nb, 512), lambda i: (0, 0))],
        out_specs=pl.BlockSpec((H_MOBA, nb, TQ), lambda i: (0, 0, i)),
        out_shape=jax.ShapeDtypeStruct((H_MOBA, nb, t), F32),
        compiler_params=_cparams("arbitrary"),
        name="moba_gate",
    )(qT, kmean)


def _flash_kernel(qT_ref, k_ref, vT_ref, *rest, heads, sub, sel_div, window):
    nh = len(heads)
    if window is None:
        sel_ref, o_ref, qh_scr, sc_scr, *state = rest
    else:
        sel_ref, (o_ref, qh_scr, sc_scr, *state) = None, rest
    m_scr, l_scr, acc_scr = state[:nh], state[nh:2 * nh], state[2 * nh:]
    i = pl.program_id(0)
    kio = lax.broadcasted_iota(I32, (TKF, TQ), 0)
    qio = lax.broadcasted_iota(I32, (TKF, TQ), 1)
    kiof = kio.astype(F32)
    rows = TKF // sub
    for h, (pair, half, set_idx, slope) in enumerate(heads):
        slope = slope * LOG2E
        qh_scr[h] = _head_rows(qT_ref[h * 64:(h + 1) * 64, :] * LOG2E, half).astype(BF16)
        sc_scr[h] = kiof * slope
        m_scr[h][...] = jnp.full((1, TQ), M_INIT, F32)
        l_scr[h][...] = jnp.zeros((1, TQ), F32)
        acc_scr[h][...] = jnp.zeros((64, TQ), F32)

    def sel_rows(set_idx, j):
        if sub == 1:
            return sel_ref[set_idx, pl.ds(j // sel_div, 1), :] > 0.0
        per8 = 8 // sub
        blk8 = sel_ref[set_idx, pl.ds(pl.multiple_of((j // per8) * 8, 8), 8), :]
        s = blk8[0:sub]
        for u in range(1, per8):
            s = jnp.where(j % per8 == u, blk8[u * sub:(u + 1) * sub], s)
        return s > 0.0

    def step(j, diag=None, band=None):
        off_base = (j * TKF - i * TQ).astype(F32)
        jc = j if band is None else jnp.maximum(j, 0)
        row0 = pl.multiple_of(jc * TKF, TKF)
        if band is not None:
            dist = qio - kio - band * TKF
            in_band = (dist >= 0) & (dist <= window) & (j >= 0)
        sel_cache = {}
        for h, (pair, half, set_idx, slope) in enumerate(heads):
            kj = k_ref[pl.ds(row0, TKF), pair * 128:(pair + 1) * 128]
            t = jnp.dot(kj, qh_scr[h], preferred_element_type=F32) + sc_scr[h]
            if diag is not None:
                t = jnp.where(kio + diag * TKF <= qio, t, NEG_INF)
            if band is not None:
                t = jnp.where(in_band, t, NEG_INF)
            off = (slope * LOG2E) * off_base
            t3 = t.reshape(sub, rows, TQ)
            m = m_scr[h][...]
            if sel_ref is None:
                m_new = jnp.maximum(m, jnp.max(jnp.max(t3, axis=1), axis=0, keepdims=True) + off)
                mu = jnp.broadcast_to(m_new - off, (sub, TQ))
            else:
                if set_idx not in sel_cache:
                    sel_cache[set_idx] = sel_rows(set_idx, j)
                selj = sel_cache[set_idx]
                mb = jnp.where(selj, jnp.max(t3, axis=1) + off, M_INIT)
                m_new = jnp.maximum(m, jnp.max(mb, axis=0, keepdims=True))
                mu = jnp.where(selj, m_new - off, POS_INF)
            p = jnp.exp2(t3 - mu[:, None, :]).reshape(TKF, TQ)
            alpha = jnp.exp2(m - m_new)
            l_scr[h][...] = l_scr[h][...] * alpha + jnp.sum(p, axis=0, keepdims=True)
            pv = jnp.dot(vT_ref[pair, jc], p.astype(BF16), preferred_element_type=F32)
            acc_scr[h][...] = acc_scr[h][...] * alpha + pv[half * 64:(half + 1) * 64]
            m_scr[h][...] = m_new

    per_q = TQ // TKF
    if window is None:
        def body(jj, c):
            for u in range(per_q):
                step(jj * per_q + u)
            return c

        lax.fori_loop(0, i, body, 0)
        for u in range(per_q):
            step(i * per_q + u, diag=u)
    else:
        for dj in range(-(window // TKF), per_q):
            step(i * per_q + dj, band=dj)
    for h in range(len(heads)):
        o_ref[h * 64:(h + 1) * 64, :] = acc_scr[h][...] / jnp.maximum(l_scr[h][...], 1e-30)


def _flash_select(qT, k, vTb, sel, heads, *, block, window=None):
    t = qT.shape[1]
    nh = len(heads)
    sub, sel_div = max(TKF // block, 1), max(block // TKF, 1)
    resident = lambda shape: pl.BlockSpec(shape, lambda i: (0,) * len(shape), pipeline_mode=pl.Buffered(1))
    in_specs = [pl.BlockSpec((nh * 64, TQ), lambda i: (0, i)), resident(k.shape), resident(vTb.shape)]
    args = (qT, k, vTb)
    if window is None:
        in_specs.append(pl.BlockSpec((sel.shape[0], sel.shape[1], TQ), lambda i: (0, 0, i)))
        args += (sel,)
    return pl.pallas_call(
        functools.partial(_flash_kernel, heads=heads, sub=sub, sel_div=sel_div, window=window),
        grid=(t // TQ,),
        in_specs=in_specs,
        out_specs=pl.BlockSpec((nh * 64, TQ), lambda i: (0, i)),
        out_shape=jax.ShapeDtypeStruct(qT.shape, F32),
        scratch_shapes=([pltpu.VMEM((nh, 128, TQ), BF16), pltpu.VMEM((nh, TKF, TQ), F32)]
                        + [pltpu.VMEM((1, TQ), F32)] * (2 * nh) + [pltpu.VMEM((64, TQ), F32)] * nh),
        compiler_params=_cparams("arbitrary"),
        name="flash_select" if window is None else "flash_window",
    )(*args)


HEADS_MOBA = tuple((h // 2, h % 2, h, SLOPES_A[h]) for h in range(H_MOBA))
HEADS_NSA = tuple((0, h // R_NSA, h // R_NSA, SLOPES_B[h]) for h in range(H_NSA))


def _compress_kernel(*refs, paged, n_rows):
    if paged:
        pt_ref, x_hbm, pe_ref, w1_ref, b1_ref, wcat_ref, w2_ref, b2_ref, o_ref, xs, sem, stage = refs
    else:
        x_hbm, pe_ref, w1_ref, b1_ref, wcat_ref, w2_ref, b2_ref, o_ref, xs, sem = refs
    nc = n_rows // CMP_STRIDE
    if paged:
        b = pl.program_id(0)
        n_pages = n_rows // PAGE_SIZE
        ch = stage.shape[1]

        def page_copy(c, u):
            return pltpu.make_async_copy(x_hbm.at[pt_ref[b, c * ch + u], pl.ds(0, 256), :],
                                         stage.at[c % 2, u], sem.at[c % 2])

        def start_chunk(c):
            def body(u, carry):
                page_copy(c, u).start()
                return carry
            lax.fori_loop(0, ch, body, 0)

        def finish_chunk(c):
            def wait_body(u, carry):
                page_copy(c, u).wait()
                return carry
            lax.fori_loop(0, ch, wait_body, 0)

            def body(u, carry):
                row0 = pl.multiple_of((c * ch + u) * PAGE_SIZE, PAGE_SIZE)
                for c2 in range(2):
                    xs[c2, pl.ds(row0, PAGE_SIZE), :] = stage[c % 2, u, c2 * LANES:(c2 + 1) * LANES, :].T
                return carry
            lax.fori_loop(0, ch, body, 0, unroll=4)

        start_chunk(0)
        xs[:, pl.ds(n_rows, CMP_STRIDE), :] = jnp.zeros((2, CMP_STRIDE, LANES), F32)
        for c in range(n_pages // ch):
            if c + 1 < n_pages // ch:
                start_chunk(c + 1)
            finish_chunk(c)
    else:
        cps = [pltpu.make_async_copy(x_hbm.at[:, pl.ds(c2 * LANES, LANES)], xs.at[c2, pl.ds(0, n_rows), :], sem.at[0])
               for c2 in range(2)]
        for cp in cps:
            cp.start()
        xs[:, pl.ds(n_rows, CMP_STRIDE), :] = jnp.zeros((2, CMP_STRIDE, LANES), F32)
        for cp in cps:
            cp.wait()

    def rows(r, c0, n):
        return jnp.concatenate([xs[c2, pl.ds(r + CMP_STRIDE * c0, n, stride=CMP_STRIDE), :] for c2 in range(2)],
                               axis=1).astype(BF16)

    consts = []
    for j in range(2):
        cj = _dot3(pe_ref[j], w1_ref[j])[0:1] + b1_ref[j]
        consts += [cj, cj]
    const = jnp.concatenate(consts, axis=1)
    rc = min(256, nc)
    for c0 in range(0, nc, rc):
        xcat = jnp.concatenate([rows(r, c0, rc) for r in range(CMP_LEN)], axis=1)
        hid = jnp.dot(xcat, wcat_ref[...], preferred_element_type=F32) + const
        act = jax.nn.gelu(hid)
        out = jnp.dot(act.astype(BF16), w2_ref[...], preferred_element_type=F32) + b2_ref[...]
        row = lax.broadcasted_iota(I32, out.shape, 0) + c0
        o_ref[0, c0:c0 + rc, :] = jnp.where(row < nc - 1, out, 0.0)


def _compress(x, page_table, cw, *, n_rows, n_batch):
    paged = page_table is not None
    nc = n_rows // CMP_STRIDE
    const = lambda shape: pl.BlockSpec(shape, lambda *_: (0,) * len(shape), pipeline_mode=pl.Buffered(1))
    in_specs = [pl.BlockSpec(memory_space=pl.ANY), const((2, 8, 2048)), const((2, 2048, 128)), const((2, 1, 128)),
                const((CMP_LEN * 256, 512)), const((512, 256)), const((1, 256))]
    scratch = [pltpu.VMEM((2, n_rows + CMP_STRIDE, LANES), F32), pltpu.SemaphoreType.DMA((2,))]
    if paged:
        chunk = min(32, n_rows // PAGE_SIZE)
        scratch.append(pltpu.VMEM((2, chunk, 256, PAGE_SIZE), F32))
    grid_spec = pltpu.PrefetchScalarGridSpec(
        num_scalar_prefetch=1 if paged else 0,
        grid=(n_batch,),
        in_specs=in_specs,
        out_specs=pl.BlockSpec((1, nc, 256), lambda b, *_: (b, 0, 0)),
        scratch_shapes=scratch)
    args = ((page_table,) if paged else ()) + (x,) + cw
    return pl.pallas_call(
        functools.partial(_compress_kernel, paged=paged, n_rows=n_rows),
        grid_spec=grid_spec,
        out_shape=jax.ShapeDtypeStruct((n_batch, nc, 256), F32),
        compiler_params=_cparams("arbitrary"),
        name="compress_paged" if paged else "compress",
    )(*args)


def _compress_weights(cmp_pe, cmp_w1, cmp_b1, cmp_w2, cmp_b2):
    half = CMP_STRIDE * D_HEAD
    wa = jnp.zeros((16, 256, 512), F32)
    wb = jnp.zeros((16, 256, 512), F32)
    w2 = jnp.zeros((512, 256), F32)
    for jg in range(4):
        j = jg // 2
        wa = wa.at[:, jg * 64:(jg + 1) * 64, jg * 128:(jg + 1) * 128].set(cmp_w1[j, :half].reshape(16, 64, 128))
        wb = wb.at[:, jg * 64:(jg + 1) * 64, jg * 128:(jg + 1) * 128].set(cmp_w1[j, half:].reshape(16, 64, 128))
        w2 = w2.at[jg * 128:(jg + 1) * 128, jg * 64:(jg + 1) * 64].set(cmp_w2[j])
    pe = jnp.zeros((2, 8, 2048), F32).at[:, 0].set(cmp_pe.reshape(2, 2048))
    b2 = jnp.concatenate([cmp_b2[0], cmp_b2[0], cmp_b2[1], cmp_b2[1]]).reshape(1, 256)
    wcat = jnp.concatenate([wa.reshape(16 * 256, 512), wb.reshape(16 * 256, 512)], axis=0)
    return (pe, cmp_w1, cmp_b1.reshape(2, 1, 128), wcat.astype(BF16), w2.astype(BF16), b2)


def _overlap_matrix(ns, nc_pad):
    ratio = SLC_BLOCK // CMP_STRIDE
    lr = CMP_LEN // CMP_STRIDE
    w = np.zeros((ns, nc_pad), np.float32)
    for s in range(ns):
        for m in range(ratio):
            for n in range(lr):
                c = ratio * s + m - n
                if 0 <= c < nc_pad - 1:
                    w[s, c] += 1.0
    return w


def _cmp_attn_kernel(qT_ref, pos_ref, kc_ref, vcT_ref, wT_ref, oc_ref, sel_ref, idx_ref, *, k_sel, tq):
    nc_pad = kc_ref.shape[1]
    ns = wT_ref.shape[0]
    pos = pos_ref[...]
    cidx = lax.broadcasted_iota(I32, (nc_pad, tq), 0)
    cpos = cidx * CMP_STRIDE + (CMP_LEN - 1)
    valid = (cpos <= pos) & (cidx < nc_pad - 1)
    ndist = (cpos - pos).astype(F32)
    blk = lax.broadcasted_iota(I32, (ns, tq), 0)
    d = jnp.right_shift(pos, 6) - blk
    forced = (blk < SLC_INIT) | ((d >= 0) & (d < SLC_LOCAL))
    kc = kc_ref[0]
    vcT = vcT_ref[0]
    for g in range(G_NSA):
        imp = jnp.zeros((nc_pad, tq), F32)
        for r in range(R_NSA):
            h = g * R_NSA + r
            qh = _head_rows(qT_ref[h * 64:(h + 1) * 64, :], g).astype(BF16)
            s = jnp.dot(kc, qh, preferred_element_type=F32) + ndist * SLOPES_B[h]
            s = jnp.where(valid, s, NEG_INF)
            m = jnp.max(s, axis=0, keepdims=True)
            m = jnp.where(m > NEG_INF, m, 0.0)
            p = jnp.exp(s - m)
            p = p * (1.0 / jnp.maximum(jnp.sum(p, axis=0, keepdims=True), 1e-30))
            imp = imp + p
            oc = jnp.dot(vcT, p.astype(BF16), preferred_element_type=F32)
            oc_ref[h * 64:(h + 1) * 64, :] = oc[g * 64:(g + 1) * 64]
        hi, lo = _split_bf16(imp)
        wT = wT_ref[...]
        islc = jnp.dot(wT, hi, preferred_element_type=F32) + jnp.dot(wT, lo, preferred_element_type=F32)
        score = jnp.where(forced, POS_INF, jnp.where(d >= 0, islc, NEG_INF))
        sel, picks = _topk_axis0(score, k_sel, blk)
        sel_ref[g] = sel
        for r, pk in enumerate(picks):
            idx_ref[g, r:r + 1, :] = pk
        for r in range(len(picks), SLC_TOPK):
            idx_ref[g, r:r + 1, :] = jnp.full((1, tq), -1, I32)


def _cmp_attn(qT, pos, kc, vcT, ns, *, k_sel, tq, tiles_per_batch):
    n = qT.shape[1]
    nc_pad = kc.shape[1]
    wT = jnp.asarray(_overlap_matrix(ns, nc_pad), BF16)
    bmap = (lambda i: (i // tiles_per_batch, 0, 0)) if tiles_per_batch else (lambda i: (0, 0, 0))
    return pl.pallas_call(
        functools.partial(_cmp_attn_kernel, k_sel=k_sel, tq=tq),
        grid=(n // tq,),
        in_specs=[pl.BlockSpec((512, tq), lambda i: (0, i)),
                  pl.BlockSpec((1, tq), lambda i: (0, i)),
                  pl.BlockSpec((1, nc_pad, 128), bmap),
                  pl.BlockSpec((1, 128, nc_pad), bmap),
                  pl.BlockSpec((ns, nc_pad), lambda i: (0, 0))],
        out_specs=[pl.BlockSpec((512, tq), lambda i: (0, i)),
                   pl.BlockSpec((G_NSA, ns, tq), lambda i: (0, 0, i)),
                   pl.BlockSpec((G_NSA, SLC_TOPK, tq), lambda i: (0, 0, i))],
        out_shape=[jax.ShapeDtypeStruct((512, n), F32),
                   jax.ShapeDtypeStruct((G_NSA, ns, n), F32),
                   jax.ShapeDtypeStruct((G_NSA, SLC_TOPK, n), I32)],
        compiler_params=_cparams("arbitrary"),
        name="cmp_attn",
    )(qT, pos, kc, vcT, wT)


def _post_kernel(x_ref, oa_ref, oc_ref, os_ref, ow_ref, gb_ref, g1_ref, sc2_ref, sh2_ref,
                 goa_ref, gob_ref, gpost_ref, gffn_ref, e3_ref, wout_ref, x1_ref, h2_ref):
    gates = jax.nn.sigmoid(gb_ref[...])
    ob = (_dot3(gates, e3_ref[0]) * oc_ref[...] + _dot3(gates, e3_ref[1]) * os_ref[...]
          + _dot3(gates, e3_ref[2]) * ow_ref[...])
    oa_n = _rms(oa_ref[...], goa_ref[...]).astype(BF16)
    ob_n = _rms(ob, gob_ref[...]).astype(BF16)
    o = (jnp.dot(oa_n, wout_ref[0:512, :], preferred_element_type=F32)
         + jnp.dot(ob_n, wout_ref[512:1024, :], preferred_element_type=F32))
    x1 = x_ref[...] + g1_ref[...] * _rms(o, gpost_ref[...])
    x1_ref[...] = x1
    h2_ref[...] = _rms(x1, gffn_ref[...]) * (1.0 + sc2_ref[...]) + sh2_ref[...]


def _gate_expand():
    e = np.zeros((3, 128, 512), np.float32)
    for h in range(H_NSA):
        for j in range(3):
            e[j, 3 * h + j, h * 64:(h + 1) * 64] = 1.0
    return jnp.asarray(e)


def _post(x, oa, oc, os_, ow, gb, gate1, scale2, shift2, g_oa, g_ob, g_post, g_ffn, w_out_bf):
    r, d = x.shape
    tm = min(256, r)
    per_row = gate1.shape[0] != 1
    mod = pl.BlockSpec((tm, d), lambda i: (i, 0)) if per_row else pl.BlockSpec((1, d), lambda i: (0, 0))
    row = lambda w: pl.BlockSpec((tm, w), lambda i: (i, 0))
    const = lambda shape: pl.BlockSpec(shape, lambda i: (0,) * len(shape))
    return pl.pallas_call(
        _post_kernel,
        grid=(r // tm,),
        in_specs=[row(d), row(512), row(512), row(512), row(512), row(128), mod, mod, mod,
                  const((1, 512)), const((1, 512)), const((1, d)), const((1, d)),
                  const((3, 128, 512)), const((1024, d))],
        out_specs=[row(d), row(d)],
        out_shape=[jax.ShapeDtypeStruct((r, d), F32), jax.ShapeDtypeStruct((r, d), F32)],
        compiler_params=_cparams("arbitrary"),
        name="post_attn",
    )(x, oa, oc, os_, ow, gb, gate1, scale2, shift2, g_oa.reshape(1, 512), g_ob.reshape(1, 512),
      g_post.reshape(1, d), g_ffn.reshape(1, d), _gate_expand(), w_out_bf)


def _router_kernel(h_ref, wr_ref, br_ref, tri_ref, idx_ref, wt_ref, rank_ref, cnt_ref, run_ref, *, n_valid, tm):
    i = pl.program_id(0)

    @pl.when(i == 0)
    def _():
        run_ref[...] = jnp.zeros_like(run_ref)

    s = jax.nn.sigmoid(_dot3(h_ref[...], wr_ref[...]))
    sel = s + br_ref[...]
    lane = lax.broadcasted_iota(I32, (tm, N_EXPERTS), 1)
    per = N_EXPERTS // N_EXPERT_GROUPS
    grp = jnp.right_shift(lane, 5)
    lane_g = lax.broadcasted_iota(I32, (tm, LANES), 1)
    gscore = jnp.full((tm, LANES), NEG_INF, F32)
    for g in range(N_EXPERT_GROUPS):
        v = jnp.where(grp == g, sel, NEG_INF)
        (i1, m1), (_, m2) = _topk_axis1(v, 2, lane)
        gscore = jnp.where(lane_g == g, m1 + m2, gscore)
    emask = jnp.zeros((tm, N_EXPERTS), jnp.bool_)
    for first, _ in _topk_axis1(gscore, TOPK_GROUPS, lane_g):
        emask = emask | (grp == first)
    picks = _topk_axis1(jnp.where(emask, sel, NEG_INF), TOPK_EXPERTS, lane)
    row = lax.broadcasted_iota(I32, (tm, 1), 0) + i * tm
    row_ok = row < n_valid
    onehot = jnp.zeros((tm, N_EXPERTS), F32)
    ws = []
    for first, _ in picks:
        pick = lane == first
        ws.append(jnp.sum(jnp.where(pick, s, 0.0), axis=1, keepdims=True))
        onehot = jnp.where(pick & row_ok, 1.0, onehot)
    wsum = ws[0]
    for w in ws[1:]:
        wsum = wsum + w
    excl = jnp.dot(tri_ref[...], onehot.astype(BF16), preferred_element_type=F32) + run_ref[...]
    idx_out = jnp.zeros((tm, LANES), I32)
    wt_out = jnp.zeros((tm, LANES), F32)
    rank_out = jnp.zeros((tm, LANES), I32)
    for k, (first, _) in enumerate(picks):
        rk = jnp.sum(jnp.where(lane == first, excl, 0.0), axis=1, keepdims=True)
        idx_out = jnp.where(lane_g == k, first, idx_out)
        wt_out = jnp.where(lane_g == k, ws[k] / wsum * ROUTED_SCALE, wt_out)
        rank_out = jnp.where(lane_g == k, rk.astype(I32), rank_out)
    idx_ref[...] = idx_out
    wt_ref[...] = wt_out
    rank_ref[...] = rank_out
    run_ref[...] = run_ref[...] + jnp.sum(onehot, axis=0, keepdims=True)
    cnt_ref[...] = run_ref[...]


def _router(h_all, w_router, b_router, n_valid):
    n, d = h_all.shape
    tm = 128
    tri = jnp.asarray(np.tril(np.ones((tm, tm), np.float32), -1), BF16)
    const = lambda shape: pl.BlockSpec(shape, lambda i: (0,) * len(shape))
    row = lambda w: pl.BlockSpec((tm, w), lambda i: (i, 0))
    return pl.pallas_call(
        functools.partial(_router_kernel, n_valid=n_valid, tm=tm),
        grid=(n // tm,),
        in_specs=[row(d), const((d, N_EXPERTS)), const((1, N_EXPERTS)), const((tm, tm))],
        out_specs=[row(LANES), row(LANES), row(LANES), const((1, N_EXPERTS))],
        out_shape=[jax.ShapeDtypeStruct((n, LANES), I32), jax.ShapeDtypeStruct((n, LANES), F32),
                   jax.ShapeDtypeStruct((n, LANES), I32), jax.ShapeDtypeStruct((1, N_EXPERTS), F32)],
        scratch_shapes=[pltpu.VMEM((1, N_EXPERTS), F32)],
        compiler_params=_cparams("arbitrary"),
        name="router",
    )(h_all, w_router, b_router.reshape(1, N_EXPERTS), tri)


def _gather_rows(idx_hbm, idx_smem, isem, src_hbm, buf, sem, step, n_steps, n_rows):
    def idx_copy(k, slot):
        return pltpu.make_async_copy(idx_hbm.at[k], idx_smem.at[slot], isem.at[slot])

    def issue(slot):
        def body(r2, c):
            for prio in range(2):
                r = 2 * r2 + prio
                pltpu.make_async_copy(src_hbm.at[idx_smem[slot, r]], buf.at[slot, r], sem.at[slot]).start(priority=prio)
            return c
        lax.fori_loop(0, n_rows // 2, body, 0, unroll=4)

    @pl.when(step == 0)
    def _():
        idx_copy(0, 0).start()
        idx_copy(0, 0).wait()
        issue(0)

        @pl.when(1 < n_steps)
        def _():
            idx_copy(1, 1).start()

    nxt = (step + 1) % 2

    @pl.when(step + 1 < n_steps)
    def _():
        idx_copy(step + 1, nxt).wait()
        issue(nxt)

    @pl.when(step + 2 < n_steps)
    def _():
        idx_copy(step + 2, step % 2).start()

    slot = step % 2

    @pl.when(step < n_steps)
    def _():
        pltpu.make_async_copy(src_hbm.at[pl.ds(0, n_rows)], buf.at[slot], sem.at[slot]).wait()
    return slot


def _expert_kernel(blk_e_ref, nused_ref, tok_hbm, h_hbm, wup_ref, wdn_ref, o_ref, buf0, buf1, idx_smem, sem, isem,
                   wup_bf, wdn_bf):
    i = pl.program_id(0)
    nused = nused_ref[0]
    rb = ROW_BLOCK
    bufs = (buf0, buf1)

    def idx_copy(k, slot):
        return pltpu.make_async_copy(tok_hbm.at[k], idx_smem.at[slot], isem.at[slot])

    def issue_rows(slot):
        for r in range(rb):
            pltpu.make_async_copy(h_hbm.at[idx_smem[slot, r]], bufs[slot].at[r], sem.at[slot]).start()

    def wait_rows(slot):
        pltpu.make_async_copy(h_hbm.at[pl.ds(0, rb)], bufs[slot], sem.at[slot]).wait()

    def compute(slot):
        buf = bufs[slot]
        x = jnp.concatenate([buf[:, s, :] for s in range(8)], axis=1).astype(BF16)
        up = jnp.dot(x, wup_bf[...], preferred_element_type=F32)
        half = up.shape[1] // 2
        act = (jax.nn.silu(up[:, :half]) * up[:, half:]).astype(BF16)
        y = jnp.dot(act, wdn_bf[...], preferred_element_type=F32)
        for s in range(8):
            o_ref[:, s, :] = y[:, s * 128:(s + 1) * 128]

    @pl.when(i == 0)
    def _():
        idx_copy(0, 0).start()
        idx_copy(0, 0).wait()
        issue_rows(0)

        @pl.when(1 < nused)
        def _():
            idx_copy(1, 1).start()

    @pl.when(i + 2 < nused)
    def _():
        idx_copy(i + 2, i % 2).start()

    new_expert = (i == 0) | (blk_e_ref[i] != blk_e_ref[jnp.maximum(i - 1, 0)])

    @pl.when(new_expert & (i < nused))
    def _():
        wup_bf[...] = wup_ref[0].astype(BF16)
        wdn_bf[...] = wdn_ref[0].astype(BF16)

    has_next = i + 1 < nused
    for slot in range(2):
        mine = (i % 2) == slot

        @pl.when(mine & has_next)
        def _():
            idx_copy(i + 1, 1 - slot).wait()
            issue_rows(1 - slot)
            wait_rows(slot)
            compute(slot)

        @pl.when(mine & (i < nused) & jnp.logical_not(has_next))
        def _():
            wait_rows(slot)
            compute(slot)

    @pl.when(i >= nused)
    def _():
        o_ref[...] = jnp.zeros_like(o_ref)


def _experts(h3, blk_e, row_tok, nused, w_exp_up, w_exp_down):
    nblk = blk_e.shape[0]
    rb = ROW_BLOCK
    e, d, f2 = w_exp_up.shape
    grid_spec = pltpu.PrefetchScalarGridSpec(
        num_scalar_prefetch=2,
        grid=(nblk,),
        in_specs=[pl.BlockSpec(memory_space=pl.ANY), pl.BlockSpec(memory_space=pl.ANY),
                  pl.BlockSpec((1, d, f2), lambda i, be, nu: (be[i], 0, 0)),
                  pl.BlockSpec((1, f2 // 2, d), lambda i, be, nu: (be[i], 0, 0))],
        out_specs=pl.BlockSpec((rb, 8, 128), lambda i, be, nu: (i, 0, 0)),
        scratch_shapes=[pltpu.VMEM((rb, 8, 128), F32), pltpu.VMEM((rb, 8, 128), F32), pltpu.SMEM((2, rb), I32),
                        pltpu.SemaphoreType.DMA((2,)), pltpu.SemaphoreType.DMA((2,)),
                        pltpu.VMEM((d, f2), BF16), pltpu.VMEM((f2 // 2, d), BF16)])
    return pl.pallas_call(
        _expert_kernel,
        grid_spec=grid_spec,
        out_shape=jax.ShapeDtypeStruct((nblk * rb, 8, 128), F32),
        compiler_params=_cparams("arbitrary"),
        name="experts",
    )(blk_e, nused, row_tok.reshape(nblk, rb), h3, w_exp_up, w_exp_down)


def _combine_kernel(dest_hbm, y_hbm, wt_ref, h_ref, x1_ref, g2_ref, gpost_ref, wsu_ref, wsd_ref, o_ref,
                    buf, idx_smem, sem, isem, *, tm):
    i = pl.program_id(0)
    slot = _gather_rows(dest_hbm, idx_smem, isem, y_hbm, buf, sem, i, pl.num_programs(0), tm * TOPK_EXPERTS)
    wt = wt_ref[...]
    f = jnp.zeros((tm, h_ref.shape[1]), F32)
    for k in range(TOPK_EXPERTS):
        yk = jnp.concatenate([buf[slot, pl.ds(k, tm, stride=8), s, :] for s in range(8)], axis=1)
        f = f + yk * wt[:, k:k + 1]
    hb = h_ref[...].astype(BF16)
    up = jnp.dot(hb, wsu_ref[...], preferred_element_type=F32)
    half = up.shape[1] // 2
    act = (jax.nn.silu(up[:, :half]) * up[:, half:]).astype(BF16)
    f = f + jnp.dot(act, wsd_ref[...], preferred_element_type=F32)
    o_ref[...] = x1_ref[...] + g2_ref[...] * _rms(f, gpost_ref[...])


def _combine(dest, y, wt, h_all, x1_all, gate2_tab, n_prompt, g_ffn_post, w_sh_up_bf, w_sh_down_bf):
    n, d = h_all.shape
    tm = 128
    assert n_prompt % tm == 0
    const = lambda shape: pl.BlockSpec(shape, lambda i: (0,) * len(shape))
    row = lambda w: pl.BlockSpec((tm, w), lambda i: (i, 0))
    gate_spec = pl.BlockSpec((tm, d), lambda i: (jnp.maximum(i - (n_prompt // tm - 1), 0), 0))
    na = tm * TOPK_EXPERTS
    return pl.pallas_call(
        functools.partial(_combine_kernel, tm=tm),
        grid=(n // tm,),
        in_specs=[pl.BlockSpec(memory_space=pl.ANY), pl.BlockSpec(memory_space=pl.ANY), row(LANES), row(d), row(d),
                  gate_spec, const((1, d)), const(w_sh_up_bf.shape), const(w_sh_down_bf.shape)],
        out_specs=row(d),
        scratch_shapes=[pltpu.VMEM((2, na, 8, 128), F32), pltpu.SMEM((2, na), I32),
                        pltpu.SemaphoreType.DMA((2,)), pltpu.SemaphoreType.DMA((2,))],
        out_shape=jax.ShapeDtypeStruct((n, d), F32),
        compiler_params=_cparams("arbitrary"),
        name="moe_combine",
    )(dest.reshape(n // tm, na), y, wt, h_all, x1_all, gate2_tab, g_ffn_post.reshape(1, d),
      w_sh_up_bf, w_sh_down_bf)


def _dest_kernel(idx_ref, rank_ref, ps_ref, o_ref):
    tm = idx_ref.shape[0]
    idxf = idx_ref[...].astype(F32)
    ps = ps_ref[...].astype(F32)
    lane_g = lax.broadcasted_iota(I32, (tm, LANES), 1)
    lane = lax.broadcasted_iota(I32, (tm, N_EXPERTS), 1).astype(F32)
    base = jnp.zeros((tm, LANES), F32)
    for k in range(TOPK_EXPERTS):
        idx_k = jnp.sum(jnp.where(lane_g == k, idxf, 0.0), axis=1, keepdims=True)
        ps_k = jnp.sum(jnp.where(lane == idx_k, ps, 0.0), axis=1, keepdims=True)
        base = jnp.where(lane_g == k, ps_k, base)
    o_ref[...] = base.astype(I32) + rank_ref[...]


def _moe_dest(idx_p, rank_p, p_start):
    n = idx_p.shape[0]
    tm = 128
    row = pl.BlockSpec((tm, LANES), lambda i: (i, 0))
    return pl.pallas_call(
        _dest_kernel,
        grid=(n // tm,),
        in_specs=[row, row, pl.BlockSpec((1, N_EXPERTS), lambda i: (0, 0))],
        out_specs=row,
        out_shape=jax.ShapeDtypeStruct((n, LANES), I32),
        compiler_params=_cparams("arbitrary"),
        name="moe_dest",
    )(idx_p, rank_p, p_start.reshape(1, N_EXPERTS))


def _moe(h_all, x1_all, gate2_tab, n_prompt, n_valid, w_router, b_router, w_exp_up, w_exp_down, w_sh_up, w_sh_down,
         g_ffn_post):
    n, d = h_all.shape
    rb = ROW_BLOCK
    e = N_EXPERTS
    idx_p, wt_p, rank_p, cnt = _router(h_all, w_router, b_router, n_valid)
    counts = cnt[0].astype(I32)
    padded = (counts + rb - 1) // rb * rb
    p_end = jnp.cumsum(padded)
    p_start = p_end - padded
    dest = _moe_dest(idx_p, rank_p, p_start)[:, :TOPK_EXPERTS]
    nblk = -(-(n_valid * TOPK_EXPERTS + e * (rb - 1)) // rb)
    tok = jnp.repeat(jnp.arange(n_valid, dtype=I32), TOPK_EXPERTS)
    row_tok = jnp.zeros((nblk * rb,), I32).at[dest[:n_valid].reshape(-1)].set(tok)
    blk_start = jnp.arange(nblk, dtype=I32) * rb
    blk_e = jnp.minimum(jnp.sum((p_end[None, :] <= blk_start[:, None]).astype(I32), axis=1), e - 1)
    nused = (p_end[-1] // rb).astype(I32).reshape(1)
    y = _experts(h_all.reshape(n, 8, 128), blk_e, row_tok, nused, w_exp_up, w_exp_down)
    dest = jnp.where(jnp.arange(n, dtype=I32)[:, None] < n_valid, dest, 0)
    return _combine(dest, y, wt_p, h_all, x1_all, gate2_tab, n_prompt, g_ffn_post,
                    w_sh_up.astype(BF16), w_sh_down.astype(BF16))


def _kmean_pages_kernel(pt_ref, *refs):
    pages, o_ref = refs[:-1], refs[-1]
    s_idx = pl.program_id(1)
    per_step = len(pages) // 2

    @pl.when(s_idx == 0)
    def _():
        o_ref[...] = jnp.zeros_like(o_ref)

    lane = lax.broadcasted_iota(I32, (512, LANES), 1)
    acc = o_ref[0]
    for u in range(per_step):
        tot = jnp.sum(pages[2 * u][0] + pages[2 * u + 1][0], axis=1, keepdims=True)
        acc = jnp.where(lane == s_idx * per_step + u, tot * (1.0 / MOBA_BLOCK), acc)
    o_ref[0] = acc


def _kmean_pages(cache_a, page_table):
    bs, n_pages = page_table.shape
    pps = PAGES_PER_STEP
    assert n_pages // 2 <= LANES
    in_specs = [pl.BlockSpec((1, 512, PAGE_SIZE), lambda b, s, pt, u=u: (pt[b, s * pps + u], 0, 0)) for u in range(pps)]
    grid_spec = pltpu.PrefetchScalarGridSpec(
        num_scalar_prefetch=1, grid=(bs, n_pages // pps), in_specs=in_specs,
        out_specs=pl.BlockSpec((1, 512, LANES), lambda b, s, pt: (b, 0, 0)))
    return pl.pallas_call(
        _kmean_pages_kernel, grid_spec=grid_spec,
        out_shape=jax.ShapeDtypeStruct((bs, 512, LANES), F32),
        compiler_params=_cparams("arbitrary", "arbitrary"),
        name="kmean_pages",
    )(page_table, *([cache_a] * pps))


def _moba_sample_kernel(pt_ref, qbd_ref, km_ref, new_ref, *refs, nb, ts, past):
    pps = PAGES_PER_STEP
    pages = refs[:pps]
    o_ref, sel_scr, m_scr, l_scr, acc_scr = refs[pps:]
    s_idx = pl.program_id(1)
    rows = H_MOBA * ts
    qbd = qbd_ref[0]
    qb = qbd.astype(BF16)
    rowi = lax.broadcasted_iota(I32, (rows, 1), 0)
    slope = jnp.zeros((rows, 1), F32)
    for h in range(H_MOBA):
        slope = jnp.where(_div_pow2(rowi, ts) == h, SLOPES_A[h], slope)
    lane = lax.broadcasted_iota(I32, (rows, LANES), 1)

    @pl.when(s_idx == 0)
    def _():
        gate = _dot3(qbd, km_ref[0])
        gate = jnp.where(lane < nb, gate, NEG_INF)
        sel = jnp.zeros((rows, LANES), F32)
        for first, m in _topk_axis1(gate, MOBA_TOPK, lane):
            sel = jnp.where((lane == first) & (m > NEG_INF), 1.0, sel)
        sel_scr[...] = sel
        m_scr[...] = jnp.full_like(m_scr, M_INIT)
        l_scr[...] = jnp.zeros_like(l_scr)
        acc_scr[...] = jnp.zeros_like(acc_scr)

    def update(ss, vTs):
        m = m_scr[...]
        m_new = m
        for s in ss:
            m_new = jnp.maximum(m_new, jnp.max(s, axis=1, keepdims=True))
        alpha = jnp.exp(m - m_new)
        lsum = jnp.zeros_like(m)
        pv = jnp.zeros(acc_scr.shape, F32)
        for s, vT in zip(ss, vTs):
            p = jnp.exp(s - m_new)
            lsum = lsum + jnp.sum(p, axis=1, keepdims=True)
            pv = pv + lax.dot_general(p.astype(BF16), vT(), _NT, preferred_element_type=F32)
        l_scr[...] = l_scr[...] * alpha + lsum
        acc_scr[...] = acc_scr[...] * alpha + pv
        m_scr[...] = m_new

    sel = sel_scr[...]
    ss, vTs = [], []
    for u in range(pps):
        page = s_idx * pps + u
        flag = jnp.sum(jnp.where(lane == page // 2, sel, 0.0), axis=1, keepdims=True) > 0.0
        kT = pages[u][0, 0:512, :].astype(BF16)
        s = jnp.dot(qb, kT, preferred_element_type=F32)
        kpos = page * PAGE_SIZE + lane - past
        ss.append(jnp.where(flag, s + slope * kpos.astype(F32), NEG_INF))
        vTs.append(lambda u=u: pages[u][0, 512:1024, :].astype(BF16))
    update(ss, vTs)

    @pl.when(s_idx == pl.num_programs(1) - 1)
    def _():
        kn = new_ref[0, 0:512, :].astype(BF16)
        s = jnp.dot(qb, kn, preferred_element_type=F32)
        ok = (lane < ts) & (lane <= _mod_pow2(rowi, ts))
        s = jnp.where(ok, s + slope * lane.astype(F32), NEG_INF)
        update([s], [lambda: new_ref[0, 512:1024, :].astype(BF16)])
        o = acc_scr[...] / jnp.maximum(l_scr[...], 1e-30)
        col_h = _div_pow2(lax.broadcasted_iota(I32, (rows, 512), 1), D_HEAD)
        o = jnp.where(col_h == _div_pow2(rowi, ts), o, 0.0)
        pick_r = lax.broadcasted_iota(I32, (8, rows), 0)
        pick_c = lax.broadcasted_iota(I32, (8, rows), 1)
        gather = jnp.where(_mod_pow2(pick_c, ts) == pick_r, 1.0, 0.0)
        o_ref[0] = _dot3(gather, o)


def _moba_sample(qbd, km_pad, new_pad, cache_a, page_table, *, nb, ts, past):
    bs, n_pages = page_table.shape
    pps = PAGES_PER_STEP
    rows = H_MOBA * ts
    in_specs = [pl.BlockSpec((1, rows, 512), lambda b, s, pt: (b, 0, 0)),
                pl.BlockSpec((1, 512, LANES), lambda b, s, pt: (b, 0, 0)),
                pl.BlockSpec((1, 1024, LANES), lambda b, s, pt: (b, 0, 0))]
    in_specs += [pl.BlockSpec((1, 1024, PAGE_SIZE), lambda b, s, pt, u=u: (pt[b, s * pps + u], 0, 0)) for u in range(pps)]
    grid_spec = pltpu.PrefetchScalarGridSpec(
        num_scalar_prefetch=1, grid=(bs, n_pages // pps), in_specs=in_specs,
        out_specs=pl.BlockSpec((1, 8, 512), lambda b, s, pt: (b, 0, 0)),
        scratch_shapes=[pltpu.VMEM((rows, LANES), F32), pltpu.VMEM((rows, 1), F32), pltpu.VMEM((rows, 1), F32),
                        pltpu.VMEM((rows, 512), F32)])
    return pl.pallas_call(
        functools.partial(_moba_sample_kernel, nb=nb, ts=ts, past=past),
        grid_spec=grid_spec,
        out_shape=jax.ShapeDtypeStruct((bs, 8, 512), F32),
        compiler_params=_cparams("arbitrary", "arbitrary"),
        name="moba_sample",
    )(page_table, qbd, km_pad, new_pad, *([cache_a] * pps))


def _nsa_sample_kernel(pt_ref, idx_ref, qs_ref, qw_ref, kpos_ref, new_ref, win_ref, wnew_ref, cache_hbm,
                       os_ref, ow_ref, buf, sem, *, ts, past, n_slots):
    b = pl.program_id(0)
    ngq = G_NSA * ts
    copies = []
    for gq in range(ngq):
        for t in range(n_slots):
            blk = jnp.maximum(idx_ref[(b * ngq + gq) * SLC_TOPK + t], 0)
            page = pt_ref[b, blk // 2]
            cp = pltpu.make_async_copy(cache_hbm.at[page, pl.ds(256, 256), :],
                                       buf.at[gq, :, pl.ds(t * PAGE_SIZE, PAGE_SIZE)], sem.at[0])
            cp.start()
            copies.append(cp)
    new = new_ref[0]
    for gq in range(ngq):
        buf[gq, :, pl.ds(n_slots * PAGE_SIZE, PAGE_SIZE)] = new
    for cp in copies:
        cp.wait()

    rowi = lax.broadcasted_iota(I32, (8, 1), 0)
    for gq in range(ngq):
        g, q = divmod(gq, ts)
        slope = jnp.zeros((8, 1), F32)
        for r in range(R_NSA):
            slope = jnp.where(rowi == r, SLOPES_B[g * R_NSA + r], slope)
        kT = buf[gq, 0:128, :].astype(BF16)
        vT = buf[gq, 128:256, :].astype(BF16)
        s = jnp.dot(qs_ref[0, gq].astype(BF16), kT, preferred_element_type=F32)
        kpos = kpos_ref[0, gq]
        ok = (kpos >= 0) & (kpos <= past + q)
        s = jnp.where(ok, s + slope * (kpos - past).astype(F32), NEG_INF)
        m = jnp.max(s, axis=1, keepdims=True)
        m = jnp.where(m > NEG_INF, m, 0.0)
        p = jnp.exp(s - m)
        p = p * (1.0 / jnp.maximum(jnp.sum(p, axis=1, keepdims=True), 1e-30))
        os_ref[0, gq] = lax.dot_general(p.astype(BF16), vT, _NT, preferred_element_type=F32)

    nw = WINDOW + LANES
    kw_all = jnp.concatenate([win_ref[0], wnew_ref[0]], axis=1)
    kw = kw_all[0:128, :].astype(BF16)
    vw = kw_all[128:256, :].astype(BF16)
    wrow = lax.broadcasted_iota(I32, (R_NSA * ts, 1), 0)
    wlane = lax.broadcasted_iota(I32, (R_NSA * ts, nw), 1)
    dist = (WINDOW + _mod_pow2(wrow, ts)) - wlane
    okw = (dist >= 0) & (dist <= WINDOW) & (wlane < WINDOW + ts)
    for g in range(G_NSA):
        slope = jnp.zeros((R_NSA * ts, 1), F32)
        for r in range(R_NSA):
            slope = jnp.where(_div_pow2(wrow, ts) == r, SLOPES_B[g * R_NSA + r], slope)
        s = jnp.dot(qw_ref[0, g].astype(BF16), kw, preferred_element_type=F32)
        s = jnp.where(okw, s - slope * dist.astype(F32), NEG_INF)
        m = jnp.max(s, axis=1, keepdims=True)
        m = jnp.where(m > NEG_INF, m, 0.0)
        p = jnp.exp(s - m)
        p = p * (1.0 / jnp.maximum(jnp.sum(p, axis=1, keepdims=True), 1e-30))
        ow_ref[0, g] = lax.dot_general(p.astype(BF16), vw, _NT, preferred_element_type=F32)


def _nsa_sample(idx_flat, qs, qw, kpos, new_pad, win, wnew_pad, cache_b, page_table, *, ts, past, n_slots):
    bs = page_table.shape[0]
    ngq = G_NSA * ts
    nk = (n_slots + 1) * PAGE_SIZE
    m4 = lambda shape: pl.BlockSpec(shape, lambda b, pt, ix: (b,) + (0,) * (len(shape) - 1))
    grid_spec = pltpu.PrefetchScalarGridSpec(
        num_scalar_prefetch=2, grid=(bs,),
        in_specs=[m4((1, ngq, 8, 128)), m4((1, G_NSA, R_NSA * ts, 128)), m4((1, ngq, 1, nk)),
                  m4((1, 256, LANES)), m4((1, 256, WINDOW)), m4((1, 256, LANES)),
                  pl.BlockSpec(memory_space=pl.ANY)],
        out_specs=[m4((1, ngq, 8, 128)), m4((1, G_NSA, R_NSA * ts, 128))],
        scratch_shapes=[pltpu.VMEM((ngq, 256, nk), F32), pltpu.SemaphoreType.DMA((1,))])
    return pl.pallas_call(
        functools.partial(_nsa_sample_kernel, ts=ts, past=past, n_slots=n_slots),
        grid_spec=grid_spec,
        out_shape=[jax.ShapeDtypeStruct((bs, ngq, 8, 128), F32),
                   jax.ShapeDtypeStruct((bs, G_NSA, R_NSA * ts, 128), F32)],
        compiler_params=_cparams("arbitrary"),
        name="nsa_sample",
    )(page_table, idx_flat, qs, qw, kpos, new_pad, win, wnew_pad, cache_b)


def _vT_blocks(v, tk):
    t, c = v.shape
    return v.astype(BF16).reshape(t // tk, tk, c // 128, 128).transpose(2, 0, 3, 1)


def _prompt_mixer(proj, cw):
    qa, kva, qb, kvb, kvw, _ = proj
    t = qa.shape[0]
    qaT = (qa * Q_SCALE).T
    qbT = (qb * Q_SCALE).T
    kmean = _kmean_prompt(kva)
    sel_a = _moba_gate_prompt(qaT, kmean)
    oaT = _flash_select(qaT, kva[:, :512].astype(BF16), _vT_blocks(kva[:, 512:], TKF), sel_a, HEADS_MOBA,
                        block=MOBA_BLOCK)
    cmp_out = _compress(kvb, None, cw, n_rows=t, n_batch=1)
    kc = cmp_out[:, :, 0:128].astype(BF16)
    vcT = cmp_out[:, :, 128:256].astype(BF16).transpose(0, 2, 1)
    pos = jnp.arange(t, dtype=I32).reshape(1, t)
    ocT, sel_b, _ = _cmp_attn(qbT, pos, kc, vcT, t // SLC_BLOCK, k_sel=SLC_TOPK, tq=TQ, tiles_per_batch=0)
    osT = _flash_select(qbT, kvb[:, 256:384].astype(BF16), _vT_blocks(kvb[:, 384:512], TKF), sel_b, HEADS_NSA,
                        block=SLC_BLOCK)
    owT = _flash_select(qbT, kvw[:, 0:128].astype(BF16), _vT_blocks(kvw[:, 128:256], TKF), None, HEADS_NSA,
                        block=TKF, window=WINDOW)
    return oaT.T, ocT.T, osT.T, owT.T


def _sample_mixer(proj, cache_a, cache_b, win_state, page_table, cw, bs, ts):
    qa, kva, qb, kvb, kvw, _ = proj
    n_pages = page_table.shape[1]
    past = n_pages * PAGE_SIZE
    nb = past // MOBA_BLOCK
    new_rows_T = lambda a: jnp.zeros((bs, LANES, a.shape[-1]), F32).at[:, :ts].set(a).transpose(0, 2, 1)
    km = _kmean_pages(cache_a, page_table)
    q4 = (qa * Q_SCALE).reshape(bs, ts, H_MOBA, D_HEAD).transpose(0, 2, 1, 3)
    eye = jnp.eye(H_MOBA, dtype=F32)
    qbd = (q4[:, :, :, None, :] * eye[None, :, None, :, None]).reshape(bs, H_MOBA * ts, 512)
    new_a = new_rows_T(kva.reshape(bs, ts, 1024))
    oa = _moba_sample(qbd, km, new_a, cache_a, page_table, nb=nb, ts=ts, past=past)[:, :ts]
    oa = oa.reshape(bs * ts, 512)
    cmp_out = _compress(cache_b, page_table, cw, n_rows=past, n_batch=bs)
    kc = cmp_out[:, :, 0:128].astype(BF16)
    vcT = cmp_out[:, :, 128:256].astype(BF16).transpose(0, 2, 1)
    qbs = (qb * Q_SCALE).reshape(bs, ts, 512)
    qT = jnp.zeros((bs, LANES, 512), F32).at[:, :ts].set(qbs).reshape(bs * LANES, 512).T
    pos = jnp.broadcast_to(past + jnp.minimum(jnp.arange(LANES, dtype=I32), ts - 1), (bs, LANES)).reshape(1, bs * LANES)
    n_slots = SLC_TOPK - 1
    ocT, _, idx = _cmp_attn(qT, pos, kc, vcT, past // SLC_BLOCK, k_sel=n_slots, tq=LANES, tiles_per_batch=1)
    oc = ocT.T.reshape(bs, LANES, 512)[:, :ts].reshape(bs * ts, 512)
    idx = idx.reshape(G_NSA, SLC_TOPK, bs, LANES)[:, :, :, :ts].transpose(2, 0, 3, 1)
    q5 = qbs.reshape(bs, ts, G_NSA, R_NSA, D_HEAD)
    lane_g = jnp.eye(G_NSA, dtype=F32)
    qsel = q5.transpose(0, 2, 1, 3, 4)[:, :, :, :, None, :] * lane_g[None, :, None, None, :, None]
    qs = jnp.zeros((bs, G_NSA, ts, 8, 128), F32).at[:, :, :, :R_NSA].set(qsel.reshape(bs, G_NSA, ts, R_NSA, 128))
    qs = qs.reshape(bs, G_NSA * ts, 8, 128)
    qwin = q5.transpose(0, 2, 3, 1, 4)[:, :, :, :, None, :] * lane_g[None, :, None, None, :, None]
    qw = qwin.reshape(bs, G_NSA, R_NSA * ts, 128)
    sl = idx[..., :n_slots, None]
    lane = jnp.arange(PAGE_SIZE, dtype=I32)
    in_blk = (sl >= 0) & ((lane // SLC_BLOCK) == (sl % 2))
    slot_pos = jnp.where(in_blk, (sl // 2) * PAGE_SIZE + lane, -1).reshape(bs, G_NSA, ts, n_slots * PAGE_SIZE)
    own = jnp.where(lane < ts, past + lane, -1)
    kpos = jnp.concatenate([slot_pos, jnp.broadcast_to(own, (bs, G_NSA, ts, PAGE_SIZE))], axis=-1)
    kpos = kpos.reshape(bs, G_NSA * ts, 1, (n_slots + 1) * PAGE_SIZE).astype(I32)
    new_b = new_rows_T(kvb.reshape(bs, ts, 512)[:, :, 256:])
    wnew = new_rows_T(kvw.reshape(bs, ts, 256))
    os_raw, ow_raw = _nsa_sample(idx.reshape(-1).astype(I32), qs, qw, kpos, new_b, win_state, wnew, cache_b,
                                 page_table, ts=ts, past=past, n_slots=n_slots)
    os5 = os_raw.reshape(bs, G_NSA, ts, 8, G_NSA, D_HEAD)[:, :, :, :R_NSA]
    os_ = jnp.stack([os5[:, g, :, :, g] for g in range(G_NSA)], axis=2).reshape(bs * ts, 512)
    ow5 = ow_raw.reshape(bs, G_NSA, R_NSA, ts, G_NSA, D_HEAD)
    ow = jnp.stack([ow5[:, g, :, :, g] for g in range(G_NSA)], axis=1)
    ow = ow.transpose(0, 3, 1, 2, 4).reshape(bs * ts, 512)
    return oa, oc, os_, ow


def _layer(x_p, x_s, cache_a, cache_b, win_state, page_table, c_p, c_s, w):
    (w_ada, b_ada, g_mix_pre, g_mix_post, g_ffn_pre, g_ffn_post, w_in, g_out_moba, g_out_nsa, w_out,
     cmp_pe, cmp_w1, cmp_b1, cmp_w2, cmp_b2, w_router, b_router, w_exp_up, w_exp_down, w_sh_up, w_sh_down) = w
    t, d = x_p.shape
    bs, ts, _ = x_s.shape
    assert c_p.shape[0] == 1 and t % (8 * MOBA_BLOCK) == 0 and win_state.shape[1] == WINDOW
    n_s = bs * ts
    rows = -(-(1 + bs) // 8) * 8
    c_all = jnp.zeros((rows, d), F32).at[0:1].set(c_p).at[1:1 + bs].set(c_s)
    mod = _ada(c_all, w_ada, b_ada).reshape(rows, 6, d)
    mod_p = [mod[0:1, i] for i in range(6)]
    mod_s = [jnp.repeat(mod[1:1 + bs, i], ts, axis=0) for i in range(6)]

    w_pad = jnp.zeros((d, _PROJ_CUTS[-1]), F32).at[:, :w_in.shape[1]].set(w_in).astype(BF16)
    cw = _compress_weights(cmp_pe, cmp_w1, cmp_b1, cmp_w2, cmp_b2)
    w_out_bf = w_out.astype(BF16)

    proj_p = _inproj(x_p, mod_p[1], mod_p[0], g_mix_pre, w_pad)
    proj_s = _inproj(x_s.reshape(n_s, d), mod_s[1], mod_s[0], g_mix_pre, w_pad)

    o_p = _prompt_mixer(proj_p, cw)
    win_t = win_state.transpose(0, 2, 3, 1).reshape(bs, 2 * G_NSA * D_HEAD, WINDOW)
    o_s = _sample_mixer(proj_s, cache_a, cache_b, win_t, page_table, cw, bs, ts)

    x1_p, h2_p = _post(x_p, *o_p, proj_p[5], mod_p[2], mod_p[4], mod_p[3], g_out_moba, g_out_nsa, g_mix_post,
                       g_ffn_pre, w_out_bf)
    x1_s, h2_s = _post(x_s.reshape(n_s, d), *o_s, proj_s[5], mod_s[2], mod_s[4], mod_s[3], g_out_moba, g_out_nsa,
                       g_mix_post, g_ffn_pre, w_out_bf)

    n_valid = t + n_s
    n_all = -(-n_valid // 128) * 128
    pad = lambda a: jnp.concatenate([a, jnp.zeros((n_all - n_valid, d), F32)], axis=0) if n_all > n_valid else a
    h_all = pad(jnp.concatenate([h2_p, h2_s], axis=0))
    x1_all = pad(jnp.concatenate([x1_p, x1_s], axis=0))
    g2_tab = jnp.concatenate([jnp.broadcast_to(mod_p[5], (128, d)), mod_s[5],
                              jnp.zeros((n_all - n_valid, d), F32)], axis=0)
    y_all = _moe(h_all, x1_all, g2_tab, t, n_valid, w_router, b_router, w_exp_up, w_exp_down, w_sh_up, w_sh_down,
                 g_ffn_post)
    y_p = y_all[:t]
    y_s = y_all[t:n_valid].reshape(bs, ts, d)

    _, kva_p, _, kvb_p, kvw_p, _ = proj_p
    _, kva_s, _, kvb_s, kvw_s, _ = proj_s
    state_p = (kva_p.reshape(1, t, 2 * H_MOBA, D_HEAD), kvb_p.reshape(1, t, 4 * G_NSA, D_HEAD),
               kvw_p[t - min(WINDOW, t):].reshape(1, min(WINDOW, t), 2 * G_NSA, D_HEAD))
    win_new = jnp.concatenate([win_state, kvw_s.reshape(bs, ts, 2 * G_NSA, D_HEAD)], axis=1)[:, -WINDOW:]
    state_s = (kva_s.reshape(bs, ts, 2 * H_MOBA, D_HEAD), kvb_s.reshape(bs, ts, 4 * G_NSA, D_HEAD), win_new)
    return y_p, y_s, state_p, state_s


def kernel(x_prompt, x_sample, cache_moba, cache_nsa, state_nsa_win, page_table, c_prompt, c_sample, w_ada, b_ada, g_mix_pre, g_mix_post, g_ffn_pre, g_ffn_post, w_in, g_out_moba, g_out_nsa, w_out, cmp_pe, cmp_w1, cmp_b1, cmp_w2, cmp_b2, w_router, b_router, w_exp_up, w_exp_down, w_sh_up, w_sh_down):
    weights = (w_ada, b_ada, g_mix_pre, g_mix_post, g_ffn_pre, g_ffn_post, w_in, g_out_moba, g_out_nsa, w_out,
               cmp_pe, cmp_w1, cmp_b1, cmp_w2, cmp_b2, w_router, b_router, w_exp_up, w_exp_down, w_sh_up, w_sh_down)
    depth = w_ada.shape[0]
    n_pool = cache_moba.shape[1]
    y_p, y_s = x_prompt[0], x_sample
    st_p, st_s = [], []
    for layer in range(depth):
        w_l = tuple(w[layer] for w in weights)
        cache_a = cache_moba[layer].transpose(0, 2, 3, 1).reshape(n_pool, 2 * H_MOBA * D_HEAD, PAGE_SIZE)
        cache_b = cache_nsa[layer].transpose(0, 2, 3, 1).reshape(n_pool, 4 * G_NSA * D_HEAD, PAGE_SIZE)
        y_p, y_s, sp, ss = _layer(y_p, y_s, cache_a, cache_b, state_nsa_win[layer], page_table, c_prompt, c_sample, w_l)
        st_p.append(sp)
        st_s.append(ss)
    stack = lambda sts, i: jnp.stack([s[i] for s in sts])
    return (y_p[None], y_s, stack(st_p, 0), stack(st_p, 1), stack(st_p, 2), stack(st_s, 0), stack(st_s, 1), stack(st_s, 2))
```

```python
import functools

import numpy as np
import jax
import jax.numpy as jnp
from jax import lax
from jax.experimental import pallas as pl
from jax.experimental.pallas import tpu as pltpu

F32, BF16, I32 = jnp.float32, jnp.bfloat16, jnp.int32
NEG_INF = float("-inf")
POS_INF = float("inf")
M_INIT = -1e30

D_HEAD = 64
H_MOBA = 8
H_NSA = 8
G_NSA = 2
R_NSA = H_NSA // G_NSA
MOBA_BLOCK = 256
MOBA_TOPK = 3
CMP_STRIDE = 16
CMP_LEN = 2 * CMP_STRIDE
CMP_HIDDEN = 2 * D_HEAD
SLC_BLOCK = 64
SLC_TOPK = 16
SLC_INIT = 1
SLC_LOCAL = 2
WINDOW = 512
N_EXPERTS = 256
TOPK_EXPERTS = 8
N_EXPERT_GROUPS = 8
TOPK_GROUPS = 4
ROUTED_SCALE = 2.5
PAGE_SIZE = 128
EPS = 1e-6
Q_SCALE = D_HEAD ** -0.5
LOG2E = 1.4426950408889634

_SLOPES = [2.0 ** (-8.0 * i / (H_MOBA + H_NSA)) for i in range(1, H_MOBA + H_NSA + 1)]
SLOPES_A = _SLOPES[0::2][:H_MOBA]
SLOPES_B = _SLOPES[1::2][:H_NSA]

LANES = 128
TQ = 256
TKF = 128
VMEM_LIMIT_BYTES = 56 * 1024 * 1024
ROW_BLOCK = 128
ROUTER_TILE = 256
PAGES_PER_STEP = 16


def _cparams(*sem):
    return pltpu.CompilerParams(dimension_semantics=sem, vmem_limit_bytes=VMEM_LIMIT_BYTES)


def _rms(x, g):
    return x * lax.rsqrt(jnp.mean(x * x, axis=-1, keepdims=True) + EPS) * g


def _split_bf16(a):
    hi = a.astype(BF16)
    lo = (a - hi.astype(F32)).astype(BF16)
    return hi, lo


def _dot3(a, b, dims=None):
    ah, al = _split_bf16(a)
    bh, bl = _split_bf16(b)
    if dims is None:
        d = lambda x, y: jnp.dot(x, y, preferred_element_type=F32)
    else:
        d = lambda x, y: lax.dot_general(x, y, dims, preferred_element_type=F32)
    return d(ah, bh) + (d(ah, bl) + d(al, bh))


_NT = (((1,), (1,)), ((), ()))


def _div_pow2(x, n):
    assert n & (n - 1) == 0
    return jnp.right_shift(x, n.bit_length() - 1)


def _mod_pow2(x, n):
    assert n & (n - 1) == 0
    return jnp.bitwise_and(x, n - 1)


def _head_rows(q64, half):
    z = jnp.zeros_like(q64)
    return jnp.concatenate([z, q64] if half else [q64, z], axis=0)


def _ada_kernel(c_ref, w_ref, b_ref, o_ref):
    a = jax.nn.silu(c_ref[...])
    o_ref[...] = _dot3(a, w_ref[...]) + b_ref[...]


def _ada(c, w_ada, b_ada):
    r, d = c.shape
    n = w_ada.shape[1]
    tn = 768
    return pl.pallas_call(
        _ada_kernel,
        grid=(n // tn,),
        in_specs=[pl.BlockSpec((r, d), lambda j: (0, 0)),
                  pl.BlockSpec((d, tn), lambda j: (0, j)),
                  pl.BlockSpec((1, tn), lambda j: (0, j))],
        out_specs=pl.BlockSpec((r, tn), lambda j: (0, j)),
        out_shape=jax.ShapeDtypeStruct((r, n), F32),
        compiler_params=_cparams("arbitrary"),
        name="ada",
    )(c, w_ada, b_ada.reshape(1, n))


_PROJ_CUTS = (0, 512, 1536, 2048, 2560, 2816, 2944)


def _inproj_kernel(x_ref, sc_ref, sh_ref, g_ref, w_ref, *out_refs):
    h = _rms(x_ref[...], g_ref[...]) * (1.0 + sc_ref[...]) + sh_ref[...]
    hb = h.astype(BF16)
    for o_ref, a, b in zip(out_refs, _PROJ_CUTS[:-1], _PROJ_CUTS[1:]):
        o_ref[...] = jnp.dot(hb, w_ref[:, a:b], preferred_element_type=F32)


def _inproj(x, scale, shift, g, w_pad):
    r, d = x.shape
    tm = min(512, r)
    per_row = scale.shape[0] != 1
    mod_spec = pl.BlockSpec((tm, d), lambda i: (i, 0)) if per_row else pl.BlockSpec((1, d), lambda i: (0, 0))
    widths = [b - a for a, b in zip(_PROJ_CUTS[:-1], _PROJ_CUTS[1:])]
    return pl.pallas_call(
        _inproj_kernel,
        grid=(r // tm,),
        in_specs=[pl.BlockSpec((tm, d), lambda i: (i, 0)), mod_spec, mod_spec,
                  pl.BlockSpec((1, d), lambda i: (0, 0)),
                  pl.BlockSpec(w_pad.shape, lambda i: (0, 0))],
        out_specs=[pl.BlockSpec((tm, w), lambda i: (i, 0)) for w in widths],
        out_shape=[jax.ShapeDtypeStruct((r, w), F32) for w in widths],
        compiler_params=_cparams("arbitrary"),
        name="inproj",
    )(x, scale, shift, g.reshape(1, d), w_pad)


def _topk_axis0(score, k, idx):
    n = score.shape[0]
    sel = jnp.zeros(score.shape, F32)
    picks = []
    for _ in range(k):
        m = jnp.max(score, axis=0, keepdims=True)
        first = jnp.min(jnp.where(score == m, idx, n), axis=0, keepdims=True)
        pick = idx == first
        ok = m > NEG_INF
        sel = jnp.where(pick & ok, 1.0, sel)
        picks.append(jnp.where(ok, first, -1))
        score = jnp.where(pick, NEG_INF, score)
    return sel, picks


def _topk_axis1(score, k, idx):
    n = score.shape[1]
    out = []
    for _ in range(k):
        m = jnp.max(score, axis=1, keepdims=True)
        first = jnp.min(jnp.where(score == m, idx, n), axis=1, keepdims=True)
        out.append((first, m))
        score = jnp.where(idx == first, NEG_INF, score)
    return out


def _kmean_kernel(k_ref, o_ref):
    x = k_ref[...]
    n = x.shape[0] // MOBA_BLOCK
    o_ref[...] = jnp.sum(x.reshape(n, MOBA_BLOCK, x.shape[1]), axis=1) * (1.0 / MOBA_BLOCK)


def _kmean_prompt(kva):
    t = kva.shape[0]
    nb = t // MOBA_BLOCK
    per = 8
    return pl.pallas_call(
        _kmean_kernel,
        grid=(nb // per,),
        in_specs=[pl.BlockSpec((per * MOBA_BLOCK, 512), lambda i: (i, 0))],
        out_specs=pl.BlockSpec((per, 512), lambda i: (i, 0)),
        out_shape=jax.ShapeDtypeStruct((nb, 512), F32),
        compiler_params=_cparams("arbitrary"),
        name="kmean_prompt",
    )(kva)


def _moba_gate_kernel(qT_ref, km_ref, sel_ref, *, nb):
    own = pl.program_id(0)
    blk = lax.broadcasted_iota(I32, (nb, TQ), 0)
    for h in range(H_MOBA):
        p, half = divmod(h, 2)
        qh = _head_rows(qT_ref[h * 64:(h + 1) * 64, :], half)
        g = _dot3(km_ref[:, p * 128:(p + 1) * 128], qh)
        g = jnp.where(blk < own, g, NEG_INF)
        sel, _ = _topk_axis0(g, MOBA_TOPK, blk)
        sel_ref[h] = jnp.where(blk == own, 1.0, sel)


def _moba_gate_prompt(qT, kmean):
    t = qT.shape[1]
    nb = kmean.shape[0]
    return pl.pallas_call(
        functools.partial(_moba_gate_kernel, nb=nb),
        grid=(t // TQ,),
        in_specs=[pl.BlockSpec((512, TQ), lambda i: (0, i)),
                  pl.BlockSpec((nb, 512), lambda i: (0, 0))],
        out_specs=pl.BlockSpec((H_MOBA, nb, TQ), lambda i: (0, 0, i)),
        out_shape=jax.ShapeDtypeStruct((H_MOBA, nb, t), F32),
        compiler_params=_cparams("arbitrary"),
        name="moba_gate",
    )(qT, kmean)


def _flash_kernel(qT_ref, k_ref, vT_ref, *rest, heads, sub, sel_div, window):
    nh = len(heads)
    if window is None:
        sel_ref, o_ref, qh_scr, sc_scr, *state = rest
    else:
        sel_ref, (o_ref, qh_scr, sc_scr, *state) = None, rest
    m_scr, l_scr, acc_scr = state[:nh], state[nh:2 * nh], state[2 * nh:]
    i = pl.program_id(0)
    kio = lax.broadcasted_iota(I32, (TKF, TQ), 0)
    qio = lax.broadcasted_iota(I32, (TKF, TQ), 1)
    kiof = kio.astype(F32)
    rows = TKF // sub
    for h, (pair, half, set_idx, slope) in enumerate(heads):
        slope = slope * LOG2E
        qh_scr[h] = _head_rows(qT_ref[h * 64:(h + 1) * 64, :] * LOG2E, half).astype(BF16)
        sc_scr[h] = kiof * slope
        m_scr[h][...] = jnp.full((1, TQ), M_INIT, F32)
        l_scr[h][...] = jnp.zeros((1, TQ), F32)
        acc_scr[h][...] = jnp.zeros((64, TQ), F32)

    def sel_rows(set_idx, j):
        if sub == 1:
            return sel_ref[set_idx, pl.ds(j // sel_div, 1), :] > 0.0
        per8 = 8 // sub
        blk8 = sel_ref[set_idx, pl.ds(pl.multiple_of((j // per8) * 8, 8), 8), :]
        s = blk8[0:sub]
        for u in range(1, per8):
            s = jnp.where(j % per8 == u, blk8[u * sub:(u + 1) * sub], s)
        return s > 0.0

    def step(j, diag=None, band=None):
        off_base = (j * TKF - i * TQ).astype(F32)
        jc = j if band is None else jnp.maximum(j, 0)
        row0 = pl.multiple_of(jc * TKF, TKF)
        if band is not None:
            dist = qio - kio - band * TKF
            in_band = (dist >= 0) & (dist <= window) & (j >= 0)
        sel_cache = {}
        for h, (pair, half, set_idx, slope) in enumerate(heads):
            kj = k_ref[pl.ds(row0, TKF), pair * 128:(pair + 1) * 128]
            t = jnp.dot(kj, qh_scr[h], preferred_element_type=F32) + sc_scr[h]
            if diag is not None:
                t = jnp.where(kio + diag * TKF <= qio, t, NEG_INF)
            if band is not None:
                t = jnp.where(in_band, t, NEG_INF)
            off = (slope * LOG2E) * off_base
            t3 = t.reshape(sub, rows, TQ)
            m = m_scr[h][...]
            if sel_ref is None:
                m_new = jnp.maximum(m, jnp.max(jnp.max(t3, axis=1), axis=0, keepdims=True) + off)
                mu = jnp.broadcast_to(m_new - off, (sub, TQ))
            else:
                if set_idx not in sel_cache:
                    sel_cache[set_idx] = sel_rows(set_idx, j)
                selj = sel_cache[set_idx]
                mb = jnp.where(selj, jnp.max(t3, axis=1) + off, M_INIT)
                m_new = jnp.maximum(m, jnp.max(mb, axis=0, keepdims=True))
                mu = jnp.where(selj, m_new - off, POS_INF)
            p = jnp.exp2(t3 - mu[:, None, :]).reshape(TKF, TQ)
            alpha = jnp.exp2(m - m_new)
            l_scr[h][...] = l_scr[h][...] * alpha + jnp.sum(p, axis=0, keepdims=True)
            pv = jnp.dot(vT_ref[pair, jc], p.astype(BF16), preferred_element_type=F32)
            acc_scr[h][...] = acc_scr[h][...] * alpha + pv[half * 64:(half + 1) * 64]
            m_scr[h][...] = m_new

    per_q = TQ // TKF
    if window is None:
        def body(jj, c):
            for u in range(per_q):
                step(jj * per_q + u)
            return c

        lax.fori_loop(0, i, body, 0)
        for u in range(per_q):
            step(i * per_q + u, diag=u)
    else:
        for dj in range(-(window // TKF), per_q):
            step(i * per_q + dj, band=dj)
    for h in range(len(heads)):
        o_ref[h * 64:(h + 1) * 64, :] = acc_scr[h][...] / jnp.maximum(l_scr[h][...], 1e-30)


def _flash_select(qT, k, vTb, sel, heads, *, block, window=None):
    t = qT.shape[1]
    nh = len(heads)
    sub, sel_div = max(TKF // block, 1), max(block // TKF, 1)
    resident = lambda shape: pl.BlockSpec(shape, lambda i: (0,) * len(shape), pipeline_mode=pl.Buffered(1))
    in_specs = [pl.BlockSpec((nh * 64, TQ), lambda i: (0, i)), resident(k.shape), resident(vTb.shape)]
    args = (qT, k, vTb)
    if window is None:
        in_specs.append(pl.BlockSpec((sel.shape[0], sel.shape[1], TQ), lambda i: (0, 0, i)))
        args += (sel,)
    return pl.pallas_call(
        functools.partial(_flash_kernel, heads=heads, sub=sub, sel_div=sel_div, window=window),
        grid=(t // TQ,),
        in_specs=in_specs,
        out_specs=pl.BlockSpec((nh * 64, TQ), lambda i: (0, i)),
        out_shape=jax.ShapeDtypeStruct(qT.shape, F32),
        scratch_shapes=([pltpu.VMEM((nh, 128, TQ), BF16), pltpu.VMEM((nh, TKF, TQ), F32)]
                        + [pltpu.VMEM((1, TQ), F32)] * (2 * nh) + [pltpu.VMEM((64, TQ), F32)] * nh),
        compiler_params=_cparams("arbitrary"),
        name="flash_select" if window is None else "flash_window",
    )(*args)


HEADS_MOBA = tuple((h // 2, h % 2, h, SLOPES_A[h]) for h in range(H_MOBA))
HEADS_NSA = tuple((0, h // R_NSA, h // R_NSA, SLOPES_B[h]) for h in range(H_NSA))


def _compress_kernel(*refs, paged, n_rows):
    if paged:
        pt_ref, x_hbm, pe_ref, w1_ref, b1_ref, wcat_ref, w2_ref, b2_ref, o_ref, xs, sem, stage = refs
    else:
        x_hbm, pe_ref, w1_ref, b1_ref, wcat_ref, w2_ref, b2_ref, o_ref, xs, sem = refs
    nc = n_rows // CMP_STRIDE
    if paged:
        b = pl.program_id(0)
        n_pages = n_rows // PAGE_SIZE
        ch = stage.shape[1]

        def page_copy(c, u):
            return pltpu.make_async_copy(x_hbm.at[pt_ref[b, c * ch + u], pl.ds(0, 256), :],
                                         stage.at[c % 2, u], sem.at[c % 2])

        def start_chunk(c):
            def body(u, carry):
                page_copy(c, u).start()
                return carry
            lax.fori_loop(0, ch, body, 0)

        def finish_chunk(c):
            def wait_body(u, carry):
                page_copy(c, u).wait()
                return carry
            lax.fori_loop(0, ch, wait_body, 0)

            def body(u, carry):
                row0 = pl.multiple_of((c * ch + u) * PAGE_SIZE, PAGE_SIZE)
                for c2 in range(2):
                    xs[c2, pl.ds(row0, PAGE_SIZE), :] = stage[c % 2, u, c2 * LANES:(c2 + 1) * LANES, :].T
                return carry
            lax.fori_loop(0, ch, body, 0, unroll=4)

        start_chunk(0)
        xs[:, pl.ds(n_rows, CMP_STRIDE), :] = jnp.zeros((2, CMP_STRIDE, LANES), F32)
        for c in range(n_pages // ch):
            if c + 1 < n_pages // ch:
                start_chunk(c + 1)
            finish_chunk(c)
    else:
        cps = [pltpu.make_async_copy(x_hbm.at[:, pl.ds(c2 * LANES, LANES)], xs.at[c2, pl.ds(0, n_rows), :], sem.at[0])
               for c2 in range(2)]
        for cp in cps:
            cp.start()
        xs[:, pl.ds(n_rows, CMP_STRIDE), :] = jnp.zeros((2, CMP_STRIDE, LANES), F32)
        for cp in cps:
            cp.wait()

    def rows(r, c0, n):
        return jnp.concatenate([xs[c2, pl.ds(r + CMP_STRIDE * c0, n, stride=CMP_STRIDE), :] for c2 in range(2)],
                               axis=1).astype(BF16)

    consts = []
    for j in range(2):
        cj = _dot3(pe_ref[j], w1_ref[j])[0:1] + b1_ref[j]
        consts += [cj, cj]
    const = jnp.concatenate(consts, axis=1)
    rc = min(256, nc)
    for c0 in range(0, nc, rc):
        xcat = jnp.concatenate([rows(r, c0, rc) for r in range(CMP_LEN)], axis=1)
        hid = jnp.dot(xcat, wcat_ref[...], preferred_element_type=F32) + const
        act = jax.nn.gelu(hid)
        out = jnp.dot(act.astype(BF16), w2_ref[...], preferred_element_type=F32) + b2_ref[...]
        row = lax.broadcasted_iota(I32, out.shape, 0) + c0
        o_ref[0, c0:c0 + rc, :] = jnp.where(row < nc - 1, out, 0.0)


def _compress(x, page_table, cw, *, n_rows, n_batch):
    paged = page_table is not None
    nc = n_rows // CMP_STRIDE
    const = lambda shape: pl.BlockSpec(shape, lambda *_: (0,) * len(shape), pipeline_mode=pl.Buffered(1))
    in_specs = [pl.BlockSpec(memory_space=pl.ANY), const((2, 8, 2048)), const((2, 2048, 128)), const((2, 1, 128)),
                const((CMP_LEN * 256, 512)), const((512, 256)), const((1, 256))]
    scratch = [pltpu.VMEM((2, n_rows + CMP_STRIDE, LANES), F32), pltpu.SemaphoreType.DMA((2,))]
    if paged:
        chunk = min(32, n_rows // PAGE_SIZE)
        scratch.append(pltpu.VMEM((2, chunk, 256, PAGE_SIZE), F32))
    grid_spec = pltpu.PrefetchScalarGridSpec(
        num_scalar_prefetch=1 if paged else 0,
        grid=(n_batch,),
        in_specs=in_specs,
        out_specs=pl.BlockSpec((1, nc, 256), lambda b, *_: (b, 0, 0)),
        scratch_shapes=scratch)
    args = ((page_table,) if paged else ()) + (x,) + cw
    return pl.pallas_call(
        functools.partial(_compress_kernel, paged=paged, n_rows=n_rows),
        grid_spec=grid_spec,
        out_shape=jax.ShapeDtypeStruct((n_batch, nc, 256), F32),
        compiler_params=_cparams("arbitrary"),
        name="compress_paged" if paged else "compress",
    )(*args)


def _compress_weights(cmp_pe, cmp_w1, cmp_b1, cmp_w2, cmp_b2):
    half = CMP_STRIDE * D_HEAD
    wa = jnp.zeros((16, 256, 512), F32)
    wb = jnp.zeros((16, 256, 512), F32)
    w2 = jnp.zeros((512, 256), F32)
    for jg in range(4):
        j = jg // 2
        wa = wa.at[:, jg * 64:(jg + 1) * 64, jg * 128:(jg + 1) * 128].set(cmp_w1[j, :half].reshape(16, 64, 128))
        wb = wb.at[:, jg * 64:(jg + 1) * 64, jg * 128:(jg + 1) * 128].set(cmp_w1[j, half:].reshape(16, 64, 128))
        w2 = w2.at[jg * 128:(jg + 1) * 128, jg * 64:(jg + 1) * 64].set(cmp_w2[j])
    pe = jnp.zeros((2, 8, 2048), F32).at[:, 0].set(cmp_pe.reshape(2, 2048))
    b2 = jnp.concatenate([cmp_b2[0], cmp_b2[0], cmp_b2[1], cmp_b2[1]]).reshape(1, 256)
    wcat = jnp.concatenate([wa.reshape(16 * 256, 512), wb.reshape(16 * 256, 512)], axis=0)
    return (pe, cmp_w1, cmp_b1.reshape(2, 1, 128), wcat.astype(BF16), w2.astype(BF16), b2)


def _overlap_matrix(ns, nc_pad):
    ratio = SLC_BLOCK // CMP_STRIDE
    lr = CMP_LEN // CMP_STRIDE
    w = np.zeros((ns, nc_pad), np.float32)
    for s in range(ns):
        for m in range(ratio):
            for n in range(lr):
                c = ratio * s + m - n
                if 0 <= c < nc_pad - 1:
                    w[s, c] += 1.0
    return w


def _cmp_attn_kernel(qT_ref, pos_ref, kc_ref, vcT_ref, wT_ref, oc_ref, sel_ref, idx_ref, *, k_sel, tq, causal_tiles):
    nc_pad = kc_ref.shape[1]
    ns = wT_ref.shape[0]
    ratio = SLC_BLOCK // CMP_STRIDE
    pos = pos_ref[...]

    def run(n):
        nsu = n // ratio
        cidx = lax.broadcasted_iota(I32, (n, tq), 0)
        cpos = cidx * CMP_STRIDE + (CMP_LEN - 1)
        valid = (cpos <= pos) & (cidx < nc_pad - 1)
        ndist = (cpos - pos).astype(F32)
        blk = lax.broadcasted_iota(I32, (nsu, tq), 0)
        d = jnp.right_shift(pos, 6) - blk
        forced = (blk < SLC_INIT) | ((d >= 0) & (d < SLC_LOCAL))
        kc = kc_ref[0, 0:n, :]
        vcT = vcT_ref[0, :, 0:n]
        wT = wT_ref[0:nsu, 0:n]
        for g in range(G_NSA):
            imp = jnp.zeros((n, tq), F32)
            for r in range(R_NSA):
                h = g * R_NSA + r
                qh = _head_rows(qT_ref[h * 64:(h + 1) * 64, :], g).astype(BF16)
                s = jnp.dot(kc, qh, preferred_element_type=F32) + ndist * SLOPES_B[h]
                s = jnp.where(valid, s, NEG_INF)
                m = jnp.max(s, axis=0, keepdims=True)
                m = jnp.where(m > NEG_INF, m, 0.0)
                p = jnp.exp(s - m)
                p = p * (1.0 / jnp.maximum(jnp.sum(p, axis=0, keepdims=True), 1e-30))
                imp = imp + p
                oc = jnp.dot(vcT, p.astype(BF16), preferred_element_type=F32)
                oc_ref[h * 64:(h + 1) * 64, :] = oc[g * 64:(g + 1) * 64]
            hi, lo = _split_bf16(imp)
            islc = jnp.dot(wT, hi, preferred_element_type=F32) + jnp.dot(wT, lo, preferred_element_type=F32)
            score = jnp.where(forced, POS_INF, jnp.where(d >= 0, islc, NEG_INF))
            sel, picks = _topk_axis0(score, k_sel, blk)
            sel_ref[g, 0:nsu, :] = sel
            if nsu < ns:
                sel_ref[g, nsu:ns, :] = jnp.zeros((ns - nsu, tq), F32)
            for r, pk in enumerate(picks):
                idx_ref[g, r:r + 1, :] = pk
            for r in range(len(picks), SLC_TOPK):
                idx_ref[g, r:r + 1, :] = jnp.full((1, tq), -1, I32)

    if causal_tiles and nc_pad % (4 * LANES) == 0:
        quarter = nc_pad // 4
        tiles_per_quarter = quarter * CMP_STRIDE // tq
        v_sel = pl.program_id(0) // tiles_per_quarter
        for v in range(4):
            pl.when(v_sel == v)(functools.partial(run, (v + 1) * quarter))
    else:
        run(nc_pad)


def _cmp_attn(qT, pos, kc, vcT, ns, *, k_sel, tq, tiles_per_batch):
    n = qT.shape[1]
    nc_pad = kc.shape[1]
    wT = jnp.asarray(_overlap_matrix(ns, nc_pad), BF16)
    bmap = (lambda i: (i // tiles_per_batch, 0, 0)) if tiles_per_batch else (lambda i: (0, 0, 0))
    return pl.pallas_call(
        functools.partial(_cmp_attn_kernel, k_sel=k_sel, tq=tq, causal_tiles=(tiles_per_batch == 0)),
        grid=(n // tq,),
        in_specs=[pl.BlockSpec((512, tq), lambda i: (0, i)),
                  pl.BlockSpec((1, tq), lambda i: (0, i)),
                  pl.BlockSpec((1, nc_pad, 128), bmap),
                  pl.BlockSpec((1, 128, nc_pad), bmap),
                  pl.BlockSpec((ns, nc_pad), lambda i: (0, 0))],
        out_specs=[pl.BlockSpec((512, tq), lambda i: (0, i)),
                   pl.BlockSpec((G_NSA, ns, tq), lambda i: (0, 0, i)),
                   pl.BlockSpec((G_NSA, SLC_TOPK, tq), lambda i: (0, 0, i))],
        out_shape=[jax.ShapeDtypeStruct((512, n), F32),
                   jax.ShapeDtypeStruct((G_NSA, ns, n), F32),
                   jax.ShapeDtypeStruct((G_NSA, SLC_TOPK, n), I32)],
        compiler_params=_cparams("arbitrary"),
        name="cmp_attn",
    )(qT, pos, kc, vcT, wT)


def _post_kernel(x_ref, oa_ref, oc_ref, os_ref, ow_ref, gb_ref, g1_ref, sc2_ref, sh2_ref,
                 goa_ref, gob_ref, gpost_ref, gffn_ref, e3_ref, wout_ref, x1_ref, h2_ref):
    gates = jax.nn.sigmoid(gb_ref[...])
    ob = (_dot3(gates, e3_ref[0]) * oc_ref[...] + _dot3(gates, e3_ref[1]) * os_ref[...]
          + _dot3(gates, e3_ref[2]) * ow_ref[...])
    oa_n = _rms(oa_ref[...], goa_ref[...]).astype(BF16)
    ob_n = _rms(ob, gob_ref[...]).astype(BF16)
    o = (jnp.dot(oa_n, wout_ref[0:512, :], preferred_element_type=F32)
         + jnp.dot(ob_n, wout_ref[512:1024, :], preferred_element_type=F32))
    x1 = x_ref[...] + g1_ref[...] * _rms(o, gpost_ref[...])
    x1_ref[...] = x1
    h2_ref[...] = _rms(x1, gffn_ref[...]) * (1.0 + sc2_ref[...]) + sh2_ref[...]


def _gate_expand():
    e = np.zeros((3, 128, 512), np.float32)
    for h in range(H_NSA):
        for j in range(3):
            e[j, 3 * h + j, h * 64:(h + 1) * 64] = 1.0
    return jnp.asarray(e)


def _post(x, oa, oc, os_, ow, gb, gate1, scale2, shift2, g_oa, g_ob, g_post, g_ffn, w_out_bf):
    r, d = x.shape
    tm = min(256, r)
    per_row = gate1.shape[0] != 1
    mod = pl.BlockSpec((tm, d), lambda i: (i, 0)) if per_row else pl.BlockSpec((1, d), lambda i: (0, 0))
    row = lambda w: pl.BlockSpec((tm, w), lambda i: (i, 0))
    const = lambda shape: pl.BlockSpec(shape, lambda i: (0,) * len(shape))
    return pl.pallas_call(
        _post_kernel,
        grid=(r // tm,),
        in_specs=[row(d), row(512), row(512), row(512), row(512), row(128), mod, mod, mod,
                  const((1, 512)), const((1, 512)), const((1, d)), const((1, d)),
                  const((3, 128, 512)), const((1024, d))],
        out_specs=[row(d), row(d)],
        out_shape=[jax.ShapeDtypeStruct((r, d), F32), jax.ShapeDtypeStruct((r, d), F32)],
        compiler_params=_cparams("arbitrary"),
        name="post_attn",
    )(x, oa, oc, os_, ow, gb, gate1, scale2, shift2, g_oa.reshape(1, 512), g_ob.reshape(1, 512),
      g_post.reshape(1, d), g_ffn.reshape(1, d), _gate_expand(), w_out_bf)


def _router_kernel(h_ref, wr_ref, br_ref, tri_ref, idx_ref, wt_ref, rank_ref, cnt_ref, run_ref, *, n_valid, tm):
    i = pl.program_id(0)

    @pl.when(i == 0)
    def _():
        run_ref[...] = jnp.zeros_like(run_ref)

    s = jax.nn.sigmoid(_dot3(h_ref[...], wr_ref[...]))
    sel = s + br_ref[...]
    lane = lax.broadcasted_iota(I32, (tm, N_EXPERTS), 1)
    per = N_EXPERTS // N_EXPERT_GROUPS
    grp = jnp.right_shift(lane, 5)
    lane_g = lax.broadcasted_iota(I32, (tm, LANES), 1)
    gscore = jnp.full((tm, LANES), NEG_INF, F32)
    for g in range(N_EXPERT_GROUPS):
        v = jnp.where(grp == g, sel, NEG_INF)
        (i1, m1), (_, m2) = _topk_axis1(v, 2, lane)
        gscore = jnp.where(lane_g == g, m1 + m2, gscore)
    emask = jnp.zeros((tm, N_EXPERTS), jnp.bool_)
    for first, _ in _topk_axis1(gscore, TOPK_GROUPS, lane_g):
        emask = emask | (grp == first)
    picks = _topk_axis1(jnp.where(emask, sel, NEG_INF), TOPK_EXPERTS, lane)
    row = lax.broadcasted_iota(I32, (tm, 1), 0) + i * tm
    row_ok = row < n_valid
    onehot = jnp.zeros((tm, N_EXPERTS), F32)
    ws = []
    for first, _ in picks:
        pick = lane == first
        ws.append(jnp.sum(jnp.where(pick, s, 0.0), axis=1, keepdims=True))
        onehot = jnp.where(pick & row_ok, 1.0, onehot)
    wsum = ws[0]
    for w in ws[1:]:
        wsum = wsum + w
    excl = jnp.dot(tri_ref[...], onehot.astype(BF16), preferred_element_type=F32) + run_ref[...]
    idx_out = jnp.zeros((tm, LANES), I32)
    wt_out = jnp.zeros((tm, LANES), F32)
    rank_out = jnp.zeros((tm, LANES), I32)
    for k, (first, _) in enumerate(picks):
        rk = jnp.sum(jnp.where(lane == first, excl, 0.0), axis=1, keepdims=True)
        idx_out = jnp.where(lane_g == k, first, idx_out)
        wt_out = jnp.where(lane_g == k, ws[k] / wsum * ROUTED_SCALE, wt_out)
        rank_out = jnp.where(lane_g == k, rk.astype(I32), rank_out)
    idx_ref[...] = idx_out
    wt_ref[...] = wt_out
    rank_ref[...] = rank_out
    run_ref[...] = run_ref[...] + jnp.sum(onehot, axis=0, keepdims=True)
    cnt_ref[...] = run_ref[...]


def _router(h_all, w_router, b_router, n_valid):
    n, d = h_all.shape
    tm = ROUTER_TILE
    tri = jnp.asarray(np.tril(np.ones((tm, tm), np.float32), -1), BF16)
    const = lambda shape: pl.BlockSpec(shape, lambda i: (0,) * len(shape))
    row = lambda w: pl.BlockSpec((tm, w), lambda i: (i, 0))
    return pl.pallas_call(
        functools.partial(_router_kernel, n_valid=n_valid, tm=tm),
        grid=(n // tm,),
        in_specs=[row(d), const((d, N_EXPERTS)), const((1, N_EXPERTS)), const((tm, tm))],
        out_specs=[row(LANES), row(LANES), row(LANES), const((1, N_EXPERTS))],
        out_shape=[jax.ShapeDtypeStruct((n, LANES), I32), jax.ShapeDtypeStruct((n, LANES), F32),
                   jax.ShapeDtypeStruct((n, LANES), I32), jax.ShapeDtypeStruct((1, N_EXPERTS), F32)],
        scratch_shapes=[pltpu.VMEM((1, N_EXPERTS), F32)],
        compiler_params=_cparams("arbitrary"),
        name="router",
    )(h_all, w_router, b_router.reshape(1, N_EXPERTS), tri)


def _gather_rows(idx_hbm, idx_smem, isem, src_hbm, buf, sem, step, n_steps, n_rows):
    def idx_copy(k, slot):
        return pltpu.make_async_copy(idx_hbm.at[k], idx_smem.at[slot], isem.at[slot])

    def issue(slot):
        def body(r2, c):
            for prio in range(2):
                r = 2 * r2 + prio
                pltpu.make_async_copy(src_hbm.at[idx_smem[slot, r]], buf.at[slot, r], sem.at[slot]).start(priority=prio)
            return c
        lax.fori_loop(0, n_rows // 2, body, 0, unroll=4)

    @pl.when(step == 0)
    def _():
        idx_copy(0, 0).start()
        idx_copy(0, 0).wait()
        issue(0)

        @pl.when(1 < n_steps)
        def _():
            idx_copy(1, 1).start()

    nxt = (step + 1) % 2

    @pl.when(step + 1 < n_steps)
    def _():
        idx_copy(step + 1, nxt).wait()
        issue(nxt)

    @pl.when(step + 2 < n_steps)
    def _():
        idx_copy(step + 2, step % 2).start()

    slot = step % 2

    @pl.when(step < n_steps)
    def _():
        pltpu.make_async_copy(src_hbm.at[pl.ds(0, n_rows)], buf.at[slot], sem.at[slot]).wait()
    return slot


def _expert_kernel(blk_e_ref, nused_ref, sched_ref, tok_hbm, h_hbm, wup_hbm, wdn_hbm, o_ref, buf0, buf1, idx_smem,
                   sem, isem, wup_bf, wdn_bf, wup_f32, wdn_f32, wsem):
    i = pl.program_id(0)
    nused = nused_ref[0]
    rb = ROW_BLOCK
    bufs = (buf0, buf1)

    def w_copies(e, slot):
        return (pltpu.make_async_copy(wup_hbm.at[e], wup_f32.at[slot], wsem.at[slot]),
                pltpu.make_async_copy(wdn_hbm.at[e], wdn_f32.at[slot], wsem.at[slot]))

    def idx_copy(k, slot):
        return pltpu.make_async_copy(tok_hbm.at[k], idx_smem.at[slot], isem.at[slot])

    def issue_rows(slot):
        for r in range(rb):
            pltpu.make_async_copy(h_hbm.at[idx_smem[slot, r]], bufs[slot].at[r], sem.at[slot]).start()

    def wait_rows(slot):
        pltpu.make_async_copy(h_hbm.at[pl.ds(0, rb)], bufs[slot], sem.at[slot]).wait()

    def compute(slot):
        buf = bufs[slot]
        x = jnp.concatenate([buf[:, s, :] for s in range(8)], axis=1).astype(BF16)
        up = jnp.dot(x, wup_bf[...], preferred_element_type=F32)
        half = up.shape[1] // 2
        act = (jax.nn.silu(up[:, :half]) * up[:, half:]).astype(BF16)
        y = jnp.dot(act, wdn_bf[...], preferred_element_type=F32)
        for s in range(8):
            o_ref[:, s, :] = y[:, s * 128:(s + 1) * 128]

    @pl.when(i == 0)
    def _():
        for cp in w_copies(blk_e_ref[0], 0):
            cp.start()
        idx_copy(0, 0).start()
        idx_copy(0, 0).wait()
        issue_rows(0)

        @pl.when(1 < nused)
        def _():
            idx_copy(1, 1).start()

    @pl.when(i + 2 < nused)
    def _():
        idx_copy(i + 2, i % 2).start()

    first = (sched_ref[0, i] == 1) & (i < nused)
    for wslot in range(2):
        @pl.when(first & (sched_ref[1, i] == wslot))
        def _():
            for cp in w_copies(blk_e_ref[i], wslot):
                cp.wait()

            @pl.when(sched_ref[3, i] == 1)
            def _():
                for cp in w_copies(sched_ref[2, i], 1 - wslot):
                    cp.start()

            wup_bf[...] = wup_f32[wslot].astype(BF16)
            wdn_bf[...] = wdn_f32[wslot].astype(BF16)

    has_next = i + 1 < nused
    for slot in range(2):
        mine = (i % 2) == slot

        @pl.when(mine & has_next)
        def _():
            idx_copy(i + 1, 1 - slot).wait()
            issue_rows(1 - slot)
            wait_rows(slot)
            compute(slot)

        @pl.when(mine & (i < nused) & jnp.logical_not(has_next))
        def _():
            wait_rows(slot)
            compute(slot)

    @pl.when(i >= nused)
    def _():
        o_ref[...] = jnp.zeros_like(o_ref)


def _experts(h3, blk_e, row_tok, nused, w_exp_up, w_exp_down):
    nblk = blk_e.shape[0]
    rb = ROW_BLOCK
    e, d, f2 = w_exp_up.shape
    blk = jnp.arange(nblk, dtype=I32)
    first = (blk == 0) | (blk_e != jnp.roll(blk_e, 1))
    parity = (jnp.cumsum(first.astype(I32)) - 1) % 2
    nxt_pos = jnp.sum((blk_e[None, :] <= blk_e[:, None]).astype(I32), axis=1)
    nxt_e = blk_e[jnp.minimum(nxt_pos, nblk - 1)]
    sched = jnp.stack([first.astype(I32), parity, nxt_e, (nxt_pos < nused[0]).astype(I32)]).astype(I32)
    any_spec = pl.BlockSpec(memory_space=pl.ANY)
    grid_spec = pltpu.PrefetchScalarGridSpec(
        num_scalar_prefetch=3,
        grid=(nblk,),
        in_specs=[any_spec, any_spec, any_spec, any_spec],
        out_specs=pl.BlockSpec((rb, 8, 128), lambda i, be, nu, sc: (i, 0, 0)),
        scratch_shapes=[pltpu.VMEM((rb, 8, 128), F32), pltpu.VMEM((rb, 8, 128), F32), pltpu.SMEM((2, rb), I32),
                        pltpu.SemaphoreType.DMA((2,)), pltpu.SemaphoreType.DMA((2,)),
                        pltpu.VMEM((d, f2), BF16), pltpu.VMEM((f2 // 2, d), BF16),
                        pltpu.VMEM((2, d, f2), F32), pltpu.VMEM((2, f2 // 2, d), F32),
                        pltpu.SemaphoreType.DMA((2,))])
    return pl.pallas_call(
        _expert_kernel,
        grid_spec=grid_spec,
        out_shape=jax.ShapeDtypeStruct((nblk * rb, 8, 128), F32),
        compiler_params=_cparams("arbitrary"),
        name="experts",
    )(blk_e, nused, sched, row_tok.reshape(nblk, rb), h3, w_exp_up, w_exp_down)


def _combine_kernel(dest_hbm, y_hbm, wt_ref, h_ref, x1_ref, g2_ref, gpost_ref, wsu_ref, wsd_ref, o_ref,
                    buf, idx_smem, sem, isem, *, tm):
    i = pl.program_id(0)
    slot = _gather_rows(dest_hbm, idx_smem, isem, y_hbm, buf, sem, i, pl.num_programs(0), tm * TOPK_EXPERTS)
    wt = wt_ref[...]
    f = jnp.zeros((tm, h_ref.shape[1]), F32)
    for k in range(TOPK_EXPERTS):
        yk = jnp.concatenate([buf[slot, pl.ds(k, tm, stride=8), s, :] for s in range(8)], axis=1)
        f = f + yk * wt[:, k:k + 1]
    hb = h_ref[...].astype(BF16)
    up = jnp.dot(hb, wsu_ref[...], preferred_element_type=F32)
    half = up.shape[1] // 2
    act = (jax.nn.silu(up[:, :half]) * up[:, half:]).astype(BF16)
    f = f + jnp.dot(act, wsd_ref[...], preferred_element_type=F32)
    o_ref[...] = x1_ref[...] + g2_ref[...] * _rms(f, gpost_ref[...])


def _combine(dest, y, wt, h_all, x1_all, gate2_tab, n_prompt, g_ffn_post, w_sh_up_bf, w_sh_down_bf):
    n, d = h_all.shape
    tm = 128
    assert n_prompt % tm == 0
    const = lambda shape: pl.BlockSpec(shape, lambda i: (0,) * len(shape))
    row = lambda w: pl.BlockSpec((tm, w), lambda i: (i, 0))
    gate_spec = pl.BlockSpec((tm, d), lambda i: (jnp.maximum(i - (n_prompt // tm - 1), 0), 0))
    na = tm * TOPK_EXPERTS
    return pl.pallas_call(
        functools.partial(_combine_kernel, tm=tm),
        grid=(n // tm,),
        in_specs=[pl.BlockSpec(memory_space=pl.ANY), pl.BlockSpec(memory_space=pl.ANY), row(LANES), row(d), row(d),
                  gate_spec, const((1, d)), const(w_sh_up_bf.shape), const(w_sh_down_bf.shape)],
        out_specs=row(d),
        scratch_shapes=[pltpu.VMEM((2, na, 8, 128), F32), pltpu.SMEM((2, na), I32),
                        pltpu.SemaphoreType.DMA((2,)), pltpu.SemaphoreType.DMA((2,))],
        out_shape=jax.ShapeDtypeStruct((n, d), F32),
        compiler_params=_cparams("arbitrary"),
        name="moe_combine",
    )(dest.reshape(n // tm, na), y, wt, h_all, x1_all, gate2_tab, g_ffn_post.reshape(1, d),
      w_sh_up_bf, w_sh_down_bf)


def _dest_kernel(idx_ref, rank_ref, ps_ref, o_ref):
    tm = idx_ref.shape[0]
    idxf = idx_ref[...].astype(F32)
    ps = ps_ref[...].astype(F32)
    lane_g = lax.broadcasted_iota(I32, (tm, LANES), 1)
    lane = lax.broadcasted_iota(I32, (tm, N_EXPERTS), 1).astype(F32)
    base = jnp.zeros((tm, LANES), F32)
    for k in range(TOPK_EXPERTS):
        idx_k = jnp.sum(jnp.where(lane_g == k, idxf, 0.0), axis=1, keepdims=True)
        ps_k = jnp.sum(jnp.where(lane == idx_k, ps, 0.0), axis=1, keepdims=True)
        base = jnp.where(lane_g == k, ps_k, base)
    o_ref[...] = base.astype(I32) + rank_ref[...]


def _moe_dest(idx_p, rank_p, p_start):
    n = idx_p.shape[0]
    tm = 128
    row = pl.BlockSpec((tm, LANES), lambda i: (i, 0))
    return pl.pallas_call(
        _dest_kernel,
        grid=(n // tm,),
        in_specs=[row, row, pl.BlockSpec((1, N_EXPERTS), lambda i: (0, 0))],
        out_specs=row,
        out_shape=jax.ShapeDtypeStruct((n, LANES), I32),
        compiler_params=_cparams("arbitrary"),
        name="moe_dest",
    )(idx_p, rank_p, p_start.reshape(1, N_EXPERTS))


def _moe(h_all, x1_all, gate2_tab, n_prompt, n_valid, w_router, b_router, w_exp_up, w_exp_down, w_sh_up, w_sh_down,
         g_ffn_post):
    n, d = h_all.shape
    rb = ROW_BLOCK
    e = N_EXPERTS
    idx_p, wt_p, rank_p, cnt = _router(h_all, w_router, b_router, n_valid)
    counts = cnt[0].astype(I32)
    padded = (counts + rb - 1) // rb * rb
    p_end = jnp.cumsum(padded)
    p_start = p_end - padded
    dest = _moe_dest(idx_p, rank_p, p_start)[:, :TOPK_EXPERTS]
    nblk = -(-(n_valid * TOPK_EXPERTS + e * (rb - 1)) // rb)
    tok = jnp.repeat(jnp.arange(n_valid, dtype=I32), TOPK_EXPERTS)
    row_tok = jnp.zeros((nblk * rb,), I32).at[dest[:n_valid].reshape(-1)].set(tok)
    blk_start = jnp.arange(nblk, dtype=I32) * rb
    blk_e = jnp.minimum(jnp.sum((p_end[None, :] <= blk_start[:, None]).astype(I32), axis=1), e - 1)
    nused = (p_end[-1] // rb).astype(I32).reshape(1)
    y = _experts(h_all.reshape(n, 8, 128), blk_e, row_tok, nused, w_exp_up, w_exp_down)
    dest = jnp.where(jnp.arange(n, dtype=I32)[:, None] < n_valid, dest, 0)
    return _combine(dest, y, wt_p, h_all, x1_all, gate2_tab, n_prompt, g_ffn_post,
                    w_sh_up.astype(BF16), w_sh_down.astype(BF16))


def _kmean_pages_kernel(pt_ref, *refs):
    pages, o_ref = refs[:-1], refs[-1]
    s_idx = pl.program_id(1)
    per_step = len(pages) // 2

    @pl.when(s_idx == 0)
    def _():
        o_ref[...] = jnp.zeros_like(o_ref)

    lane = lax.broadcasted_iota(I32, (512, LANES), 1)
    acc = o_ref[0]
    for u in range(per_step):
        tot = jnp.sum(pages[2 * u][0] + pages[2 * u + 1][0], axis=1, keepdims=True)
        acc = jnp.where(lane == s_idx * per_step + u, tot * (1.0 / MOBA_BLOCK), acc)
    o_ref[0] = acc


def _kmean_pages(cache_a, page_table):
    bs, n_pages = page_table.shape
    pps = PAGES_PER_STEP
    assert n_pages // 2 <= LANES
    in_specs = [pl.BlockSpec((1, 512, PAGE_SIZE), lambda b, s, pt, u=u: (pt[b, s * pps + u], 0, 0)) for u in range(pps)]
    grid_spec = pltpu.PrefetchScalarGridSpec(
        num_scalar_prefetch=1, grid=(bs, n_pages // pps), in_specs=in_specs,
        out_specs=pl.BlockSpec((1, 512, LANES), lambda b, s, pt: (b, 0, 0)))
    return pl.pallas_call(
        _kmean_pages_kernel, grid_spec=grid_spec,
        out_shape=jax.ShapeDtypeStruct((bs, 512, LANES), F32),
        compiler_params=_cparams("arbitrary", "arbitrary"),
        name="kmean_pages",
    )(page_table, *([cache_a] * pps))


def _moba_sample_kernel(pt_ref, qbd_ref, km_ref, new_ref, *refs, nb, ts, past):
    pps = PAGES_PER_STEP
    pages = refs[:pps]
    o_ref, sel_scr, m_scr, l_scr, acc_scr = refs[pps:]
    s_idx = pl.program_id(1)
    rows = H_MOBA * ts
    qbd = qbd_ref[0]
    qb = qbd.astype(BF16)
    rowi = lax.broadcasted_iota(I32, (rows, 1), 0)
    slope = jnp.zeros((rows, 1), F32)
    for h in range(H_MOBA):
        slope = jnp.where(_div_pow2(rowi, ts) == h, SLOPES_A[h], slope)
    lane = lax.broadcasted_iota(I32, (rows, LANES), 1)

    @pl.when(s_idx == 0)
    def _():
        gate = _dot3(qbd, km_ref[0])
        gate = jnp.where(lane < nb, gate, NEG_INF)
        sel = jnp.zeros((rows, LANES), F32)
        for first, m in _topk_axis1(gate, MOBA_TOPK, lane):
            sel = jnp.where((lane == first) & (m > NEG_INF), 1.0, sel)
        sel_scr[...] = sel
        m_scr[...] = jnp.full_like(m_scr, M_INIT)
        l_scr[...] = jnp.zeros_like(l_scr)
        acc_scr[...] = jnp.zeros_like(acc_scr)

    def update(ss, vTs):
        m = m_scr[...]
        m_new = m
        for s in ss:
            m_new = jnp.maximum(m_new, jnp.max(s, axis=1, keepdims=True))
        alpha = jnp.exp(m - m_new)
        lsum = jnp.zeros_like(m)
        pv = jnp.zeros(acc_scr.shape, F32)
        for s, vT in zip(ss, vTs):
            p = jnp.exp(s - m_new)
            lsum = lsum + jnp.sum(p, axis=1, keepdims=True)
            pv = pv + lax.dot_general(p.astype(BF16), vT(), _NT, preferred_element_type=F32)
        l_scr[...] = l_scr[...] * alpha + lsum
        acc_scr[...] = acc_scr[...] * alpha + pv
        m_scr[...] = m_new

    sel = sel_scr[...]
    ss, vTs = [], []
    for u in range(pps):
        page = s_idx * pps + u
        flag = jnp.sum(jnp.where(lane == page // 2, sel, 0.0), axis=1, keepdims=True) > 0.0
        kT = pages[u][0, 0:512, :].astype(BF16)
        s = jnp.dot(qb, kT, preferred_element_type=F32)
        kpos = page * PAGE_SIZE + lane - past
        ss.append(jnp.where(flag, s + slope * kpos.astype(F32), NEG_INF))
        vTs.append(lambda u=u: pages[u][0, 512:1024, :].astype(BF16))
    update(ss, vTs)

    @pl.when(s_idx == pl.num_programs(1) - 1)
    def _():
        kn = new_ref[0, 0:512, :].astype(BF16)
        s = jnp.dot(qb, kn, preferred_element_type=F32)
        ok = (lane < ts) & (lane <= _mod_pow2(rowi, ts))
        s = jnp.where(ok, s + slope * lane.astype(F32), NEG_INF)
        update([s], [lambda: new_ref[0, 512:1024, :].astype(BF16)])
        o = acc_scr[...] / jnp.maximum(l_scr[...], 1e-30)
        col_h = _div_pow2(lax.broadcasted_iota(I32, (rows, 512), 1), D_HEAD)
        o = jnp.where(col_h == _div_pow2(rowi, ts), o, 0.0)
        pick_r = lax.broadcasted_iota(I32, (8, rows), 0)
        pick_c = lax.broadcasted_iota(I32, (8, rows), 1)
        gather = jnp.where(_mod_pow2(pick_c, ts) == pick_r, 1.0, 0.0)
        o_ref[0] = _dot3(gather, o)


def _moba_sample(qbd, km_pad, new_pad, cache_a, page_table, *, nb, ts, past):
    bs, n_pages = page_table.shape
    pps = PAGES_PER_STEP
    rows = H_MOBA * ts
    in_specs = [pl.BlockSpec((1, rows, 512), lambda b, s, pt: (b, 0, 0)),
                pl.BlockSpec((1, 512, LANES), lambda b, s, pt: (b, 0, 0)),
                pl.BlockSpec((1, 1024, LANES), lambda b, s, pt: (b, 0, 0))]
    in_specs += [pl.BlockSpec((1, 1024, PAGE_SIZE), lambda b, s, pt, u=u: (pt[b, s * pps + u], 0, 0)) for u in range(pps)]
    grid_spec = pltpu.PrefetchScalarGridSpec(
        num_scalar_prefetch=1, grid=(bs, n_pages // pps), in_specs=in_specs,
        out_specs=pl.BlockSpec((1, 8, 512), lambda b, s, pt: (b, 0, 0)),
        scratch_shapes=[pltpu.VMEM((rows, LANES), F32), pltpu.VMEM((rows, 1), F32), pltpu.VMEM((rows, 1), F32),
                        pltpu.VMEM((rows, 512), F32)])
    return pl.pallas_call(
        functools.partial(_moba_sample_kernel, nb=nb, ts=ts, past=past),
        grid_spec=grid_spec,
        out_shape=jax.ShapeDtypeStruct((bs, 8, 512), F32),
        compiler_params=_cparams("arbitrary", "arbitrary"),
        name="moba_sample",
    )(page_table, qbd, km_pad, new_pad, *([cache_a] * pps))


def _nsa_sample_kernel(pt_ref, idx_ref, qs_ref, qw_ref, kpos_ref, new_ref, win_ref, wnew_ref, cache_hbm,
                       os_ref, ow_ref, buf, sem, *, ts, past, n_slots):
    b = pl.program_id(0)
    ngq = G_NSA * ts
    copies = []
    for gq in range(ngq):
        for t in range(n_slots):
            blk = jnp.maximum(idx_ref[(b * ngq + gq) * SLC_TOPK + t], 0)
            page = pt_ref[b, blk // 2]
            cp = pltpu.make_async_copy(cache_hbm.at[page, pl.ds(256, 256), :],
                                       buf.at[gq, :, pl.ds(t * PAGE_SIZE, PAGE_SIZE)], sem.at[0])
            cp.start()
            copies.append(cp)
    new = new_ref[0]
    for gq in range(ngq):
        buf[gq, :, pl.ds(n_slots * PAGE_SIZE, PAGE_SIZE)] = new
    for cp in copies:
        cp.wait()

    rowi = lax.broadcasted_iota(I32, (8, 1), 0)
    for gq in range(ngq):
        g, q = divmod(gq, ts)
        slope = jnp.zeros((8, 1), F32)
        for r in range(R_NSA):
            slope = jnp.where(rowi == r, SLOPES_B[g * R_NSA + r], slope)
        kT = buf[gq, 0:128, :].astype(BF16)
        vT = buf[gq, 128:256, :].astype(BF16)
        s = jnp.dot(qs_ref[0, gq].astype(BF16), kT, preferred_element_type=F32)
        kpos = kpos_ref[0, gq]
        ok = (kpos >= 0) & (kpos <= past + q)
        s = jnp.where(ok, s + slope * (kpos - past).astype(F32), NEG_INF)
        m = jnp.max(s, axis=1, keepdims=True)
        m = jnp.where(m > NEG_INF, m, 0.0)
        p = jnp.exp(s - m)
        p = p * (1.0 / jnp.maximum(jnp.sum(p, axis=1, keepdims=True), 1e-30))
        os_ref[0, gq] = lax.dot_general(p.astype(BF16), vT, _NT, preferred_element_type=F32)

    nw = WINDOW + LANES
    kw_all = jnp.concatenate([win_ref[0], wnew_ref[0]], axis=1)
    kw = kw_all[0:128, :].astype(BF16)
    vw = kw_all[128:256, :].astype(BF16)
    wrow = lax.broadcasted_iota(I32, (R_NSA * ts, 1), 0)
    wlane = lax.broadcasted_iota(I32, (R_NSA * ts, nw), 1)
    dist = (WINDOW + _mod_pow2(wrow, ts)) - wlane
    okw = (dist >= 0) & (dist <= WINDOW) & (wlane < WINDOW + ts)
    for g in range(G_NSA):
        slope = jnp.zeros((R_NSA * ts, 1), F32)
        for r in range(R_NSA):
            slope = jnp.where(_div_pow2(wrow, ts) == r, SLOPES_B[g * R_NSA + r], slope)
        s = jnp.dot(qw_ref[0, g].astype(BF16), kw, preferred_element_type=F32)
        s = jnp.where(okw, s - slope * dist.astype(F32), NEG_INF)
        m = jnp.max(s, axis=1, keepdims=True)
        m = jnp.where(m > NEG_INF, m, 0.0)
        p = jnp.exp(s - m)
        p = p * (1.0 / jnp.maximum(jnp.sum(p, axis=1, keepdims=True), 1e-30))
        ow_ref[0, g] = lax.dot_general(p.astype(BF16), vw, _NT, preferred_element_type=F32)


def _nsa_sample(idx_flat, qs, qw, kpos, new_pad, win, wnew_pad, cache_b, page_table, *, ts, past, n_slots):
    bs = page_table.shape[0]
    ngq = G_NSA * ts
    nk = (n_slots + 1) * PAGE_SIZE
    m4 = lambda shape: pl.BlockSpec(shape, lambda b, pt, ix: (b,) + (0,) * (len(shape) - 1))
    grid_spec = pltpu.PrefetchScalarGridSpec(
        num_scalar_prefetch=2, grid=(bs,),
        in_specs=[m4((1, ngq, 8, 128)), m4((1, G_NSA, R_NSA * ts, 128)), m4((1, ngq, 1, nk)),
                  m4((1, 256, LANES)), m4((1, 256, WINDOW)), m4((1, 256, LANES)),
                  pl.BlockSpec(memory_space=pl.ANY)],
        out_specs=[m4((1, ngq, 8, 128)), m4((1, G_NSA, R_NSA * ts, 128))],
        scratch_shapes=[pltpu.VMEM((ngq, 256, nk), F32), pltpu.SemaphoreType.DMA((1,))])
    return pl.pallas_call(
        functools.partial(_nsa_sample_kernel, ts=ts, past=past, n_slots=n_slots),
        grid_spec=grid_spec,
        out_shape=[jax.ShapeDtypeStruct((bs, ngq, 8, 128), F32),
                   jax.ShapeDtypeStruct((bs, G_NSA, R_NSA * ts, 128), F32)],
        compiler_params=_cparams("arbitrary"),
        name="nsa_sample",
    )(page_table, idx_flat, qs, qw, kpos, new_pad, win, wnew_pad, cache_b)


def _vT_blocks(v, tk):
    t, c = v.shape
    return v.astype(BF16).reshape(t // tk, tk, c // 128, 128).transpose(2, 0, 3, 1)


def _prompt_mixer(proj, cw):
    qa, kva, qb, kvb, kvw, _ = proj
    t = qa.shape[0]
    qaT = (qa * Q_SCALE).T
    qbT = (qb * Q_SCALE).T
    kmean = _kmean_prompt(kva)
    sel_a = _moba_gate_prompt(qaT, kmean)
    oaT = _flash_select(qaT, kva[:, :512].astype(BF16), _vT_blocks(kva[:, 512:], TKF), sel_a, HEADS_MOBA,
                        block=MOBA_BLOCK)
    cmp_out = _compress(kvb, None, cw, n_rows=t, n_batch=1)
    kc = cmp_out[:, :, 0:128].astype(BF16)
    vcT = cmp_out[:, :, 128:256].astype(BF16).transpose(0, 2, 1)
    pos = jnp.arange(t, dtype=I32).reshape(1, t)
    ocT, sel_b, _ = _cmp_attn(qbT, pos, kc, vcT, t // SLC_BLOCK, k_sel=SLC_TOPK, tq=TQ, tiles_per_batch=0)
    osT = _flash_select(qbT, kvb[:, 256:384].astype(BF16), _vT_blocks(kvb[:, 384:512], TKF), sel_b, HEADS_NSA,
                        block=SLC_BLOCK)
    owT = _flash_select(qbT, kvw[:, 0:128].astype(BF16), _vT_blocks(kvw[:, 128:256], TKF), None, HEADS_NSA,
                        block=TKF, window=WINDOW)
    return oaT.T, ocT.T, osT.T, owT.T


def _sample_mixer(proj, cache_a, cache_b, win_state, page_table, cw, bs, ts):
    qa, kva, qb, kvb, kvw, _ = proj
    n_pages = page_table.shape[1]
    past = n_pages * PAGE_SIZE
    nb = past // MOBA_BLOCK
    new_rows_T = lambda a: jnp.zeros((bs, LANES, a.shape[-1]), F32).at[:, :ts].set(a).transpose(0, 2, 1)
    km = _kmean_pages(cache_a, page_table)
    q4 = (qa * Q_SCALE).reshape(bs, ts, H_MOBA, D_HEAD).transpose(0, 2, 1, 3)
    eye = jnp.eye(H_MOBA, dtype=F32)
    qbd = (q4[:, :, :, None, :] * eye[None, :, None, :, None]).reshape(bs, H_MOBA * ts, 512)
    new_a = new_rows_T(kva.reshape(bs, ts, 1024))
    oa = _moba_sample(qbd, km, new_a, cache_a, page_table, nb=nb, ts=ts, past=past)[:, :ts]
    oa = oa.reshape(bs * ts, 512)
    cmp_out = _compress(cache_b, page_table, cw, n_rows=past, n_batch=bs)
    kc = cmp_out[:, :, 0:128].astype(BF16)
    vcT = cmp_out[:, :, 128:256].astype(BF16).transpose(0, 2, 1)
    qbs = (qb * Q_SCALE).reshape(bs, ts, 512)
    qT = jnp.zeros((bs, LANES, 512), F32).at[:, :ts].set(qbs).reshape(bs * LANES, 512).T
    pos = jnp.broadcast_to(past + jnp.minimum(jnp.arange(LANES, dtype=I32), ts - 1), (bs, LANES)).reshape(1, bs * LANES)
    n_slots = SLC_TOPK - 1
    ocT, _, idx = _cmp_attn(qT, pos, kc, vcT, past // SLC_BLOCK, k_sel=n_slots, tq=LANES, tiles_per_batch=1)
    oc = ocT.T.reshape(bs, LANES, 512)[:, :ts].reshape(bs * ts, 512)
    idx = idx.reshape(G_NSA, SLC_TOPK, bs, LANES)[:, :, :, :ts].transpose(2, 0, 3, 1)
    q5 = qbs.reshape(bs, ts, G_NSA, R_NSA, D_HEAD)
    lane_g = jnp.eye(G_NSA, dtype=F32)
    qsel = q5.transpose(0, 2, 1, 3, 4)[:, :, :, :, None, :] * lane_g[None, :, None, None, :, None]
    qs = jnp.zeros((bs, G_NSA, ts, 8, 128), F32).at[:, :, :, :R_NSA].set(qsel.reshape(bs, G_NSA, ts, R_NSA, 128))
    qs = qs.reshape(bs, G_NSA * ts, 8, 128)
    qwin = q5.transpose(0, 2, 3, 1, 4)[:, :, :, :, None, :] * lane_g[None, :, None, None, :, None]
    qw = qwin.reshape(bs, G_NSA, R_NSA * ts, 128)
    sl = idx[..., :n_slots, None]
    lane = jnp.arange(PAGE_SIZE, dtype=I32)
    in_blk = (sl >= 0) & ((lane // SLC_BLOCK) == (sl % 2))
    slot_pos = jnp.where(in_blk, (sl // 2) * PAGE_SIZE + lane, -1).reshape(bs, G_NSA, ts, n_slots * PAGE_SIZE)
    own = jnp.where(lane < ts, past + lane, -1)
    kpos = jnp.concatenate([slot_pos, jnp.broadcast_to(own, (bs, G_NSA, ts, PAGE_SIZE))], axis=-1)
    kpos = kpos.reshape(bs, G_NSA * ts, 1, (n_slots + 1) * PAGE_SIZE).astype(I32)
    new_b = new_rows_T(kvb.reshape(bs, ts, 512)[:, :, 256:])
    wnew = new_rows_T(kvw.reshape(bs, ts, 256))
    os_raw, ow_raw = _nsa_sample(idx.reshape(-1).astype(I32), qs, qw, kpos, new_b, win_state, wnew, cache_b,
                                 page_table, ts=ts, past=past, n_slots=n_slots)
    os5 = os_raw.reshape(bs, G_NSA, ts, 8, G_NSA, D_HEAD)[:, :, :, :R_NSA]
    os_ = jnp.stack([os5[:, g, :, :, g] for g in range(G_NSA)], axis=2).reshape(bs * ts, 512)
    ow5 = ow_raw.reshape(bs, G_NSA, R_NSA, ts, G_NSA, D_HEAD)
    ow = jnp.stack([ow5[:, g, :, :, g] for g in range(G_NSA)], axis=1)
    ow = ow.transpose(0, 3, 1, 2, 4).reshape(bs * ts, 512)
    return oa, oc, os_, ow


def _layer(x_p, x_s, cache_a, cache_b, win_state, page_table, c_p, c_s, w):
    (w_ada, b_ada, g_mix_pre, g_mix_post, g_ffn_pre, g_ffn_post, w_in, g_out_moba, g_out_nsa, w_out,
     cmp_pe, cmp_w1, cmp_b1, cmp_w2, cmp_b2, w_router, b_router, w_exp_up, w_exp_down, w_sh_up, w_sh_down) = w
    t, d = x_p.shape
    bs, ts, _ = x_s.shape
    assert c_p.shape[0] == 1 and t % (8 * MOBA_BLOCK) == 0 and win_state.shape[1] == WINDOW
    n_s = bs * ts
    rows = -(-(1 + bs) // 8) * 8
    c_all = jnp.zeros((rows, d), F32).at[0:1].set(c_p).at[1:1 + bs].set(c_s)
    mod = _ada(c_all, w_ada, b_ada).reshape(rows, 6, d)
    mod_p = [mod[0:1, i] for i in range(6)]
    mod_s = [jnp.repeat(mod[1:1 + bs, i], ts, axis=0) for i in range(6)]

    w_pad = jnp.zeros((d, _PROJ_CUTS[-1]), F32).at[:, :w_in.shape[1]].set(w_in).astype(BF16)
    cw = _compress_weights(cmp_pe, cmp_w1, cmp_b1, cmp_w2, cmp_b2)
    w_out_bf = w_out.astype(BF16)

    proj_p = _inproj(x_p, mod_p[1], mod_p[0], g_mix_pre, w_pad)
    proj_s = _inproj(x_s.reshape(n_s, d), mod_s[1], mod_s[0], g_mix_pre, w_pad)

    o_p = _prompt_mixer(proj_p, cw)
    win_t = win_state.transpose(0, 2, 3, 1).reshape(bs, 2 * G_NSA * D_HEAD, WINDOW)
    o_s = _sample_mixer(proj_s, cache_a, cache_b, win_t, page_table, cw, bs, ts)

    x1_p, h2_p = _post(x_p, *o_p, proj_p[5], mod_p[2], mod_p[4], mod_p[3], g_out_moba, g_out_nsa, g_mix_post,
                       g_ffn_pre, w_out_bf)
    x1_s, h2_s = _post(x_s.reshape(n_s, d), *o_s, proj_s[5], mod_s[2], mod_s[4], mod_s[3], g_out_moba, g_out_nsa,
                       g_mix_post, g_ffn_pre, w_out_bf)

    n_valid = t + n_s
    n_all = -(-n_valid // ROUTER_TILE) * ROUTER_TILE
    pad = lambda a: jnp.concatenate([a, jnp.zeros((n_all - n_valid, d), F32)], axis=0) if n_all > n_valid else a
    h_all = pad(jnp.concatenate([h2_p, h2_s], axis=0))
    x1_all = pad(jnp.concatenate([x1_p, x1_s], axis=0))
    g2_tab = jnp.concatenate([jnp.broadcast_to(mod_p[5], (128, d)), mod_s[5],
                              jnp.zeros((n_all - n_valid, d), F32)], axis=0)
    y_all = _moe(h_all, x1_all, g2_tab, t, n_valid, w_router, b_router, w_exp_up, w_exp_down, w_sh_up, w_sh_down,
                 g_ffn_post)
    y_p = y_all[:t]
    y_s = y_all[t:n_valid].reshape(bs, ts, d)

    _, kva_p, _, kvb_p, kvw_p, _ = proj_p
    _, kva_s, _, kvb_s, kvw_s, _ = proj_s
    state_p = (kva_p.reshape(1, t, 2 * H_MOBA, D_HEAD), kvb_p.reshape(1, t, 4 * G_NSA, D_HEAD),
               kvw_p[t - min(WINDOW, t):].reshape(1, min(WINDOW, t), 2 * G_NSA, D_HEAD))
    win_new = jnp.concatenate([win_state, kvw_s.reshape(bs, ts, 2 * G_NSA, D_HEAD)], axis=1)[:, -WINDOW:]
    state_s = (kva_s.reshape(bs, ts, 2 * H_MOBA, D_HEAD), kvb_s.reshape(bs, ts, 4 * G_NSA, D_HEAD), win_new)
    return y_p, y_s, state_p, state_s


def kernel(x_prompt, x_sample, cache_moba, cache_nsa, state_nsa_win, page_table, c_prompt, c_sample, w_ada, b_ada, g_mix_pre, g_mix_post, g_ffn_pre, g_ffn_post, w_in, g_out_moba, g_out_nsa, w_out, cmp_pe, cmp_w1, cmp_b1, cmp_w2, cmp_b2, w_router, b_router, w_exp_up, w_exp_down, w_sh_up, w_sh_down):
    weights = (w_ada, b_ada, g_mix_pre, g_mix_post, g_ffn_pre, g_ffn_post, w_in, g_out_moba, g_out_nsa, w_out,
               cmp_pe, cmp_w1, cmp_b1, cmp_w2, cmp_b2, w_router, b_router, w_exp_up, w_exp_down, w_sh_up, w_sh_down)
    depth = w_ada.shape[0]
    n_pool = cache_moba.shape[1]
    y_p, y_s = x_prompt[0], x_sample
    st_p, st_s = [], []
    for layer in range(depth):
        w_l = tuple(w[layer] for w in weights)
        cache_a = cache_moba[layer].transpose(0, 2, 3, 1).reshape(n_pool, 2 * H_MOBA * D_HEAD, PAGE_SIZE)
        cache_b = cache_nsa[layer].transpose(0, 2, 3, 1).reshape(n_pool, 4 * G_NSA * D_HEAD, PAGE_SIZE)
        y_p, y_s, sp, ss = _layer(y_p, y_s, cache_a, cache_b, state_nsa_win[layer], page_table, c_prompt, c_sample, w_l)
        st_p.append(sp)
        st_s.append(ss)
    stack = lambda sts, i: jnp.stack([s[i] for s in sts])
    return (y_p[None], y_s, stack(st_p, 0), stack(st_p, 1), stack(st_p, 2), stack(st_s, 0), stack(st_s, 1), stack(st_s, 2))
```

```python
import functools

import numpy as np
import jax
import jax.numpy as jnp
from jax import lax
from jax.experimental import pallas as pl
from jax.experimental.pallas import tpu as pltpu

F32, BF16, I32 = jnp.float32, jnp.bfloat16, jnp.int32
NEG_INF = float("-inf")
POS_INF = float("inf")
M_INIT = -1e30

D_HEAD = 64
H_MOBA = 8
H_NSA = 8
G_NSA = 2
R_NSA = H_NSA // G_NSA
MOBA_BLOCK = 256
MOBA_TOPK = 3
CMP_STRIDE = 16
CMP_LEN = 2 * CMP_STRIDE
CMP_HIDDEN = 2 * D_HEAD
SLC_BLOCK = 64
SLC_TOPK = 16
SLC_INIT = 1
SLC_LOCAL = 2
WINDOW = 512
N_EXPERTS = 256
TOPK_EXPERTS = 8
N_EXPERT_GROUPS = 8
TOPK_GROUPS = 4
ROUTED_SCALE = 2.5
PAGE_SIZE = 128
EPS = 1e-6
Q_SCALE = D_HEAD ** -0.5
LOG2E = 1.4426950408889634

_SLOPES = [2.0 ** (-8.0 * i / (H_MOBA + H_NSA)) for i in range(1, H_MOBA + H_NSA + 1)]
SLOPES_A = _SLOPES[0::2][:H_MOBA]
SLOPES_B = _SLOPES[1::2][:H_NSA]

LANES = 128
TQ = 256
TKF = 128
VMEM_LIMIT_BYTES = 56 * 1024 * 1024
ROW_BLOCK = 128
ROUTER_TILE = 256
PAGES_PER_STEP = 16


def _cparams(*sem):
    return pltpu.CompilerParams(dimension_semantics=sem, vmem_limit_bytes=VMEM_LIMIT_BYTES)


def _rms(x, g):
    return x * lax.rsqrt(jnp.mean(x * x, axis=-1, keepdims=True) + EPS) * g


def _split_bf16(a):
    hi = a.astype(BF16)
    lo = (a - hi.astype(F32)).astype(BF16)
    return hi, lo


def _dot3(a, b, dims=None):
    ah, al = _split_bf16(a)
    bh, bl = _split_bf16(b)
    if dims is None:
        d = lambda x, y: jnp.dot(x, y, preferred_element_type=F32)
    else:
        d = lambda x, y: lax.dot_general(x, y, dims, preferred_element_type=F32)
    return d(ah, bh) + (d(ah, bl) + d(al, bh))


_NT = (((1,), (1,)), ((), ()))


def _div_pow2(x, n):
    assert n & (n - 1) == 0
    return jnp.right_shift(x, n.bit_length() - 1)


def _mod_pow2(x, n):
    assert n & (n - 1) == 0
    return jnp.bitwise_and(x, n - 1)


def _head_rows(q64, half):
    z = jnp.zeros_like(q64)
    return jnp.concatenate([z, q64] if half else [q64, z], axis=0)


def _ada_kernel(c_ref, w_ref, b_ref, o_ref):
    a = jax.nn.silu(c_ref[...])
    o_ref[...] = _dot3(a, w_ref[...]) + b_ref[...]


def _ada(c, w_ada, b_ada):
    r, d = c.shape
    n = w_ada.shape[1]
    tn = 768
    return pl.pallas_call(
        _ada_kernel,
        grid=(n // tn,),
        in_specs=[pl.BlockSpec((r, d), lambda j: (0, 0)),
                  pl.BlockSpec((d, tn), lambda j: (0, j)),
                  pl.BlockSpec((1, tn), lambda j: (0, j))],
        out_specs=pl.BlockSpec((r, tn), lambda j: (0, j)),
        out_shape=jax.ShapeDtypeStruct((r, n), F32),
        compiler_params=_cparams("arbitrary"),
        name="ada",
    )(c, w_ada, b_ada.reshape(1, n))


_PROJ_CUTS = (0, 512, 1536, 2048, 2560, 2816, 2944)


def _inproj_kernel(x_ref, sc_ref, sh_ref, g_ref, w_ref, *out_refs):
    h = _rms(x_ref[...], g_ref[...]) * (1.0 + sc_ref[...]) + sh_ref[...]
    hb = h.astype(BF16)
    for o_ref, a, b in zip(out_refs, _PROJ_CUTS[:-1], _PROJ_CUTS[1:]):
        o_ref[...] = jnp.dot(hb, w_ref[:, a:b], preferred_element_type=F32)


def _inproj(x, scale, shift, g, w_pad):
    r, d = x.shape
    tm = min(512, r)
    per_row = scale.shape[0] != 1
    mod_spec = pl.BlockSpec((tm, d), lambda i: (i, 0)) if per_row else pl.BlockSpec((1, d), lambda i: (0, 0))
    widths = [b - a for a, b in zip(_PROJ_CUTS[:-1], _PROJ_CUTS[1:])]
    return pl.pallas_call(
        _inproj_kernel,
        grid=(r // tm,),
        in_specs=[pl.BlockSpec((tm, d), lambda i: (i, 0)), mod_spec, mod_spec,
                  pl.BlockSpec((1, d), lambda i: (0, 0)),
                  pl.BlockSpec(w_pad.shape, lambda i: (0, 0))],
        out_specs=[pl.BlockSpec((tm, w), lambda i: (i, 0)) for w in widths],
        out_shape=[jax.ShapeDtypeStruct((r, w), F32) for w in widths],
        compiler_params=_cparams("arbitrary"),
        name="inproj",
    )(x, scale, shift, g.reshape(1, d), w_pad)


def _topk_axis0(score, k, idx):
    n = score.shape[0]
    sel = jnp.zeros(score.shape, F32)
    picks = []
    for _ in range(k):
        m = jnp.max(score, axis=0, keepdims=True)
        first = jnp.min(jnp.where(score == m, idx, n), axis=0, keepdims=True)
        pick = idx == first
        ok = m > NEG_INF
        sel = jnp.where(pick & ok, 1.0, sel)
        picks.append(jnp.where(ok, first, -1))
        score = jnp.where(pick, NEG_INF, score)
    return sel, picks


def _topk_axis1(score, k, idx):
    n = score.shape[1]
    out = []
    for _ in range(k):
        m = jnp.max(score, axis=1, keepdims=True)
        first = jnp.min(jnp.where(score == m, idx, n), axis=1, keepdims=True)
        out.append((first, m))
        score = jnp.where(idx == first, NEG_INF, score)
    return out


def _kmean_kernel(k_ref, o_ref):
    x = k_ref[...]
    n = x.shape[0] // MOBA_BLOCK
    o_ref[...] = jnp.sum(x.reshape(n, MOBA_BLOCK, x.shape[1]), axis=1) * (1.0 / MOBA_BLOCK)


def _kmean_prompt(kva):
    t = kva.shape[0]
    nb = t // MOBA_BLOCK
    per = 8
    return pl.pallas_call(
        _kmean_kernel,
        grid=(nb // per,),
        in_specs=[pl.BlockSpec((per * MOBA_BLOCK, 512), lambda i: (i, 0))],
        out_specs=pl.BlockSpec((per, 512), lambda i: (i, 0)),
        out_shape=jax.ShapeDtypeStruct((nb, 512), F32),
        compiler_params=_cparams("arbitrary"),
        name="kmean_prompt",
    )(kva)


def _moba_gate_kernel(qT_ref, km_ref, sel_ref, *, nb):
    own = pl.program_id(0)
    blk = lax.broadcasted_iota(I32, (nb, TQ), 0)
    for h in range(H_MOBA):
        p, half = divmod(h, 2)
        qh = _head_rows(qT_ref[h * 64:(h + 1) * 64, :], half)
        g = _dot3(km_ref[:, p * 128:(p + 1) * 128], qh)
        g = jnp.where(blk < own, g, NEG_INF)
        sel, _ = _topk_axis0(g, MOBA_TOPK, blk)
        sel_ref[h] = jnp.where(blk == own, 1.0, sel)


def _moba_gate_prompt(qT, kmean):
    t = qT.shape[1]
    nb = kmean.shape[0]
    return pl.pallas_call(
        functools.partial(_moba_gate_kernel, nb=nb),
        grid=(t // TQ,),
        in_specs=[pl.BlockSpec((512, TQ), lambda i: (0, i)),
                  pl.BlockSpec((nb, 512), lambda i: (0, 0))],
        out_specs=pl.BlockSpec((H_MOBA, nb, TQ), lambda i: (0, 0, i)),
        out_shape=jax.ShapeDtypeStruct((H_MOBA, nb, t), F32),
        compiler_params=_cparams("arbitrary"),
        name="moba_gate",
    )(qT, kmean)


def _flash_kernel(qT_ref, k_ref, vT_ref, *rest, heads, sub, sel_div, window):
    nh = len(heads)
    if window is None:
        sel_ref, o_ref, qh_scr, sc_scr, *state = rest
    else:
        sel_ref, (o_ref, qh_scr, sc_scr, *state) = None, rest
    m_scr, l_scr, acc_scr = state[:nh], state[nh:2 * nh], state[2 * nh:]
    i = pl.program_id(0)
    kio = lax.broadcasted_iota(I32, (TKF, TQ), 0)
    qio = lax.broadcasted_iota(I32, (TKF, TQ), 1)
    kiof = kio.astype(F32)
    rows = TKF // sub
    for h, (pair, half, set_idx, slope) in enumerate(heads):
        slope = slope * LOG2E
        qh_scr[h] = _head_rows(qT_ref[h * 64:(h + 1) * 64, :] * LOG2E, half).astype(BF16)
        sc_scr[h] = kiof * slope
        m_scr[h][...] = jnp.full((1, TQ), M_INIT, F32)
        l_scr[h][...] = jnp.zeros((1, TQ), F32)
        acc_scr[h][...] = jnp.zeros((64, TQ), F32)

    def sel_rows(set_idx, j):
        if sub == 1:
            return sel_ref[set_idx, pl.ds(j // sel_div, 1), :] > 0.0
        per8 = 8 // sub
        blk8 = sel_ref[set_idx, pl.ds(pl.multiple_of((j // per8) * 8, 8), 8), :]
        s = blk8[0:sub]
        for u in range(1, per8):
            s = jnp.where(j % per8 == u, blk8[u * sub:(u + 1) * sub], s)
        return s > 0.0

    def step(j, diag=None, band=None):
        off_base = (j * TKF - i * TQ).astype(F32)
        jc = j if band is None else jnp.maximum(j, 0)
        row0 = pl.multiple_of(jc * TKF, TKF)
        if band is not None:
            dist = qio - kio - band * TKF
            in_band = (dist >= 0) & (dist <= window) & (j >= 0)
        sel_cache = {}
        for h, (pair, half, set_idx, slope) in enumerate(heads):
            kj = k_ref[pl.ds(row0, TKF), pair * 128:(pair + 1) * 128]
            t = jnp.dot(kj, qh_scr[h], preferred_element_type=F32) + sc_scr[h]
            if diag is not None:
                t = jnp.where(kio + diag * TKF <= qio, t, NEG_INF)
            if band is not None:
                t = jnp.where(in_band, t, NEG_INF)
            off = (slope * LOG2E) * off_base
            t3 = t.reshape(sub, rows, TQ)
            m = m_scr[h][...]
            if sel_ref is None:
                m_new = jnp.maximum(m, jnp.max(jnp.max(t3, axis=1), axis=0, keepdims=True) + off)
                mu = jnp.broadcast_to(m_new - off, (sub, TQ))
            else:
                if set_idx not in sel_cache:
                    sel_cache[set_idx] = sel_rows(set_idx, j)
                selj = sel_cache[set_idx]
                mb = jnp.where(selj, jnp.max(t3, axis=1) + off, M_INIT)
                m_new = jnp.maximum(m, jnp.max(mb, axis=0, keepdims=True))
                mu = jnp.where(selj, m_new - off, POS_INF)
            p = jnp.exp2(t3 - mu[:, None, :]).reshape(TKF, TQ)
            alpha = jnp.exp2(m - m_new)
            l_scr[h][...] = l_scr[h][...] * alpha + jnp.sum(p, axis=0, keepdims=True)
            pv = jnp.dot(vT_ref[pair, jc], p.astype(BF16), preferred_element_type=F32)
            acc_scr[h][...] = acc_scr[h][...] * alpha + pv[half * 64:(half + 1) * 64]
            m_scr[h][...] = m_new

    per_q = TQ // TKF
    if window is None:
        def body(jj, c):
            for u in range(per_q):
                step(jj * per_q + u)
            return c

        lax.fori_loop(0, i, body, 0)
        for u in range(per_q):
            step(i * per_q + u, diag=u)
    else:
        for dj in range(-(window // TKF), per_q):
            step(i * per_q + dj, band=dj)
    for h in range(len(heads)):
        o_ref[h * 64:(h + 1) * 64, :] = acc_scr[h][...] / jnp.maximum(l_scr[h][...], 1e-30)


def _flash_select(qT, k, vTb, sel, heads, *, block, window=None):
    t = qT.shape[1]
    nh = len(heads)
    sub, sel_div = max(TKF // block, 1), max(block // TKF, 1)
    resident = lambda shape: pl.BlockSpec(shape, lambda i: (0,) * len(shape), pipeline_mode=pl.Buffered(1))
    in_specs = [pl.BlockSpec((nh * 64, TQ), lambda i: (0, i)), resident(k.shape), resident(vTb.shape)]
    args = (qT, k, vTb)
    if window is None:
        in_specs.append(pl.BlockSpec((sel.shape[0], sel.shape[1], TQ), lambda i: (0, 0, i)))
        args += (sel,)
    return pl.pallas_call(
        functools.partial(_flash_kernel, heads=heads, sub=sub, sel_div=sel_div, window=window),
        grid=(t // TQ,),
        in_specs=in_specs,
        out_specs=pl.BlockSpec((nh * 64, TQ), lambda i: (0, i)),
        out_shape=jax.ShapeDtypeStruct(qT.shape, F32),
        scratch_shapes=([pltpu.VMEM((nh, 128, TQ), BF16), pltpu.VMEM((nh, TKF, TQ), F32)]
                        + [pltpu.VMEM((1, TQ), F32)] * (2 * nh) + [pltpu.VMEM((64, TQ), F32)] * nh),
        compiler_params=_cparams("arbitrary"),
        name="flash_select" if window is None else "flash_window",
    )(*args)


HEADS_MOBA = tuple((h // 2, h % 2, h, SLOPES_A[h]) for h in range(H_MOBA))
HEADS_NSA = tuple((0, h // R_NSA, h // R_NSA, SLOPES_B[h]) for h in range(H_NSA))


def _compress_kernel(*refs, paged, n_rows):
    if paged:
        pt_ref, x_hbm, pe_ref, w1_ref, b1_ref, wcat_ref, w2_ref, b2_ref, o_ref, xs, sem, stage = refs
    else:
        x_hbm, pe_ref, w1_ref, b1_ref, wcat_ref, w2_ref, b2_ref, o_ref, xs, sem = refs
    nc = n_rows // CMP_STRIDE
    if paged:
        b = pl.program_id(0)
        n_pages = n_rows // PAGE_SIZE
        ch = stage.shape[1]

        def page_copy(c, u):
            return pltpu.make_async_copy(x_hbm.at[pt_ref[b, c * ch + u], pl.ds(0, 256), :],
                                         stage.at[c % 2, u], sem.at[c % 2])

        def start_chunk(c):
            def body(u, carry):
                page_copy(c, u).start()
                return carry
            lax.fori_loop(0, ch, body, 0)

        def finish_chunk(c):
            def wait_body(u, carry):
                page_copy(c, u).wait()
                return carry
            lax.fori_loop(0, ch, wait_body, 0)

            def body(u, carry):
                row0 = pl.multiple_of((c * ch + u) * PAGE_SIZE, PAGE_SIZE)
                for c2 in range(2):
                    xs[c2, pl.ds(row0, PAGE_SIZE), :] = stage[c % 2, u, c2 * LANES:(c2 + 1) * LANES, :].T
                return carry
            lax.fori_loop(0, ch, body, 0, unroll=4)

        start_chunk(0)
        xs[:, pl.ds(n_rows, CMP_STRIDE), :] = jnp.zeros((2, CMP_STRIDE, LANES), F32)
        for c in range(n_pages // ch):
            if c + 1 < n_pages // ch:
                start_chunk(c + 1)
            finish_chunk(c)
    else:
        cps = [pltpu.make_async_copy(x_hbm.at[:, pl.ds(c2 * LANES, LANES)], xs.at[c2, pl.ds(0, n_rows), :], sem.at[0])
               for c2 in range(2)]
        for cp in cps:
            cp.start()
        xs[:, pl.ds(n_rows, CMP_STRIDE), :] = jnp.zeros((2, CMP_STRIDE, LANES), F32)
        for cp in cps:
            cp.wait()

    def rows(r, c0, n):
        return jnp.concatenate([xs[c2, pl.ds(r + CMP_STRIDE * c0, n, stride=CMP_STRIDE), :] for c2 in range(2)],
                               axis=1).astype(BF16)

    consts = []
    for j in range(2):
        cj = _dot3(pe_ref[j], w1_ref[j])[0:1] + b1_ref[j]
        consts += [cj, cj]
    const = jnp.concatenate(consts, axis=1)
    rc = min(256, nc)
    for c0 in range(0, nc, rc):
        xcat = jnp.concatenate([rows(r, c0, rc) for r in range(CMP_LEN)], axis=1)
        hid = jnp.dot(xcat, wcat_ref[...], preferred_element_type=F32) + const
        act = jax.nn.gelu(hid)
        out = jnp.dot(act.astype(BF16), w2_ref[...], preferred_element_type=F32) + b2_ref[...]
        row = lax.broadcasted_iota(I32, out.shape, 0) + c0
        o_ref[0, c0:c0 + rc, :] = jnp.where(row < nc - 1, out, 0.0)


def _compress(x, page_table, cw, *, n_rows, n_batch):
    paged = page_table is not None
    nc = n_rows // CMP_STRIDE
    const = lambda shape: pl.BlockSpec(shape, lambda *_: (0,) * len(shape), pipeline_mode=pl.Buffered(1))
    in_specs = [pl.BlockSpec(memory_space=pl.ANY), const((2, 8, 2048)), const((2, 2048, 128)), const((2, 1, 128)),
                const((CMP_LEN * 256, 512)), const((512, 256)), const((1, 256))]
    scratch = [pltpu.VMEM((2, n_rows + CMP_STRIDE, LANES), F32), pltpu.SemaphoreType.DMA((2,))]
    if paged:
        chunk = min(32, n_rows // PAGE_SIZE)
        scratch.append(pltpu.VMEM((2, chunk, 256, PAGE_SIZE), F32))
    grid_spec = pltpu.PrefetchScalarGridSpec(
        num_scalar_prefetch=1 if paged else 0,
        grid=(n_batch,),
        in_specs=in_specs,
        out_specs=pl.BlockSpec((1, nc, 256), lambda b, *_: (b, 0, 0)),
        scratch_shapes=scratch)
    args = ((page_table,) if paged else ()) + (x,) + cw
    return pl.pallas_call(
        functools.partial(_compress_kernel, paged=paged, n_rows=n_rows),
        grid_spec=grid_spec,
        out_shape=jax.ShapeDtypeStruct((n_batch, nc, 256), F32),
        compiler_params=_cparams("arbitrary"),
        name="compress_paged" if paged else "compress",
    )(*args)


def _compress_weights(cmp_pe, cmp_w1, cmp_b1, cmp_w2, cmp_b2):
    half = CMP_STRIDE * D_HEAD
    wa = jnp.zeros((16, 256, 512), F32)
    wb = jnp.zeros((16, 256, 512), F32)
    w2 = jnp.zeros((512, 256), F32)
    for jg in range(4):
        j = jg // 2
        wa = wa.at[:, jg * 64:(jg + 1) * 64, jg * 128:(jg + 1) * 128].set(cmp_w1[j, :half].reshape(16, 64, 128))
        wb = wb.at[:, jg * 64:(jg + 1) * 64, jg * 128:(jg + 1) * 128].set(cmp_w1[j, half:].reshape(16, 64, 128))
        w2 = w2.at[jg * 128:(jg + 1) * 128, jg * 64:(jg + 1) * 64].set(cmp_w2[j])
    pe = jnp.zeros((2, 8, 2048), F32).at[:, 0].set(cmp_pe.reshape(2, 2048))
    b2 = jnp.concatenate([cmp_b2[0], cmp_b2[0], cmp_b2[1], cmp_b2[1]]).reshape(1, 256)
    wcat = jnp.concatenate([wa.reshape(16 * 256, 512), wb.reshape(16 * 256, 512)], axis=0)
    return (pe, cmp_w1, cmp_b1.reshape(2, 1, 128), wcat.astype(BF16), w2.astype(BF16), b2)


def _overlap_matrix(ns, nc_pad):
    ratio = SLC_BLOCK // CMP_STRIDE
    lr = CMP_LEN // CMP_STRIDE
    w = np.zeros((ns, nc_pad), np.float32)
    for s in range(ns):
        for m in range(ratio):
            for n in range(lr):
                c = ratio * s + m - n
                if 0 <= c < nc_pad - 1:
                    w[s, c] += 1.0
    return w


def _cmp_attn_kernel(qT_ref, pos_ref, kc_ref, vcT_ref, wT_ref, oc_ref, sel_ref, idx_ref, *, k_sel, tq, causal_tiles):
    nc_pad = kc_ref.shape[1]
    ns = wT_ref.shape[0]
    ratio = SLC_BLOCK // CMP_STRIDE
    pos = pos_ref[...]

    def run(n):
        nsu = n // ratio
        cidx = lax.broadcasted_iota(I32, (n, tq), 0)
        cpos = cidx * CMP_STRIDE + (CMP_LEN - 1)
        valid = (cpos <= pos) & (cidx < nc_pad - 1)
        ndist = (cpos - pos).astype(F32)
        blk = lax.broadcasted_iota(I32, (nsu, tq), 0)
        d = jnp.right_shift(pos, 6) - blk
        forced = (blk < SLC_INIT) | ((d >= 0) & (d < SLC_LOCAL))
        kc = kc_ref[0, 0:n, :]
        vcT = vcT_ref[0, :, 0:n]
        wT = wT_ref[0:nsu, 0:n]
        for g in range(G_NSA):
            imp = jnp.zeros((n, tq), F32)
            for r in range(R_NSA):
                h = g * R_NSA + r
                qh = _head_rows(qT_ref[h * 64:(h + 1) * 64, :], g).astype(BF16)
                s = jnp.dot(kc, qh, preferred_element_type=F32) + ndist * SLOPES_B[h]
                s = jnp.where(valid, s, NEG_INF)
                m = jnp.max(s, axis=0, keepdims=True)
                m = jnp.where(m > NEG_INF, m, 0.0)
                p = jnp.exp(s - m)
                p = p * (1.0 / jnp.maximum(jnp.sum(p, axis=0, keepdims=True), 1e-30))
                imp = imp + p
                oc = jnp.dot(vcT, p.astype(BF16), preferred_element_type=F32)
                oc_ref[h * 64:(h + 1) * 64, :] = oc[g * 64:(g + 1) * 64]
            hi, lo = _split_bf16(imp)
            islc = jnp.dot(wT, hi, preferred_element_type=F32) + jnp.dot(wT, lo, preferred_element_type=F32)
            score = jnp.where(forced, POS_INF, jnp.where(d >= 0, islc, NEG_INF))
            sel, picks = _topk_axis0(score, k_sel, blk)
            sel_ref[g, 0:nsu, :] = sel
            if nsu < ns:
                sel_ref[g, nsu:ns, :] = jnp.zeros((ns - nsu, tq), F32)
            for r, pk in enumerate(picks):
                idx_ref[g, r:r + 1, :] = pk
            for r in range(len(picks), SLC_TOPK):
                idx_ref[g, r:r + 1, :] = jnp.full((1, tq), -1, I32)

    if causal_tiles and nc_pad % (4 * LANES) == 0:
        quarter = nc_pad // 4
        tiles_per_quarter = quarter * CMP_STRIDE // tq
        v_sel = pl.program_id(0) // tiles_per_quarter
        for v in range(4):
            pl.when(v_sel == v)(functools.partial(run, (v + 1) * quarter))
    else:
        run(nc_pad)


def _cmp_attn(qT, pos, kc, vcT, ns, *, k_sel, tq, tiles_per_batch):
    n = qT.shape[1]
    nc_pad = kc.shape[1]
    wT = jnp.asarray(_overlap_matrix(ns, nc_pad), BF16)
    bmap = (lambda i: (i // tiles_per_batch, 0, 0)) if tiles_per_batch else (lambda i: (0, 0, 0))
    return pl.pallas_call(
        functools.partial(_cmp_attn_kernel, k_sel=k_sel, tq=tq, causal_tiles=(tiles_per_batch == 0)),
        grid=(n // tq,),
        in_specs=[pl.BlockSpec((512, tq), lambda i: (0, i)),
                  pl.BlockSpec((1, tq), lambda i: (0, i)),
                  pl.BlockSpec((1, nc_pad, 128), bmap),
                  pl.BlockSpec((1, 128, nc_pad), bmap),
                  pl.BlockSpec((ns, nc_pad), lambda i: (0, 0))],
        out_specs=[pl.BlockSpec((512, tq), lambda i: (0, i)),
                   pl.BlockSpec((G_NSA, ns, tq), lambda i: (0, 0, i)),
                   pl.BlockSpec((G_NSA, SLC_TOPK, tq), lambda i: (0, 0, i))],
        out_shape=[jax.ShapeDtypeStruct((512, n), F32),
                   jax.ShapeDtypeStruct((G_NSA, ns, n), F32),
                   jax.ShapeDtypeStruct((G_NSA, SLC_TOPK, n), I32)],
        compiler_params=_cparams("arbitrary"),
        name="cmp_attn",
    )(qT, pos, kc, vcT, wT)


def _post_kernel(x_ref, oa_ref, oc_ref, os_ref, ow_ref, gb_ref, g1_ref, sc2_ref, sh2_ref,
                 goa_ref, gob_ref, gpost_ref, gffn_ref, e3_ref, wout_ref, x1_ref, h2_ref):
    gates = jax.nn.sigmoid(gb_ref[...])
    ob = (_dot3(gates, e3_ref[0]) * oc_ref[...] + _dot3(gates, e3_ref[1]) * os_ref[...]
          + _dot3(gates, e3_ref[2]) * ow_ref[...])
    oa_n = _rms(oa_ref[...], goa_ref[...]).astype(BF16)
    ob_n = _rms(ob, gob_ref[...]).astype(BF16)
    o = (jnp.dot(oa_n, wout_ref[0:512, :], preferred_element_type=F32)
         + jnp.dot(ob_n, wout_ref[512:1024, :], preferred_element_type=F32))
    x1 = x_ref[...] + g1_ref[...] * _rms(o, gpost_ref[...])
    x1_ref[...] = x1
    h2_ref[...] = _rms(x1, gffn_ref[...]) * (1.0 + sc2_ref[...]) + sh2_ref[...]


def _gate_expand():
    e = np.zeros((3, 128, 512), np.float32)
    for h in range(H_NSA):
        for j in range(3):
            e[j, 3 * h + j, h * 64:(h + 1) * 64] = 1.0
    return jnp.asarray(e)


def _post(x, oa, oc, os_, ow, gb, gate1, scale2, shift2, g_oa, g_ob, g_post, g_ffn, w_out_bf):
    r, d = x.shape
    tm = min(256, r)
    per_row = gate1.shape[0] != 1
    mod = pl.BlockSpec((tm, d), lambda i: (i, 0)) if per_row else pl.BlockSpec((1, d), lambda i: (0, 0))
    row = lambda w: pl.BlockSpec((tm, w), lambda i: (i, 0))
    const = lambda shape: pl.BlockSpec(shape, lambda i: (0,) * len(shape))
    return pl.pallas_call(
        _post_kernel,
        grid=(r // tm,),
        in_specs=[row(d), row(512), row(512), row(512), row(512), row(128), mod, mod, mod,
                  const((1, 512)), const((1, 512)), const((1, d)), const((1, d)),
                  const((3, 128, 512)), const((1024, d))],
        out_specs=[row(d), row(d)],
        out_shape=[jax.ShapeDtypeStruct((r, d), F32), jax.ShapeDtypeStruct((r, d), F32)],
        compiler_params=_cparams("arbitrary"),
        name="post_attn",
    )(x, oa, oc, os_, ow, gb, gate1, scale2, shift2, g_oa.reshape(1, 512), g_ob.reshape(1, 512),
      g_post.reshape(1, d), g_ffn.reshape(1, d), _gate_expand(), w_out_bf)


def _router_kernel(h_ref, wr_ref, br_ref, tri_ref, idx_ref, wt_ref, rank_ref, cnt_ref, run_ref, *, n_valid, tm):
    i = pl.program_id(0)

    @pl.when(i == 0)
    def _():
        run_ref[...] = jnp.zeros_like(run_ref)

    s = jax.nn.sigmoid(_dot3(h_ref[...], wr_ref[...]))
    sel = s + br_ref[...]
    lane = lax.broadcasted_iota(I32, (tm, N_EXPERTS), 1)
    per = N_EXPERTS // N_EXPERT_GROUPS
    grp = jnp.right_shift(lane, 5)
    lane_g = lax.broadcasted_iota(I32, (tm, LANES), 1)
    gscore = jnp.full((tm, LANES), NEG_INF, F32)
    for g in range(N_EXPERT_GROUPS):
        v = jnp.where(grp == g, sel, NEG_INF)
        (i1, m1), (_, m2) = _topk_axis1(v, 2, lane)
        gscore = jnp.where(lane_g == g, m1 + m2, gscore)
    emask = jnp.zeros((tm, N_EXPERTS), jnp.bool_)
    for first, _ in _topk_axis1(gscore, TOPK_GROUPS, lane_g):
        emask = emask | (grp == first)
    picks = _topk_axis1(jnp.where(emask, sel, NEG_INF), TOPK_EXPERTS, lane)
    row = lax.broadcasted_iota(I32, (tm, 1), 0) + i * tm
    row_ok = row < n_valid
    onehot = jnp.zeros((tm, N_EXPERTS), F32)
    ws = []
    for first, _ in picks:
        pick = lane == first
        ws.append(jnp.sum(jnp.where(pick, s, 0.0), axis=1, keepdims=True))
        onehot = jnp.where(pick & row_ok, 1.0, onehot)
    wsum = ws[0]
    for w in ws[1:]:
        wsum = wsum + w
    excl = jnp.dot(tri_ref[...], onehot.astype(BF16), preferred_element_type=F32) + run_ref[...]
    idx_out = jnp.zeros((tm, LANES), I32)
    wt_out = jnp.zeros((tm, LANES), F32)
    rank_out = jnp.zeros((tm, LANES), I32)
    for k, (first, _) in enumerate(picks):
        rk = jnp.sum(jnp.where(lane == first, excl, 0.0), axis=1, keepdims=True)
        idx_out = jnp.where(lane_g == k, first, idx_out)
        wt_out = jnp.where(lane_g == k, ws[k] / wsum * ROUTED_SCALE, wt_out)
        rank_out = jnp.where(lane_g == k, rk.astype(I32), rank_out)
    idx_ref[...] = idx_out
    wt_ref[...] = wt_out
    rank_ref[...] = rank_out
    run_ref[...] = run_ref[...] + jnp.sum(onehot, axis=0, keepdims=True)
    cnt_ref[...] = run_ref[...]


def _router(h_all, w_router, b_router, n_valid):
    n, d = h_all.shape
    tm = ROUTER_TILE
    tri = jnp.asarray(np.tril(np.ones((tm, tm), np.float32), -1), BF16)
    const = lambda shape: pl.BlockSpec(shape, lambda i: (0,) * len(shape))
    row = lambda w: pl.BlockSpec((tm, w), lambda i: (i, 0))
    return pl.pallas_call(
        functools.partial(_router_kernel, n_valid=n_valid, tm=tm),
        grid=(n // tm,),
        in_specs=[row(d), const((d, N_EXPERTS)), const((1, N_EXPERTS)), const((tm, tm))],
        out_specs=[row(LANES), row(LANES), row(LANES), const((1, N_EXPERTS))],
        out_shape=[jax.ShapeDtypeStruct((n, LANES), I32), jax.ShapeDtypeStruct((n, LANES), F32),
                   jax.ShapeDtypeStruct((n, LANES), I32), jax.ShapeDtypeStruct((1, N_EXPERTS), F32)],
        scratch_shapes=[pltpu.VMEM((1, N_EXPERTS), F32)],
        compiler_params=_cparams("arbitrary"),
        name="router",
    )(h_all, w_router, b_router.reshape(1, N_EXPERTS), tri)


def _gather_rows(idx_hbm, idx_smem, isem, src_hbm, buf, sem, step, n_steps, n_rows):
    def idx_copy(k, slot):
        return pltpu.make_async_copy(idx_hbm.at[k], idx_smem.at[slot], isem.at[slot])

    def issue(slot):
        def body(r2, c):
            for prio in range(2):
                r = 2 * r2 + prio
                pltpu.make_async_copy(src_hbm.at[idx_smem[slot, r]], buf.at[slot, r], sem.at[slot]).start(priority=prio)
            return c
        lax.fori_loop(0, n_rows // 2, body, 0, unroll=4)

    @pl.when(step == 0)
    def _():
        idx_copy(0, 0).start()
        idx_copy(0, 0).wait()
        issue(0)

        @pl.when(1 < n_steps)
        def _():
            idx_copy(1, 1).start()

    nxt = (step + 1) % 2

    @pl.when(step + 1 < n_steps)
    def _():
        idx_copy(step + 1, nxt).wait()
        issue(nxt)

    @pl.when(step + 2 < n_steps)
    def _():
        idx_copy(step + 2, step % 2).start()

    slot = step % 2

    @pl.when(step < n_steps)
    def _():
        pltpu.make_async_copy(src_hbm.at[pl.ds(0, n_rows)], buf.at[slot], sem.at[slot]).wait()
    return slot


def _expert_kernel(blk_e_ref, nused_ref, sched_ref, tok_hbm, h_hbm, wup_hbm, wdn_hbm, o_ref, buf0, buf1, idx_smem,
                   sem, isem, wup_bf, wdn_bf, wup_f32, wdn_f32, wsem):
    i = pl.program_id(0)
    nused = nused_ref[0]
    rb = ROW_BLOCK
    bufs = (buf0, buf1)

    def w_copies(e, slot):
        return (pltpu.make_async_copy(wup_hbm.at[e], wup_f32.at[slot], wsem.at[slot]),
                pltpu.make_async_copy(wdn_hbm.at[e], wdn_f32.at[slot], wsem.at[slot]))

    def idx_copy(k, slot):
        return pltpu.make_async_copy(tok_hbm.at[k], idx_smem.at[slot], isem.at[slot])

    def issue_rows(slot):
        for r in range(rb):
            pltpu.make_async_copy(h_hbm.at[idx_smem[slot, r]], bufs[slot].at[r], sem.at[slot]).start()

    def wait_rows(slot):
        pltpu.make_async_copy(h_hbm.at[pl.ds(0, rb)], bufs[slot], sem.at[slot]).wait()

    def compute(slot):
        buf = bufs[slot]
        x = jnp.concatenate([buf[:, s, :] for s in range(8)], axis=1).astype(BF16)
        up = jnp.dot(x, wup_bf[...], preferred_element_type=F32)
        half = up.shape[1] // 2
        act = (jax.nn.silu(up[:, :half]) * up[:, half:]).astype(BF16)
        y = jnp.dot(act, wdn_bf[...], preferred_element_type=F32)
        for s in range(8):
            o_ref[:, s, :] = y[:, s * 128:(s + 1) * 128]

    @pl.when(i == 0)
    def _():
        for cp in w_copies(blk_e_ref[0], 0):
            cp.start()
        idx_copy(0, 0).start()
        idx_copy(0, 0).wait()
        issue_rows(0)

        @pl.when(1 < nused)
        def _():
            idx_copy(1, 1).start()

    @pl.when(i + 2 < nused)
    def _():
        idx_copy(i + 2, i % 2).start()

    first = (sched_ref[0, i] == 1) & (i < nused)
    for wslot in range(2):
        @pl.when(first & (sched_ref[1, i] == wslot))
        def _():
            for cp in w_copies(blk_e_ref[i], wslot):
                cp.wait()

            @pl.when(sched_ref[3, i] == 1)
            def _():
                for cp in w_copies(sched_ref[2, i], 1 - wslot):
                    cp.start()

            wup_bf[...] = wup_f32[wslot].astype(BF16)
            wdn_bf[...] = wdn_f32[wslot].astype(BF16)

    has_next = i + 1 < nused
    for slot in range(2):
        mine = (i % 2) == slot

        @pl.when(mine & has_next)
        def _():
            idx_copy(i + 1, 1 - slot).wait()
            issue_rows(1 - slot)
            wait_rows(slot)
            compute(slot)

        @pl.when(mine & (i < nused) & jnp.logical_not(has_next))
        def _():
            wait_rows(slot)
            compute(slot)

    @pl.when(i >= nused)
    def _():
        o_ref[...] = jnp.zeros_like(o_ref)


def _experts(h3, blk_e, row_tok, nused, p_end, w_exp_up, w_exp_down):
    nblk = blk_e.shape[0]
    rb = ROW_BLOCK
    e, d, f2 = w_exp_up.shape
    blk = jnp.arange(nblk, dtype=I32)
    first = (blk == 0) | (blk_e != jnp.roll(blk_e, 1))
    parity = (jnp.cumsum(first.astype(I32)) - 1) % 2
    nxt_pos = p_end[blk_e] // rb
    nxt_e = blk_e[jnp.minimum(nxt_pos, nblk - 1)]
    sched = jnp.stack([first.astype(I32), parity, nxt_e, (nxt_pos < nused[0]).astype(I32)]).astype(I32)
    any_spec = pl.BlockSpec(memory_space=pl.ANY)
    grid_spec = pltpu.PrefetchScalarGridSpec(
        num_scalar_prefetch=3,
        grid=(nblk,),
        in_specs=[any_spec, any_spec, any_spec, any_spec],
        out_specs=pl.BlockSpec((rb, 8, 128), lambda i, be, nu, sc: (i, 0, 0)),
        scratch_shapes=[pltpu.VMEM((rb, 8, 128), F32), pltpu.VMEM((rb, 8, 128), F32), pltpu.SMEM((2, rb), I32),
                        pltpu.SemaphoreType.DMA((2,)), pltpu.SemaphoreType.DMA((2,)),
                        pltpu.VMEM((d, f2), BF16), pltpu.VMEM((f2 // 2, d), BF16),
                        pltpu.VMEM((2, d, f2), F32), pltpu.VMEM((2, f2 // 2, d), F32),
                        pltpu.SemaphoreType.DMA((2,))])
    return pl.pallas_call(
        _expert_kernel,
        grid_spec=grid_spec,
        out_shape=jax.ShapeDtypeStruct((nblk * rb, 8, 128), F32),
        compiler_params=_cparams("arbitrary"),
        name="experts",
    )(blk_e, nused, sched, row_tok.reshape(nblk, rb), h3, w_exp_up, w_exp_down)


def _combine_kernel(dest_hbm, y_hbm, wt_ref, h_ref, x1_ref, g2_ref, gpost_ref, wsu_ref, wsd_ref, o_ref,
                    buf, idx_smem, sem, isem, *, tm):
    i = pl.program_id(0)
    slot = _gather_rows(dest_hbm, idx_smem, isem, y_hbm, buf, sem, i, pl.num_programs(0), tm * TOPK_EXPERTS)
    wt = wt_ref[...]
    f = jnp.zeros((tm, h_ref.shape[1]), F32)
    for k in range(TOPK_EXPERTS):
        yk = jnp.concatenate([buf[slot, pl.ds(k, tm, stride=8), s, :] for s in range(8)], axis=1)
        f = f + yk * wt[:, k:k + 1]
    hb = h_ref[...].astype(BF16)
    up = jnp.dot(hb, wsu_ref[...], preferred_element_type=F32)
    half = up.shape[1] // 2
    act = (jax.nn.silu(up[:, :half]) * up[:, half:]).astype(BF16)
    f = f + jnp.dot(act, wsd_ref[...], preferred_element_type=F32)
    o_ref[...] = x1_ref[...] + g2_ref[...] * _rms(f, gpost_ref[...])


def _combine(dest, y, wt, h_all, x1_all, gate2_tab, n_prompt, g_ffn_post, w_sh_up_bf, w_sh_down_bf):
    n, d = h_all.shape
    tm = 128
    assert n_prompt % tm == 0
    const = lambda shape: pl.BlockSpec(shape, lambda i: (0,) * len(shape))
    row = lambda w: pl.BlockSpec((tm, w), lambda i: (i, 0))
    gate_spec = pl.BlockSpec((tm, d), lambda i: (jnp.maximum(i - (n_prompt // tm - 1), 0), 0))
    na = tm * TOPK_EXPERTS
    return pl.pallas_call(
        functools.partial(_combine_kernel, tm=tm),
        grid=(n // tm,),
        in_specs=[pl.BlockSpec(memory_space=pl.ANY), pl.BlockSpec(memory_space=pl.ANY), row(LANES), row(d), row(d),
                  gate_spec, const((1, d)), const(w_sh_up_bf.shape), const(w_sh_down_bf.shape)],
        out_specs=row(d),
        scratch_shapes=[pltpu.VMEM((2, na, 8, 128), F32), pltpu.SMEM((2, na), I32),
                        pltpu.SemaphoreType.DMA((2,)), pltpu.SemaphoreType.DMA((2,))],
        out_shape=jax.ShapeDtypeStruct((n, d), F32),
        compiler_params=_cparams("arbitrary"),
        name="moe_combine",
    )(dest.reshape(n // tm, na), y, wt, h_all, x1_all, gate2_tab, g_ffn_post.reshape(1, d),
      w_sh_up_bf, w_sh_down_bf)


def _dest_kernel(idx_ref, rank_ref, ps_ref, o_ref):
    tm = idx_ref.shape[0]
    idxf = idx_ref[...].astype(F32)
    ps = ps_ref[...].astype(F32)
    lane_g = lax.broadcasted_iota(I32, (tm, LANES), 1)
    lane = lax.broadcasted_iota(I32, (tm, N_EXPERTS), 1).astype(F32)
    base = jnp.zeros((tm, LANES), F32)
    for k in range(TOPK_EXPERTS):
        idx_k = jnp.sum(jnp.where(lane_g == k, idxf, 0.0), axis=1, keepdims=True)
        ps_k = jnp.sum(jnp.where(lane == idx_k, ps, 0.0), axis=1, keepdims=True)
        base = jnp.where(lane_g == k, ps_k, base)
    o_ref[...] = base.astype(I32) + rank_ref[...]


def _moe_dest(idx_p, rank_p, p_start):
    n = idx_p.shape[0]
    tm = 128
    row = pl.BlockSpec((tm, LANES), lambda i: (i, 0))
    return pl.pallas_call(
        _dest_kernel,
        grid=(n // tm,),
        in_specs=[row, row, pl.BlockSpec((1, N_EXPERTS), lambda i: (0, 0))],
        out_specs=row,
        out_shape=jax.ShapeDtypeStruct((n, LANES), I32),
        compiler_params=_cparams("arbitrary"),
        name="moe_dest",
    )(idx_p, rank_p, p_start.reshape(1, N_EXPERTS))


def _moe(h_all, x1_all, gate2_tab, n_prompt, n_valid, w_router, b_router, w_exp_up, w_exp_down, w_sh_up, w_sh_down,
         g_ffn_post):
    n, d = h_all.shape
    rb = ROW_BLOCK
    e = N_EXPERTS
    idx_p, wt_p, rank_p, cnt = _router(h_all, w_router, b_router, n_valid)
    counts = cnt[0].astype(I32)
    padded = (counts + rb - 1) // rb * rb
    p_end = jnp.cumsum(padded)
    p_start = p_end - padded
    dest = _moe_dest(idx_p, rank_p, p_start)[:, :TOPK_EXPERTS]
    nblk = -(-(n_valid * TOPK_EXPERTS + e * (rb - 1)) // rb)
    tok = jnp.repeat(jnp.arange(n_valid, dtype=I32), TOPK_EXPERTS)
    row_tok = jnp.zeros((nblk * rb,), I32).at[dest[:n_valid].reshape(-1)].set(tok)
    blk_start = jnp.arange(nblk, dtype=I32) * rb
    blk_e = jnp.minimum(jnp.sum((p_end[None, :] <= blk_start[:, None]).astype(I32), axis=1), e - 1)
    nused = (p_end[-1] // rb).astype(I32).reshape(1)
    y = _experts(h_all.reshape(n, 8, 128), blk_e, row_tok, nused, p_end.astype(I32), w_exp_up, w_exp_down)
    dest = jnp.where(jnp.arange(n, dtype=I32)[:, None] < n_valid, dest, 0)
    return _combine(dest, y, wt_p, h_all, x1_all, gate2_tab, n_prompt, g_ffn_post,
                    w_sh_up.astype(BF16), w_sh_down.astype(BF16))


def _kmean_pages_kernel(pt_ref, *refs):
    pages, o_ref = refs[:-1], refs[-1]
    s_idx = pl.program_id(1)
    per_step = len(pages) // 2

    @pl.when(s_idx == 0)
    def _():
        o_ref[...] = jnp.zeros_like(o_ref)

    lane = lax.broadcasted_iota(I32, (512, LANES), 1)
    acc = o_ref[0]
    for u in range(per_step):
        tot = jnp.sum(pages[2 * u][0] + pages[2 * u + 1][0], axis=1, keepdims=True)
        acc = jnp.where(lane == s_idx * per_step + u, tot * (1.0 / MOBA_BLOCK), acc)
    o_ref[0] = acc


def _kmean_pages(cache_a, page_table):
    bs, n_pages = page_table.shape
    pps = PAGES_PER_STEP
    assert n_pages // 2 <= LANES
    in_specs = [pl.BlockSpec((1, 512, PAGE_SIZE), lambda b, s, pt, u=u: (pt[b, s * pps + u], 0, 0)) for u in range(pps)]
    grid_spec = pltpu.PrefetchScalarGridSpec(
        num_scalar_prefetch=1, grid=(bs, n_pages // pps), in_specs=in_specs,
        out_specs=pl.BlockSpec((1, 512, LANES), lambda b, s, pt: (b, 0, 0)))
    return pl.pallas_call(
        _kmean_pages_kernel, grid_spec=grid_spec,
        out_shape=jax.ShapeDtypeStruct((bs, 512, LANES), F32),
        compiler_params=_cparams("arbitrary", "arbitrary"),
        name="kmean_pages",
    )(page_table, *([cache_a] * pps))


def _moba_sample_kernel(pt_ref, qbd_ref, km_ref, new_ref, *refs, nb, ts, past):
    pps = PAGES_PER_STEP
    pages = refs[:pps]
    o_ref, sel_scr, m_scr, l_scr, acc_scr = refs[pps:]
    s_idx = pl.program_id(1)
    rows = H_MOBA * ts
    qbd = qbd_ref[0]
    qb = qbd.astype(BF16)
    rowi = lax.broadcasted_iota(I32, (rows, 1), 0)
    slope = jnp.zeros((rows, 1), F32)
    for h in range(H_MOBA):
        slope = jnp.where(_div_pow2(rowi, ts) == h, SLOPES_A[h], slope)
    lane = lax.broadcasted_iota(I32, (rows, LANES), 1)

    @pl.when(s_idx == 0)
    def _():
        gate = _dot3(qbd, km_ref[0])
        gate = jnp.where(lane < nb, gate, NEG_INF)
        sel = jnp.zeros((rows, LANES), F32)
        for first, m in _topk_axis1(gate, MOBA_TOPK, lane):
            sel = jnp.where((lane == first) & (m > NEG_INF), 1.0, sel)
        sel_scr[...] = sel
        m_scr[...] = jnp.full_like(m_scr, M_INIT)
        l_scr[...] = jnp.zeros_like(l_scr)
        acc_scr[...] = jnp.zeros_like(acc_scr)

    def update(ss, vTs):
        m = m_scr[...]
        m_new = m
        for s in ss:
            m_new = jnp.maximum(m_new, jnp.max(s, axis=1, keepdims=True))
        alpha = jnp.exp(m - m_new)
        lsum = jnp.zeros_like(m)
        pv = jnp.zeros(acc_scr.shape, F32)
        for s, vT in zip(ss, vTs):
            p = jnp.exp(s - m_new)
            lsum = lsum + jnp.sum(p, axis=1, keepdims=True)
            pv = pv + lax.dot_general(p.astype(BF16), vT(), _NT, preferred_element_type=F32)
        l_scr[...] = l_scr[...] * alpha + lsum
        acc_scr[...] = acc_scr[...] * alpha + pv
        m_scr[...] = m_new

    sel = sel_scr[...]
    ss, vTs = [], []
    for u in range(pps):
        page = s_idx * pps + u
        flag = jnp.sum(jnp.where(lane == page // 2, sel, 0.0), axis=1, keepdims=True) > 0.0
        kT = pages[u][0, 0:512, :].astype(BF16)
        s = jnp.dot(qb, kT, preferred_element_type=F32)
        kpos = page * PAGE_SIZE + lane - past
        ss.append(jnp.where(flag, s + slope * kpos.astype(F32), NEG_INF))
        vTs.append(lambda u=u: pages[u][0, 512:1024, :].astype(BF16))
    update(ss, vTs)

    @pl.when(s_idx == pl.num_programs(1) - 1)
    def _():
        kn = new_ref[0, 0:512, :].astype(BF16)
        s = jnp.dot(qb, kn, preferred_element_type=F32)
        ok = (lane < ts) & (lane <= _mod_pow2(rowi, ts))
        s = jnp.where(ok, s + slope * lane.astype(F32), NEG_INF)
        update([s], [lambda: new_ref[0, 512:1024, :].astype(BF16)])
        o = acc_scr[...] / jnp.maximum(l_scr[...], 1e-30)
        col_h = _div_pow2(lax.broadcasted_iota(I32, (rows, 512), 1), D_HEAD)
        o = jnp.where(col_h == _div_pow2(rowi, ts), o, 0.0)
        pick_r = lax.broadcasted_iota(I32, (8, rows), 0)
        pick_c = lax.broadcasted_iota(I32, (8, rows), 1)
        gather = jnp.where(_mod_pow2(pick_c, ts) == pick_r, 1.0, 0.0)
        o_ref[0] = _dot3(gather, o)


def _moba_sample(qbd, km_pad, new_pad, cache_a, page_table, *, nb, ts, past):
    bs, n_pages = page_table.shape
    pps = PAGES_PER_STEP
    rows = H_MOBA * ts
    in_specs = [pl.BlockSpec((1, rows, 512), lambda b, s, pt: (b, 0, 0)),
                pl.BlockSpec((1, 512, LANES), lambda b, s, pt: (b, 0, 0)),
                pl.BlockSpec((1, 1024, LANES), lambda b, s, pt: (b, 0, 0))]
    in_specs += [pl.BlockSpec((1, 1024, PAGE_SIZE), lambda b, s, pt, u=u: (pt[b, s * pps + u], 0, 0)) for u in range(pps)]
    grid_spec = pltpu.PrefetchScalarGridSpec(
        num_scalar_prefetch=1, grid=(bs, n_pages // pps), in_specs=in_specs,
        out_specs=pl.BlockSpec((1, 8, 512), lambda b, s, pt: (b, 0, 0)),
        scratch_shapes=[pltpu.VMEM((rows, LANES), F32), pltpu.VMEM((rows, 1), F32), pltpu.VMEM((rows, 1), F32),
                        pltpu.VMEM((rows, 512), F32)])
    return pl.pallas_call(
        functools.partial(_moba_sample_kernel, nb=nb, ts=ts, past=past),
        grid_spec=grid_spec,
        out_shape=jax.ShapeDtypeStruct((bs, 8, 512), F32),
        compiler_params=_cparams("arbitrary", "arbitrary"),
        name="moba_sample",
    )(page_table, qbd, km_pad, new_pad, *([cache_a] * pps))


def _nsa_sample_kernel(pt_ref, idx_ref, qs_ref, qw_ref, kpos_ref, new_ref, win_ref, wnew_ref, cache_hbm,
                       os_ref, ow_ref, buf, sem, *, ts, past, n_slots):
    b = pl.program_id(0)
    ngq = G_NSA * ts
    slot = b % 2

    def page_copies(bb, sl):
        cps = []
        for gq in range(ngq):
            for t in range(n_slots):
                blk = jnp.maximum(idx_ref[(bb * ngq + gq) * SLC_TOPK + t], 0)
                page = pt_ref[bb, blk // 2]
                cps.append(pltpu.make_async_copy(cache_hbm.at[page, pl.ds(256, 256), :],
                                                 buf.at[sl, gq, :, pl.ds(t * PAGE_SIZE, PAGE_SIZE)], sem.at[sl]))
        return cps

    @pl.when(b == 0)
    def _():
        for cp in page_copies(0, 0):
            cp.start()

    @pl.when(b + 1 < pl.num_programs(0))
    def _():
        for cp in page_copies(b + 1, 1 - slot):
            cp.start()

    new = new_ref[0]
    for gq in range(ngq):
        buf[slot, gq, :, pl.ds(n_slots * PAGE_SIZE, PAGE_SIZE)] = new
    for cp in page_copies(b, slot):
        cp.wait()

    rowi = lax.broadcasted_iota(I32, (8, 1), 0)
    for gq in range(ngq):
        g, q = divmod(gq, ts)
        slope = jnp.zeros((8, 1), F32)
        for r in range(R_NSA):
            slope = jnp.where(rowi == r, SLOPES_B[g * R_NSA + r], slope)
        kT = buf[slot, gq, 0:128, :].astype(BF16)
        vT = buf[slot, gq, 128:256, :].astype(BF16)
        s = jnp.dot(qs_ref[0, gq].astype(BF16), kT, preferred_element_type=F32)
        kpos = kpos_ref[0, gq]
        ok = (kpos >= 0) & (kpos <= past + q)
        s = jnp.where(ok, s + slope * (kpos - past).astype(F32), NEG_INF)
        m = jnp.max(s, axis=1, keepdims=True)
        m = jnp.where(m > NEG_INF, m, 0.0)
        p = jnp.exp(s - m)
        p = p * (1.0 / jnp.maximum(jnp.sum(p, axis=1, keepdims=True), 1e-30))
        os_ref[0, gq] = lax.dot_general(p.astype(BF16), vT, _NT, preferred_element_type=F32)

    nw = WINDOW + LANES
    kw_all = jnp.concatenate([win_ref[0], wnew_ref[0]], axis=1)
    kw = kw_all[0:128, :].astype(BF16)
    vw = kw_all[128:256, :].astype(BF16)
    wrow = lax.broadcasted_iota(I32, (R_NSA * ts, 1), 0)
    wlane = lax.broadcasted_iota(I32, (R_NSA * ts, nw), 1)
    dist = (WINDOW + _mod_pow2(wrow, ts)) - wlane
    okw = (dist >= 0) & (dist <= WINDOW) & (wlane < WINDOW + ts)
    for g in range(G_NSA):
        slope = jnp.zeros((R_NSA * ts, 1), F32)
        for r in range(R_NSA):
            slope = jnp.where(_div_pow2(wrow, ts) == r, SLOPES_B[g * R_NSA + r], slope)
        s = jnp.dot(qw_ref[0, g].astype(BF16), kw, preferred_element_type=F32)
        s = jnp.where(okw, s - slope * dist.astype(F32), NEG_INF)
        m = jnp.max(s, axis=1, keepdims=True)
        m = jnp.where(m > NEG_INF, m, 0.0)
        p = jnp.exp(s - m)
        p = p * (1.0 / jnp.maximum(jnp.sum(p, axis=1, keepdims=True), 1e-30))
        ow_ref[0, g] = lax.dot_general(p.astype(BF16), vw, _NT, preferred_element_type=F32)


def _nsa_sample(idx_flat, qs, qw, kpos, new_pad, win, wnew_pad, cache_b, page_table, *, ts, past, n_slots):
    bs = page_table.shape[0]
    ngq = G_NSA * ts
    nk = (n_slots + 1) * PAGE_SIZE
    m4 = lambda shape: pl.BlockSpec(shape, lambda b, pt, ix: (b,) + (0,) * (len(shape) - 1))
    grid_spec = pltpu.PrefetchScalarGridSpec(
        num_scalar_prefetch=2, grid=(bs,),
        in_specs=[m4((1, ngq, 8, 128)), m4((1, G_NSA, R_NSA * ts, 128)), m4((1, ngq, 1, nk)),
                  m4((1, 256, LANES)), m4((1, 256, WINDOW)), m4((1, 256, LANES)),
                  pl.BlockSpec(memory_space=pl.ANY)],
        out_specs=[m4((1, ngq, 8, 128)), m4((1, G_NSA, R_NSA * ts, 128))],
        scratch_shapes=[pltpu.VMEM((2, ngq, 256, nk), F32), pltpu.SemaphoreType.DMA((2,))])
    return pl.pallas_call(
        functools.partial(_nsa_sample_kernel, ts=ts, past=past, n_slots=n_slots),
        grid_spec=grid_spec,
        out_shape=[jax.ShapeDtypeStruct((bs, ngq, 8, 128), F32),
                   jax.ShapeDtypeStruct((bs, G_NSA, R_NSA * ts, 128), F32)],
        compiler_params=_cparams("arbitrary"),
        name="nsa_sample",
    )(page_table, idx_flat, qs, qw, kpos, new_pad, win, wnew_pad, cache_b)


def _vT_blocks(v, tk):
    t, c = v.shape
    return v.astype(BF16).reshape(t // tk, tk, c // 128, 128).transpose(2, 0, 3, 1)


def _prompt_mixer(proj, cw):
    qa, kva, qb, kvb, kvw, _ = proj
    t = qa.shape[0]
    qaT = (qa * Q_SCALE).T
    qbT = (qb * Q_SCALE).T
    kmean = _kmean_prompt(kva)
    sel_a = _moba_gate_prompt(qaT, kmean)
    oaT = _flash_select(qaT, kva[:, :512].astype(BF16), _vT_blocks(kva[:, 512:], TKF), sel_a, HEADS_MOBA,
                        block=MOBA_BLOCK)
    cmp_out = _compress(kvb, None, cw, n_rows=t, n_batch=1)
    kc = cmp_out[:, :, 0:128].astype(BF16)
    vcT = cmp_out[:, :, 128:256].astype(BF16).transpose(0, 2, 1)
    pos = jnp.arange(t, dtype=I32).reshape(1, t)
    ocT, sel_b, _ = _cmp_attn(qbT, pos, kc, vcT, t // SLC_BLOCK, k_sel=SLC_TOPK, tq=TQ, tiles_per_batch=0)
    osT = _flash_select(qbT, kvb[:, 256:384].astype(BF16), _vT_blocks(kvb[:, 384:512], TKF), sel_b, HEADS_NSA,
                        block=SLC_BLOCK)
    owT = _flash_select(qbT, kvw[:, 0:128].astype(BF16), _vT_blocks(kvw[:, 128:256], TKF), None, HEADS_NSA,
                        block=TKF, window=WINDOW)
    return oaT.T, ocT.T, osT.T, owT.T


def _sample_mixer(proj, cache_a, cache_b, win_state, page_table, cw, bs, ts):
    qa, kva, qb, kvb, kvw, _ = proj
    n_pages = page_table.shape[1]
    past = n_pages * PAGE_SIZE
    nb = past // MOBA_BLOCK
    new_rows_T = lambda a: jnp.zeros((bs, LANES, a.shape[-1]), F32).at[:, :ts].set(a).transpose(0, 2, 1)
    km = _kmean_pages(cache_a, page_table)
    q4 = (qa * Q_SCALE).reshape(bs, ts, H_MOBA, D_HEAD).transpose(0, 2, 1, 3)
    eye = jnp.eye(H_MOBA, dtype=F32)
    qbd = (q4[:, :, :, None, :] * eye[None, :, None, :, None]).reshape(bs, H_MOBA * ts, 512)
    new_a = new_rows_T(kva.reshape(bs, ts, 1024))
    oa = _moba_sample(qbd, km, new_a, cache_a, page_table, nb=nb, ts=ts, past=past)[:, :ts]
    oa = oa.reshape(bs * ts, 512)
    cmp_out = _compress(cache_b, page_table, cw, n_rows=past, n_batch=bs)
    kc = cmp_out[:, :, 0:128].astype(BF16)
    vcT = cmp_out[:, :, 128:256].astype(BF16).transpose(0, 2, 1)
    qbs = (qb * Q_SCALE).reshape(bs, ts, 512)
    qT = jnp.zeros((bs, LANES, 512), F32).at[:, :ts].set(qbs).reshape(bs * LANES, 512).T
    pos = jnp.broadcast_to(past + jnp.minimum(jnp.arange(LANES, dtype=I32), ts - 1), (bs, LANES)).reshape(1, bs * LANES)
    n_slots = SLC_TOPK - 1
    ocT, _, idx = _cmp_attn(qT, pos, kc, vcT, past // SLC_BLOCK, k_sel=n_slots, tq=LANES, tiles_per_batch=1)
    oc = ocT.T.reshape(bs, LANES, 512)[:, :ts].reshape(bs * ts, 512)
    idx = idx.reshape(G_NSA, SLC_TOPK, bs, LANES)[:, :, :, :ts].transpose(2, 0, 3, 1)
    q5 = qbs.reshape(bs, ts, G_NSA, R_NSA, D_HEAD)
    lane_g = jnp.eye(G_NSA, dtype=F32)
    qsel = q5.transpose(0, 2, 1, 3, 4)[:, :, :, :, None, :] * lane_g[None, :, None, None, :, None]
    qs = jnp.zeros((bs, G_NSA, ts, 8, 128), F32).at[:, :, :, :R_NSA].set(qsel.reshape(bs, G_NSA, ts, R_NSA, 128))
    qs = qs.reshape(bs, G_NSA * ts, 8, 128)
    qwin = q5.transpose(0, 2, 3, 1, 4)[:, :, :, :, None, :] * lane_g[None, :, None, None, :, None]
    qw = qwin.reshape(bs, G_NSA, R_NSA * ts, 128)
    sl = idx[..., :n_slots, None]
    lane = jnp.arange(PAGE_SIZE, dtype=I32)
    in_blk = (sl >= 0) & ((lane // SLC_BLOCK) == (sl % 2))
    slot_pos = jnp.where(in_blk, (sl // 2) * PAGE_SIZE + lane, -1).reshape(bs, G_NSA, ts, n_slots * PAGE_SIZE)
    own = jnp.where(lane < ts, past + lane, -1)
    kpos = jnp.concatenate([slot_pos, jnp.broadcast_to(own, (bs, G_NSA, ts, PAGE_SIZE))], axis=-1)
    kpos = kpos.reshape(bs, G_NSA * ts, 1, (n_slots + 1) * PAGE_SIZE).astype(I32)
    new_b = new_rows_T(kvb.reshape(bs, ts, 512)[:, :, 256:])
    wnew = new_rows_T(kvw.reshape(bs, ts, 256))
    os_raw, ow_raw = _nsa_sample(idx.reshape(-1).astype(I32), qs, qw, kpos, new_b, win_state, wnew, cache_b,
                                 page_table, ts=ts, past=past, n_slots=n_slots)
    os5 = os_raw.reshape(bs, G_NSA, ts, 8, G_NSA, D_HEAD)[:, :, :, :R_NSA]
    os_ = jnp.stack([os5[:, g, :, :, g] for g in range(G_NSA)], axis=2).reshape(bs * ts, 512)
    ow5 = ow_raw.reshape(bs, G_NSA, R_NSA, ts, G_NSA, D_HEAD)
    ow = jnp.stack([ow5[:, g, :, :, g] for g in range(G_NSA)], axis=1)
    ow = ow.transpose(0, 3, 1, 2, 4).reshape(bs * ts, 512)
    return oa, oc, os_, ow


def _layer(x_p, x_s, cache_a, cache_b, win_state, page_table, c_p, c_s, w):
    (w_ada, b_ada, g_mix_pre, g_mix_post, g_ffn_pre, g_ffn_post, w_in, g_out_moba, g_out_nsa, w_out,
     cmp_pe, cmp_w1, cmp_b1, cmp_w2, cmp_b2, w_router, b_router, w_exp_up, w_exp_down, w_sh_up, w_sh_down) = w
    t, d = x_p.shape
    bs, ts, _ = x_s.shape
    assert c_p.shape[0] == 1 and t % (8 * MOBA_BLOCK) == 0 and win_state.shape[1] == WINDOW
    n_s = bs * ts
    rows = -(-(1 + bs) // 8) * 8
    c_all = jnp.zeros((rows, d), F32).at[0:1].set(c_p).at[1:1 + bs].set(c_s)
    mod = _ada(c_all, w_ada, b_ada).reshape(rows, 6, d)
    mod_p = [mod[0:1, i] for i in range(6)]
    mod_s = [jnp.repeat(mod[1:1 + bs, i], ts, axis=0) for i in range(6)]

    w_pad = jnp.zeros((d, _PROJ_CUTS[-1]), F32).at[:, :w_in.shape[1]].set(w_in).astype(BF16)
    cw = _compress_weights(cmp_pe, cmp_w1, cmp_b1, cmp_w2, cmp_b2)
    w_out_bf = w_out.astype(BF16)

    proj_p = _inproj(x_p, mod_p[1], mod_p[0], g_mix_pre, w_pad)
    proj_s = _inproj(x_s.reshape(n_s, d), mod_s[1], mod_s[0], g_mix_pre, w_pad)

    o_p = _prompt_mixer(proj_p, cw)
    win_t = win_state.transpose(0, 2, 3, 1).reshape(bs, 2 * G_NSA * D_HEAD, WINDOW)
    o_s = _sample_mixer(proj_s, cache_a, cache_b, win_t, page_table, cw, bs, ts)

    x1_p, h2_p = _post(x_p, *o_p, proj_p[5], mod_p[2], mod_p[4], mod_p[3], g_out_moba, g_out_nsa, g_mix_post,
                       g_ffn_pre, w_out_bf)
    x1_s, h2_s = _post(x_s.reshape(n_s, d), *o_s, proj_s[5], mod_s[2], mod_s[4], mod_s[3], g_out_moba, g_out_nsa,
                       g_mix_post, g_ffn_pre, w_out_bf)

    n_valid = t + n_s
    n_all = -(-n_valid // ROUTER_TILE) * ROUTER_TILE
    pad = lambda a: jnp.concatenate([a, jnp.zeros((n_all - n_valid, d), F32)], axis=0) if n_all > n_valid else a
    h_all = pad(jnp.concatenate([h2_p, h2_s], axis=0))
    x1_all = pad(jnp.concatenate([x1_p, x1_s], axis=0))
    g2_tab = jnp.concatenate([jnp.broadcast_to(mod_p[5], (128, d)), mod_s[5],
                              jnp.zeros((n_all - n_valid, d), F32)], axis=0)
    y_all = _moe(h_all, x1_all, g2_tab, t, n_valid, w_router, b_router, w_exp_up, w_exp_down, w_sh_up, w_sh_down,
                 g_ffn_post)
    y_p = y_all[:t]
    y_s = y_all[t:n_valid].reshape(bs, ts, d)

    _, kva_p, _, kvb_p, kvw_p, _ = proj_p
    _, kva_s, _, kvb_s, kvw_s, _ = proj_s
    state_p = (kva_p.reshape(1, t, 2 * H_MOBA, D_HEAD), kvb_p.reshape(1, t, 4 * G_NSA, D_HEAD),
               kvw_p[t - min(WINDOW, t):].reshape(1, min(WINDOW, t), 2 * G_NSA, D_HEAD))
    win_new = jnp.concatenate([win_state, kvw_s.reshape(bs, ts, 2 * G_NSA, D_HEAD)], axis=1)[:, -WINDOW:]
    state_s = (kva_s.reshape(bs, ts, 2 * H_MOBA, D_HEAD), kvb_s.reshape(bs, ts, 4 * G_NSA, D_HEAD), win_new)
    return y_p, y_s, state_p, state_s


def kernel(x_prompt, x_sample, cache_moba, cache_nsa, state_nsa_win, page_table, c_prompt, c_sample, w_ada, b_ada, g_mix_pre, g_mix_post, g_ffn_pre, g_ffn_post, w_in, g_out_moba, g_out_nsa, w_out, cmp_pe, cmp_w1, cmp_b1, cmp_w2, cmp_b2, w_router, b_router, w_exp_up, w_exp_down, w_sh_up, w_sh_down):
    weights = (w_ada, b_ada, g_mix_pre, g_mix_post, g_ffn_pre, g_ffn_post, w_in, g_out_moba, g_out_nsa, w_out,
               cmp_pe, cmp_w1, cmp_b1, cmp_w2, cmp_b2, w_router, b_router, w_exp_up, w_exp_down, w_sh_up, w_sh_down)
    depth = w_ada.shape[0]
    n_pool = cache_moba.shape[1]
    y_p, y_s = x_prompt[0], x_sample
    st_p, st_s = [], []
    for layer in range(depth):
        w_l = tuple(w[layer] for w in weights)
        cache_a = cache_moba[layer].transpose(0, 2, 3, 1).reshape(n_pool, 2 * H_MOBA * D_HEAD, PAGE_SIZE)
        cache_b = cache_nsa[layer].transpose(0, 2, 3, 1).reshape(n_pool, 4 * G_NSA * D_HEAD, PAGE_SIZE)
        y_p, y_s, sp, ss = _layer(y_p, y_s, cache_a, cache_b, state_nsa_win[layer], page_table, c_prompt, c_sample, w_l)
        st_p.append(sp)
        st_s.append(ss)
    stack = lambda sts, i: jnp.stack([s[i] for s in sts])
    return (y_p[None], y_s, stack(st_p, 0), stack(st_p, 1), stack(st_p, 2), stack(st_s, 0), stack(st_s, 1), stack(st_s, 2))
```

```python
import functools

import numpy as np
import jax
import jax.numpy as jnp
from jax import lax
from jax.experimental import pallas as pl
from jax.experimental.pallas import tpu as pltpu

F32, BF16, I32 = jnp.float32, jnp.bfloat16, jnp.int32
NEG_INF = float("-inf")
POS_INF = float("inf")
M_INIT = -1e30

D_HEAD = 64
H_MOBA = 8
H_NSA = 8
G_NSA = 2
R_NSA = H_NSA // G_NSA
MOBA_BLOCK = 256
MOBA_TOPK = 3
CMP_STRIDE = 16
CMP_LEN = 2 * CMP_STRIDE
CMP_HIDDEN = 2 * D_HEAD
SLC_BLOCK = 64
SLC_TOPK = 16
SLC_INIT = 1
SLC_LOCAL = 2
WINDOW = 512
N_EXPERTS = 256
TOPK_EXPERTS = 8
N_EXPERT_GROUPS = 8
TOPK_GROUPS = 4
ROUTED_SCALE = 2.5
PAGE_SIZE = 128
EPS = 1e-6
Q_SCALE = D_HEAD ** -0.5
LOG2E = 1.4426950408889634

_SLOPES = [2.0 ** (-8.0 * i / (H_MOBA + H_NSA)) for i in range(1, H_MOBA + H_NSA + 1)]
SLOPES_A = _SLOPES[0::2][:H_MOBA]
SLOPES_B = _SLOPES[1::2][:H_NSA]

LANES = 128
TQ = 256
TKF = 128
VMEM_LIMIT_BYTES = 56 * 1024 * 1024
ROW_BLOCK = 128
ROUTER_TILE = 256
PAGES_PER_STEP = 16


def _cparams(*sem):
    return pltpu.CompilerParams(dimension_semantics=sem, vmem_limit_bytes=VMEM_LIMIT_BYTES)


def _rms(x, g):
    return x * lax.rsqrt(jnp.mean(x * x, axis=-1, keepdims=True) + EPS) * g


def _split_bf16(a):
    hi = a.astype(BF16)
    lo = (a - hi.astype(F32)).astype(BF16)
    return hi, lo


def _dot3(a, b, dims=None):
    ah, al = _split_bf16(a)
    bh, bl = _split_bf16(b)
    if dims is None:
        d = lambda x, y: jnp.dot(x, y, preferred_element_type=F32)
    else:
        d = lambda x, y: lax.dot_general(x, y, dims, preferred_element_type=F32)
    return d(ah, bh) + (d(ah, bl) + d(al, bh))


_NT = (((1,), (1,)), ((), ()))


def _div_pow2(x, n):
    assert n & (n - 1) == 0
    return jnp.right_shift(x, n.bit_length() - 1)


def _mod_pow2(x, n):
    assert n & (n - 1) == 0
    return jnp.bitwise_and(x, n - 1)


def _head_rows(q64, half):
    z = jnp.zeros_like(q64)
    return jnp.concatenate([z, q64] if half else [q64, z], axis=0)


def _ada_kernel(c_ref, w_ref, b_ref, o_ref):
    a = jax.nn.silu(c_ref[...])
    o_ref[...] = _dot3(a, w_ref[...]) + b_ref[...]


def _ada(c, w_ada, b_ada):
    r, d = c.shape
    n = w_ada.shape[1]
    tn = 768
    return pl.pallas_call(
        _ada_kernel,
        grid=(n // tn,),
        in_specs=[pl.BlockSpec((r, d), lambda j: (0, 0)),
                  pl.BlockSpec((d, tn), lambda j: (0, j)),
                  pl.BlockSpec((1, tn), lambda j: (0, j))],
        out_specs=pl.BlockSpec((r, tn), lambda j: (0, j)),
        out_shape=jax.ShapeDtypeStruct((r, n), F32),
        compiler_params=_cparams("arbitrary"),
        name="ada",
    )(c, w_ada, b_ada.reshape(1, n))


_PROJ_CUTS = (0, 512, 1536, 2048, 2560, 2816, 2944)


def _inproj_kernel(x_ref, sc_ref, sh_ref, g_ref, w_ref, *out_refs):
    h = _rms(x_ref[...], g_ref[...]) * (1.0 + sc_ref[...]) + sh_ref[...]
    hb = h.astype(BF16)
    for o_ref, a, b in zip(out_refs, _PROJ_CUTS[:-1], _PROJ_CUTS[1:]):
        o_ref[...] = jnp.dot(hb, w_ref[:, a:b], preferred_element_type=F32)


def _inproj(x, scale, shift, g, w_pad):
    r, d = x.shape
    tm = min(512, r)
    per_row = scale.shape[0] != 1
    mod_spec = pl.BlockSpec((tm, d), lambda i: (i, 0)) if per_row else pl.BlockSpec((1, d), lambda i: (0, 0))
    widths = [b - a for a, b in zip(_PROJ_CUTS[:-1], _PROJ_CUTS[1:])]
    return pl.pallas_call(
        _inproj_kernel,
        grid=(r // tm,),
        in_specs=[pl.BlockSpec((tm, d), lambda i: (i, 0)), mod_spec, mod_spec,
                  pl.BlockSpec((1, d), lambda i: (0, 0)),
                  pl.BlockSpec(w_pad.shape, lambda i: (0, 0))],
        out_specs=[pl.BlockSpec((tm, w), lambda i: (i, 0)) for w in widths],
        out_shape=[jax.ShapeDtypeStruct((r, w), F32) for w in widths],
        compiler_params=_cparams("arbitrary"),
        name="inproj",
    )(x, scale, shift, g.reshape(1, d), w_pad)


def _topk_axis0(score, k, idx):
    n = score.shape[0]
    sel = jnp.zeros(score.shape, F32)
    picks = []
    for _ in range(k):
        m = jnp.max(score, axis=0, keepdims=True)
        first = jnp.min(jnp.where(score == m, idx, n), axis=0, keepdims=True)
        pick = idx == first
        ok = m > NEG_INF
        sel = jnp.where(pick & ok, 1.0, sel)
        picks.append(jnp.where(ok, first, -1))
        score = jnp.where(pick, NEG_INF, score)
    return sel, picks


def _topk_axis1(score, k, idx):
    n = score.shape[1]
    out = []
    for _ in range(k):
        m = jnp.max(score, axis=1, keepdims=True)
        first = jnp.min(jnp.where(score == m, idx, n), axis=1, keepdims=True)
        out.append((first, m))
        score = jnp.where(idx == first, NEG_INF, score)
    return out


def _kmean_kernel(k_ref, o_ref):
    x = k_ref[...]
    n = x.shape[0] // MOBA_BLOCK
    o_ref[...] = jnp.sum(x.reshape(n, MOBA_BLOCK, x.shape[1]), axis=1) * (1.0 / MOBA_BLOCK)


def _kmean_prompt(kva):
    t = kva.shape[0]
    nb = t // MOBA_BLOCK
    per = 8
    return pl.pallas_call(
        _kmean_kernel,
        grid=(nb // per,),
        in_specs=[pl.BlockSpec((per * MOBA_BLOCK, 512), lambda i: (i, 0))],
        out_specs=pl.BlockSpec((per, 512), lambda i: (i, 0)),
        out_shape=jax.ShapeDtypeStruct((nb, 512), F32),
        compiler_params=_cparams("arbitrary"),
        name="kmean_prompt",
    )(kva)


def _moba_gate_kernel(qT_ref, km_ref, sel_ref, *, nb):
    own = pl.program_id(0)
    blk = lax.broadcasted_iota(I32, (nb, TQ), 0)
    for h in range(H_MOBA):
        p, half = divmod(h, 2)
        qh = _head_rows(qT_ref[h * 64:(h + 1) * 64, :], half)
        g = _dot3(km_ref[:, p * 128:(p + 1) * 128], qh)
        g = jnp.where(blk < own, g, NEG_INF)
        sel, _ = _topk_axis0(g, MOBA_TOPK, blk)
        sel_ref[h] = jnp.where(blk == own, 1.0, sel)


def _moba_gate_prompt(qT, kmean):
    t = qT.shape[1]
    nb = kmean.shape[0]
    return pl.pallas_call(
        functools.partial(_moba_gate_kernel, nb=nb),
        grid=(t // TQ,),
        in_specs=[pl.BlockSpec((512, TQ), lambda i: (0, i)),
                  pl.BlockSpec((nb, 512), lambda i: (0, 0))],
        out_specs=pl.BlockSpec((H_MOBA, nb, TQ), lambda i: (0, 0, i)),
        out_shape=jax.ShapeDtypeStruct((H_MOBA, nb, t), F32),
        compiler_params=_cparams("arbitrary"),
        name="moba_gate",
    )(qT, kmean)


def _flash_kernel(qT_ref, k_ref, vT_ref, *rest, heads, sub, sel_div, window):
    nh = len(heads)
    if window is None:
        sel_ref, o_ref, qh_scr, sc_scr, *state = rest
    else:
        sel_ref, (o_ref, qh_scr, sc_scr, *state) = None, rest
    m_scr, l_scr, acc_scr = state[:nh], state[nh:2 * nh], state[2 * nh:]
    i = pl.program_id(0)
    kio = lax.broadcasted_iota(I32, (TKF, TQ), 0)
    qio = lax.broadcasted_iota(I32, (TKF, TQ), 1)
    kiof = kio.astype(F32)
    rows = TKF // sub
    for h, (pair, half, set_idx, slope) in enumerate(heads):
        slope = slope * LOG2E
        qh_scr[h] = _head_rows(qT_ref[h * 64:(h + 1) * 64, :] * LOG2E, half).astype(BF16)
        sc_scr[h] = kiof * slope
        m_scr[h][...] = jnp.full((1, TQ), M_INIT, F32)
        l_scr[h][...] = jnp.zeros((1, TQ), F32)
        acc_scr[h][...] = jnp.zeros((64, TQ), F32)

    def sel_rows(set_idx, j):
        if sub == 1:
            return sel_ref[set_idx, pl.ds(j // sel_div, 1), :] > 0.0
        per8 = 8 // sub
        blk8 = sel_ref[set_idx, pl.ds(pl.multiple_of((j // per8) * 8, 8), 8), :]
        s = blk8[0:sub]
        for u in range(1, per8):
            s = jnp.where(j % per8 == u, blk8[u * sub:(u + 1) * sub], s)
        return s > 0.0

    def step(j, diag=None, band=None):
        off_base = (j * TKF - i * TQ).astype(F32)
        jc = j if band is None else jnp.maximum(j, 0)
        row0 = pl.multiple_of(jc * TKF, TKF)
        if band is not None:
            dist = qio - kio - band * TKF
            in_band = (dist >= 0) & (dist <= window) & (j >= 0)
        sel_cache = {}
        for h, (pair, half, set_idx, slope) in enumerate(heads):
            kj = k_ref[pl.ds(row0, TKF), pair * 128:(pair + 1) * 128]
            t = jnp.dot(kj, qh_scr[h], preferred_element_type=F32) + sc_scr[h]
            if diag is not None:
                t = jnp.where(kio + diag * TKF <= qio, t, NEG_INF)
            if band is not None:
                t = jnp.where(in_band, t, NEG_INF)
            off = (slope * LOG2E) * off_base
            t3 = t.reshape(sub, rows, TQ)
            m = m_scr[h][...]
            if sel_ref is None:
                m_new = jnp.maximum(m, jnp.max(jnp.max(t3, axis=1), axis=0, keepdims=True) + off)
                mu = jnp.broadcast_to(m_new - off, (sub, TQ))
            else:
                if set_idx not in sel_cache:
                    sel_cache[set_idx] = sel_rows(set_idx, j)
                selj = sel_cache[set_idx]
                mb = jnp.where(selj, jnp.max(t3, axis=1) + off, M_INIT)
                m_new = jnp.maximum(m, jnp.max(mb, axis=0, keepdims=True))
                mu = jnp.where(selj, m_new - off, POS_INF)
            p = jnp.exp2(t3 - mu[:, None, :]).reshape(TKF, TQ)
            alpha = jnp.exp2(m - m_new)
            l_scr[h][...] = l_scr[h][...] * alpha + jnp.sum(p, axis=0, keepdims=True)
            pv = jnp.dot(vT_ref[pair, jc], p.astype(BF16), preferred_element_type=F32)
            acc_scr[h][...] = acc_scr[h][...] * alpha + pv[half * 64:(half + 1) * 64]
            m_scr[h][...] = m_new

    per_q = TQ // TKF
    if window is None:
        def body(jj, c):
            for u in range(per_q):
                step(jj * per_q + u)
            return c

        lax.fori_loop(0, i, body, 0)
        for u in range(per_q):
            step(i * per_q + u, diag=u)
    else:
        for dj in range(-(window // TKF), per_q):
            step(i * per_q + dj, band=dj)
    for h in range(len(heads)):
        o_ref[h * 64:(h + 1) * 64, :] = acc_scr[h][...] / jnp.maximum(l_scr[h][...], 1e-30)


def _flash_select(qT, k, vTb, sel, heads, *, block, window=None):
    t = qT.shape[1]
    nh = len(heads)
    sub, sel_div = max(TKF // block, 1), max(block // TKF, 1)
    resident = lambda shape: pl.BlockSpec(shape, lambda i: (0,) * len(shape), pipeline_mode=pl.Buffered(1))
    in_specs = [pl.BlockSpec((nh * 64, TQ), lambda i: (0, i)), resident(k.shape), resident(vTb.shape)]
    args = (qT, k, vTb)
    if window is None:
        in_specs.append(pl.BlockSpec((sel.shape[0], sel.shape[1], TQ), lambda i: (0, 0, i)))
        args += (sel,)
    return pl.pallas_call(
        functools.partial(_flash_kernel, heads=heads, sub=sub, sel_div=sel_div, window=window),
        grid=(t // TQ,),
        in_specs=in_specs,
        out_specs=pl.BlockSpec((nh * 64, TQ), lambda i: (0, i)),
        out_shape=jax.ShapeDtypeStruct(qT.shape, F32),
        scratch_shapes=([pltpu.VMEM((nh, 128, TQ), BF16), pltpu.VMEM((nh, TKF, TQ), F32)]
                        + [pltpu.VMEM((1, TQ), F32)] * (2 * nh) + [pltpu.VMEM((64, TQ), F32)] * nh),
        compiler_params=_cparams("arbitrary"),
        name="flash_select" if window is None else "flash_window",
    )(*args)


HEADS_MOBA = tuple((h // 2, h % 2, h, SLOPES_A[h]) for h in range(H_MOBA))
HEADS_NSA = tuple((0, h // R_NSA, h // R_NSA, SLOPES_B[h]) for h in range(H_NSA))


def _compress_kernel(*refs, paged, n_rows):
    if paged:
        pt_ref, x_hbm, pe_ref, w1_ref, b1_ref, wcat_ref, w2_ref, b2_ref, o_ref, xs, sem, stage = refs
    else:
        x_hbm, pe_ref, w1_ref, b1_ref, wcat_ref, w2_ref, b2_ref, o_ref, xs, sem = refs
    nc = n_rows // CMP_STRIDE
    if paged:
        b = pl.program_id(0)
        n_pages = n_rows // PAGE_SIZE
        ch = stage.shape[1]

        def page_copy(c, u):
            return pltpu.make_async_copy(x_hbm.at[pt_ref[b, c * ch + u], pl.ds(0, 256), :],
                                         stage.at[c % 2, u], sem.at[c % 2])

        def start_chunk(c):
            def body(u, carry):
                page_copy(c, u).start()
                return carry
            lax.fori_loop(0, ch, body, 0)

        def finish_chunk(c):
            def wait_body(u, carry):
                page_copy(c, u).wait()
                return carry
            lax.fori_loop(0, ch, wait_body, 0)

            def body(u, carry):
                row0 = pl.multiple_of((c * ch + u) * PAGE_SIZE, PAGE_SIZE)
                for c2 in range(2):
                    xs[c2, pl.ds(row0, PAGE_SIZE), :] = stage[c % 2, u, c2 * LANES:(c2 + 1) * LANES, :].T
                return carry
            lax.fori_loop(0, ch, body, 0, unroll=4)

        start_chunk(0)
        xs[:, pl.ds(n_rows, CMP_STRIDE), :] = jnp.zeros((2, CMP_STRIDE, LANES), F32)
        for c in range(n_pages // ch):
            if c + 1 < n_pages // ch:
                start_chunk(c + 1)
            finish_chunk(c)
    else:
        cps = [pltpu.make_async_copy(x_hbm.at[:, pl.ds(c2 * LANES, LANES)], xs.at[c2, pl.ds(0, n_rows), :], sem.at[0])
               for c2 in range(2)]
        for cp in cps:
            cp.start()
        xs[:, pl.ds(n_rows, CMP_STRIDE), :] = jnp.zeros((2, CMP_STRIDE, LANES), F32)
        for cp in cps:
            cp.wait()

    def rows(r, c0, n):
        return jnp.concatenate([xs[c2, pl.ds(r + CMP_STRIDE * c0, n, stride=CMP_STRIDE), :] for c2 in range(2)],
                               axis=1).astype(BF16)

    consts = []
    for j in range(2):
        cj = _dot3(pe_ref[j], w1_ref[j])[0:1] + b1_ref[j]
        consts += [cj, cj]
    const = jnp.concatenate(consts, axis=1)
    rc = min(256, nc)
    for c0 in range(0, nc, rc):
        xcat = jnp.concatenate([rows(r, c0, rc) for r in range(CMP_LEN)], axis=1)
        hid = jnp.dot(xcat, wcat_ref[...], preferred_element_type=F32) + const
        act = jax.nn.gelu(hid)
        out = jnp.dot(act.astype(BF16), w2_ref[...], preferred_element_type=F32) + b2_ref[...]
        row = lax.broadcasted_iota(I32, out.shape, 0) + c0
        o_ref[0, c0:c0 + rc, :] = jnp.where(row < nc - 1, out, 0.0)


def _compress(x, page_table, cw, *, n_rows, n_batch):
    paged = page_table is not None
    nc = n_rows // CMP_STRIDE
    const = lambda shape: pl.BlockSpec(shape, lambda *_: (0,) * len(shape), pipeline_mode=pl.Buffered(1))
    in_specs = [pl.BlockSpec(memory_space=pl.ANY), const((2, 8, 2048)), const((2, 2048, 128)), const((2, 1, 128)),
                const((CMP_LEN * 256, 512)), const((512, 256)), const((1, 256))]
    scratch = [pltpu.VMEM((2, n_rows + CMP_STRIDE, LANES), F32), pltpu.SemaphoreType.DMA((2,))]
    if paged:
        chunk = min(32, n_rows // PAGE_SIZE)
        scratch.append(pltpu.VMEM((2, chunk, 256, PAGE_SIZE), F32))
    grid_spec = pltpu.PrefetchScalarGridSpec(
        num_scalar_prefetch=1 if paged else 0,
        grid=(n_batch,),
        in_specs=in_specs,
        out_specs=pl.BlockSpec((1, nc, 256), lambda b, *_: (b, 0, 0)),
        scratch_shapes=scratch)
    args = ((page_table,) if paged else ()) + (x,) + cw
    return pl.pallas_call(
        functools.partial(_compress_kernel, paged=paged, n_rows=n_rows),
        grid_spec=grid_spec,
        out_shape=jax.ShapeDtypeStruct((n_batch, nc, 256), F32),
        compiler_params=_cparams("arbitrary"),
        name="compress_paged" if paged else "compress",
    )(*args)


def _compress_weights(cmp_pe, cmp_w1, cmp_b1, cmp_w2, cmp_b2):
    half = CMP_STRIDE * D_HEAD
    wa = jnp.zeros((16, 256, 512), F32)
    wb = jnp.zeros((16, 256, 512), F32)
    w2 = jnp.zeros((512, 256), F32)
    for jg in range(4):
        j = jg // 2
        wa = wa.at[:, jg * 64:(jg + 1) * 64, jg * 128:(jg + 1) * 128].set(cmp_w1[j, :half].reshape(16, 64, 128))
        wb = wb.at[:, jg * 64:(jg + 1) * 64, jg * 128:(jg + 1) * 128].set(cmp_w1[j, half:].reshape(16, 64, 128))
        w2 = w2.at[jg * 128:(jg + 1) * 128, jg * 64:(jg + 1) * 64].set(cmp_w2[j])
    pe = jnp.zeros((2, 8, 2048), F32).at[:, 0].set(cmp_pe.reshape(2, 2048))
    b2 = jnp.concatenate([cmp_b2[0], cmp_b2[0], cmp_b2[1], cmp_b2[1]]).reshape(1, 256)
    wcat = jnp.concatenate([wa.reshape(16 * 256, 512), wb.reshape(16 * 256, 512)], axis=0)
    return (pe, cmp_w1, cmp_b1.reshape(2, 1, 128), wcat.astype(BF16), w2.astype(BF16), b2)


def _overlap_matrix(ns, nc_pad):
    ratio = SLC_BLOCK // CMP_STRIDE
    lr = CMP_LEN // CMP_STRIDE
    w = np.zeros((ns, nc_pad), np.float32)
    for s in range(ns):
        for m in range(ratio):
            for n in range(lr):
                c = ratio * s + m - n
                if 0 <= c < nc_pad - 1:
                    w[s, c] += 1.0
    return w


def _cmp_attn_kernel(qT_ref, pos_ref, kc_ref, vcT_ref, wT_ref, oc_ref, sel_ref, idx_ref, *, k_sel, tq, causal_tiles):
    nc_pad = kc_ref.shape[1]
    ns = wT_ref.shape[0]
    ratio = SLC_BLOCK // CMP_STRIDE
    pos = pos_ref[...]

    def run(n):
        nsu = n // ratio
        cidx = lax.broadcasted_iota(I32, (n, tq), 0)
        cpos = cidx * CMP_STRIDE + (CMP_LEN - 1)
        valid = (cpos <= pos) & (cidx < nc_pad - 1)
        ndist = (cpos - pos).astype(F32)
        blk = lax.broadcasted_iota(I32, (nsu, tq), 0)
        d = jnp.right_shift(pos, 6) - blk
        forced = (blk < SLC_INIT) | ((d >= 0) & (d < SLC_LOCAL))
        kc = kc_ref[0, 0:n, :]
        vcT = vcT_ref[0, :, 0:n]
        wT = wT_ref[0:nsu, 0:n]
        for g in range(G_NSA):
            imp = jnp.zeros((n, tq), F32)
            for r in range(R_NSA):
                h = g * R_NSA + r
                qh = _head_rows(qT_ref[h * 64:(h + 1) * 64, :], g).astype(BF16)
                s = jnp.dot(kc, qh, preferred_element_type=F32) + ndist * SLOPES_B[h]
                s = jnp.where(valid, s, NEG_INF)
                m = jnp.max(s, axis=0, keepdims=True)
                m = jnp.where(m > NEG_INF, m, 0.0)
                p = jnp.exp(s - m)
                p = p * (1.0 / jnp.maximum(jnp.sum(p, axis=0, keepdims=True), 1e-30))
                imp = imp + p
                oc = jnp.dot(vcT, p.astype(BF16), preferred_element_type=F32)
                oc_ref[h * 64:(h + 1) * 64, :] = oc[g * 64:(g + 1) * 64]
            hi, lo = _split_bf16(imp)
            islc = jnp.dot(wT, hi, preferred_element_type=F32) + jnp.dot(wT, lo, preferred_element_type=F32)
            score = jnp.where(forced, POS_INF, jnp.where(d >= 0, islc, NEG_INF))
            sel, picks = _topk_axis0(score, k_sel, blk)
            sel_ref[g, 0:nsu, :] = sel
            if nsu < ns:
                sel_ref[g, nsu:ns, :] = jnp.zeros((ns - nsu, tq), F32)
            for r, pk in enumerate(picks):
                idx_ref[g, r:r + 1, :] = pk
            for r in range(len(picks), SLC_TOPK):
                idx_ref[g, r:r + 1, :] = jnp.full((1, tq), -1, I32)

    if causal_tiles and nc_pad % (4 * LANES) == 0:
        quarter = nc_pad // 4
        tiles_per_quarter = quarter * CMP_STRIDE // tq
        v_sel = pl.program_id(0) // tiles_per_quarter
        for v in range(4):
            pl.when(v_sel == v)(functools.partial(run, (v + 1) * quarter))
    else:
        run(nc_pad)


def _cmp_attn(qT, pos, kc, vcT, ns, *, k_sel, tq, tiles_per_batch):
    n = qT.shape[1]
    nc_pad = kc.shape[1]
    wT = jnp.asarray(_overlap_matrix(ns, nc_pad), BF16)
    bmap = (lambda i: (i // tiles_per_batch, 0, 0)) if tiles_per_batch else (lambda i: (0, 0, 0))
    return pl.pallas_call(
        functools.partial(_cmp_attn_kernel, k_sel=k_sel, tq=tq, causal_tiles=(tiles_per_batch == 0)),
        grid=(n // tq,),
        in_specs=[pl.BlockSpec((512, tq), lambda i: (0, i)),
                  pl.BlockSpec((1, tq), lambda i: (0, i)),
                  pl.BlockSpec((1, nc_pad, 128), bmap),
                  pl.BlockSpec((1, 128, nc_pad), bmap),
                  pl.BlockSpec((ns, nc_pad), lambda i: (0, 0))],
        out_specs=[pl.BlockSpec((512, tq), lambda i: (0, i)),
                   pl.BlockSpec((G_NSA, ns, tq), lambda i: (0, 0, i)),
                   pl.BlockSpec((G_NSA, SLC_TOPK, tq), lambda i: (0, 0, i))],
        out_shape=[jax.ShapeDtypeStruct((512, n), F32),
                   jax.ShapeDtypeStruct((G_NSA, ns, n), F32),
                   jax.ShapeDtypeStruct((G_NSA, SLC_TOPK, n), I32)],
        compiler_params=_cparams("arbitrary"),
        name="cmp_attn",
    )(qT, pos, kc, vcT, wT)


def _post_kernel(x_ref, oa_ref, oc_ref, os_ref, ow_ref, gb_ref, g1_ref, sc2_ref, sh2_ref,
                 goa_ref, gob_ref, gpost_ref, gffn_ref, e3_ref, wout_ref, x1_ref, h2_ref):
    gates = jax.nn.sigmoid(gb_ref[...])
    ob = (_dot3(gates, e3_ref[0]) * oc_ref[...] + _dot3(gates, e3_ref[1]) * os_ref[...]
          + _dot3(gates, e3_ref[2]) * ow_ref[...])
    oa_n = _rms(oa_ref[...], goa_ref[...]).astype(BF16)
    ob_n = _rms(ob, gob_ref[...]).astype(BF16)
    o = (jnp.dot(oa_n, wout_ref[0:512, :], preferred_element_type=F32)
         + jnp.dot(ob_n, wout_ref[512:1024, :], preferred_element_type=F32))
    x1 = x_ref[...] + g1_ref[...] * _rms(o, gpost_ref[...])
    x1_ref[...] = x1
    h2_ref[...] = _rms(x1, gffn_ref[...]) * (1.0 + sc2_ref[...]) + sh2_ref[...]


def _gate_expand():
    e = np.zeros((3, 128, 512), np.float32)
    for h in range(H_NSA):
        for j in range(3):
            e[j, 3 * h + j, h * 64:(h + 1) * 64] = 1.0
    return jnp.asarray(e)


def _post(x, oa, oc, os_, ow, gb, gate1, scale2, shift2, g_oa, g_ob, g_post, g_ffn, w_out_bf):
    r, d = x.shape
    tm = min(256, r)
    per_row = gate1.shape[0] != 1
    mod = pl.BlockSpec((tm, d), lambda i: (i, 0)) if per_row else pl.BlockSpec((1, d), lambda i: (0, 0))
    row = lambda w: pl.BlockSpec((tm, w), lambda i: (i, 0))
    const = lambda shape: pl.BlockSpec(shape, lambda i: (0,) * len(shape))
    return pl.pallas_call(
        _post_kernel,
        grid=(r // tm,),
        in_specs=[row(d), row(512), row(512), row(512), row(512), row(128), mod, mod, mod,
                  const((1, 512)), const((1, 512)), const((1, d)), const((1, d)),
                  const((3, 128, 512)), const((1024, d))],
        out_specs=[row(d), row(d)],
        out_shape=[jax.ShapeDtypeStruct((r, d), F32), jax.ShapeDtypeStruct((r, d), F32)],
        compiler_params=_cparams("arbitrary"),
        name="post_attn",
    )(x, oa, oc, os_, ow, gb, gate1, scale2, shift2, g_oa.reshape(1, 512), g_ob.reshape(1, 512),
      g_post.reshape(1, d), g_ffn.reshape(1, d), _gate_expand(), w_out_bf)


def _router_kernel(h_ref, wr_ref, br_ref, tri_ref, idx_ref, wt_ref, rank_ref, cnt_ref, run_ref, *, n_valid, tm):
    i = pl.program_id(0)

    @pl.when(i == 0)
    def _():
        run_ref[...] = jnp.zeros_like(run_ref)

    s = jax.nn.sigmoid(_dot3(h_ref[...], wr_ref[...]))
    sel = s + br_ref[...]
    lane = lax.broadcasted_iota(I32, (tm, N_EXPERTS), 1)
    per = N_EXPERTS // N_EXPERT_GROUPS
    grp = jnp.right_shift(lane, 5)
    lane_g = lax.broadcasted_iota(I32, (tm, LANES), 1)
    gscore = jnp.full((tm, LANES), NEG_INF, F32)
    for g in range(N_EXPERT_GROUPS):
        v = jnp.where(grp == g, sel, NEG_INF)
        (i1, m1), (_, m2) = _topk_axis1(v, 2, lane)
        gscore = jnp.where(lane_g == g, m1 + m2, gscore)
    emask = jnp.zeros((tm, N_EXPERTS), jnp.bool_)
    for first, _ in _topk_axis1(gscore, TOPK_GROUPS, lane_g):
        emask = emask | (grp == first)
    picks = _topk_axis1(jnp.where(emask, sel, NEG_INF), TOPK_EXPERTS, lane)
    row = lax.broadcasted_iota(I32, (tm, 1), 0) + i * tm
    row_ok = row < n_valid
    onehot = jnp.zeros((tm, N_EXPERTS), F32)
    ws = []
    for first, _ in picks:
        pick = lane == first
        ws.append(jnp.sum(jnp.where(pick, s, 0.0), axis=1, keepdims=True))
        onehot = jnp.where(pick & row_ok, 1.0, onehot)
    wsum = ws[0]
    for w in ws[1:]:
        wsum = wsum + w
    excl = jnp.dot(tri_ref[...], onehot.astype(BF16), preferred_element_type=F32) + run_ref[...]
    idx_out = jnp.zeros((tm, LANES), I32)
    wt_out = jnp.zeros((tm, LANES), F32)
    rank_out = jnp.zeros((tm, LANES), I32)
    for k, (first, _) in enumerate(picks):
        rk = jnp.sum(jnp.where(lane == first, excl, 0.0), axis=1, keepdims=True)
        idx_out = jnp.where(lane_g == k, first, idx_out)
        wt_out = jnp.where(lane_g == k, ws[k] / wsum * ROUTED_SCALE, wt_out)
        rank_out = jnp.where(lane_g == k, rk.astype(I32), rank_out)
    idx_ref[...] = idx_out
    wt_ref[...] = wt_out
    rank_ref[...] = rank_out
    run_ref[...] = run_ref[...] + jnp.sum(onehot, axis=0, keepdims=True)
    cnt_ref[...] = run_ref[...]


def _router(h_all, w_router, b_router, n_valid):
    n, d = h_all.shape
    tm = ROUTER_TILE
    tri = jnp.asarray(np.tril(np.ones((tm, tm), np.float32), -1), BF16)
    const = lambda shape: pl.BlockSpec(shape, lambda i: (0,) * len(shape))
    row = lambda w: pl.BlockSpec((tm, w), lambda i: (i, 0))
    return pl.pallas_call(
        functools.partial(_router_kernel, n_valid=n_valid, tm=tm),
        grid=(n // tm,),
        in_specs=[row(d), const((d, N_EXPERTS)), const((1, N_EXPERTS)), const((tm, tm))],
        out_specs=[row(LANES), row(LANES), row(LANES), const((1, N_EXPERTS))],
        out_shape=[jax.ShapeDtypeStruct((n, LANES), I32), jax.ShapeDtypeStruct((n, LANES), F32),
                   jax.ShapeDtypeStruct((n, LANES), I32), jax.ShapeDtypeStruct((1, N_EXPERTS), F32)],
        scratch_shapes=[pltpu.VMEM((1, N_EXPERTS), F32)],
        compiler_params=_cparams("arbitrary"),
        name="router",
    )(h_all, w_router, b_router.reshape(1, N_EXPERTS), tri)


def _gather_rows(idx_hbm, idx_smem, isem, src_hbm, buf, sem, step, n_steps, n_rows):
    def idx_copy(k, slot):
        return pltpu.make_async_copy(idx_hbm.at[k], idx_smem.at[slot], isem.at[slot])

    def issue(slot):
        def body(r2, c):
            for prio in range(2):
                r = 2 * r2 + prio
                pltpu.make_async_copy(src_hbm.at[idx_smem[slot, r]], buf.at[slot, r], sem.at[slot]).start(priority=prio)
            return c
        lax.fori_loop(0, n_rows // 2, body, 0, unroll=4)

    @pl.when(step == 0)
    def _():
        idx_copy(0, 0).start()
        idx_copy(0, 0).wait()
        issue(0)

        @pl.when(1 < n_steps)
        def _():
            idx_copy(1, 1).start()

    nxt = (step + 1) % 2

    @pl.when(step + 1 < n_steps)
    def _():
        idx_copy(step + 1, nxt).wait()
        issue(nxt)

    @pl.when(step + 2 < n_steps)
    def _():
        idx_copy(step + 2, step % 2).start()

    slot = step % 2

    @pl.when(step < n_steps)
    def _():
        pltpu.make_async_copy(src_hbm.at[pl.ds(0, n_rows)], buf.at[slot], sem.at[slot]).wait()
    return slot


def _expert_kernel(blk_e_ref, nused_ref, sched_ref, tok_hbm, h_hbm, wup_hbm, wdn_hbm, o_ref, buf0, buf1, idx_smem,
                   sem, isem, wup_bf, wdn_bf, wup_f32, wdn_f32, wsem):
    i = pl.program_id(0)
    nused = nused_ref[0]
    rb = ROW_BLOCK
    bufs = (buf0, buf1)

    def w_copies(e, slot):
        return (pltpu.make_async_copy(wup_hbm.at[e], wup_f32.at[slot], wsem.at[slot]),
                pltpu.make_async_copy(wdn_hbm.at[e], wdn_f32.at[slot], wsem.at[slot]))

    def idx_copy(k, slot):
        return pltpu.make_async_copy(tok_hbm.at[k], idx_smem.at[slot], isem.at[slot])

    def issue_rows(slot):
        for r in range(rb):
            pltpu.make_async_copy(h_hbm.at[idx_smem[slot, r]], bufs[slot].at[r], sem.at[slot]).start()

    def wait_rows(slot):
        pltpu.make_async_copy(h_hbm.at[pl.ds(0, rb)], bufs[slot], sem.at[slot]).wait()

    def compute(slot):
        buf = bufs[slot]
        x = jnp.concatenate([buf[:, s, :] for s in range(8)], axis=1).astype(BF16)
        up = jnp.dot(x, wup_bf[...], preferred_element_type=F32)
        half = up.shape[1] // 2
        act = (jax.nn.silu(up[:, :half]) * up[:, half:]).astype(BF16)
        y = jnp.dot(act, wdn_bf[...], preferred_element_type=F32)
        for s in range(8):
            o_ref[:, s, :] = y[:, s * 128:(s + 1) * 128]

    @pl.when(i == 0)
    def _():
        for cp in w_copies(blk_e_ref[0], 0):
            cp.start()
        idx_copy(0, 0).start()
        idx_copy(0, 0).wait()
        issue_rows(0)

        @pl.when(1 < nused)
        def _():
            idx_copy(1, 1).start()

    @pl.when(i + 2 < nused)
    def _():
        idx_copy(i + 2, i % 2).start()

    first = (sched_ref[0, i] == 1) & (i < nused)
    for wslot in range(2):
        @pl.when(first & (sched_ref[1, i] == wslot))
        def _():
            for cp in w_copies(blk_e_ref[i], wslot):
                cp.wait()

            @pl.when(sched_ref[3, i] == 1)
            def _():
                for cp in w_copies(sched_ref[2, i], 1 - wslot):
                    cp.start()

            wup_bf[...] = wup_f32[wslot].astype(BF16)
            wdn_bf[...] = wdn_f32[wslot].astype(BF16)

    has_next = i + 1 < nused
    for slot in range(2):
        mine = (i % 2) == slot

        @pl.when(mine & has_next)
        def _():
            idx_copy(i + 1, 1 - slot).wait()
            issue_rows(1 - slot)
            wait_rows(slot)
            compute(slot)

        @pl.when(mine & (i < nused) & jnp.logical_not(has_next))
        def _():
            wait_rows(slot)
            compute(slot)

    @pl.when(i >= nused)
    def _():
        o_ref[...] = jnp.zeros_like(o_ref)


def _experts(h3, blk_e, row_tok, nused, p_end, w_exp_up, w_exp_down):
    nblk = blk_e.shape[0]
    rb = ROW_BLOCK
    e, d, f2 = w_exp_up.shape
    blk = jnp.arange(nblk, dtype=I32)
    first = (blk == 0) | (blk_e != jnp.roll(blk_e, 1))
    parity = (jnp.cumsum(first.astype(I32)) - 1) % 2
    nxt_pos = p_end[blk_e] // rb
    nxt_e = blk_e[jnp.minimum(nxt_pos, nblk - 1)]
    sched = jnp.stack([first.astype(I32), parity, nxt_e, (nxt_pos < nused[0]).astype(I32)]).astype(I32)
    any_spec = pl.BlockSpec(memory_space=pl.ANY)
    grid_spec = pltpu.PrefetchScalarGridSpec(
        num_scalar_prefetch=3,
        grid=(nblk,),
        in_specs=[any_spec, any_spec, any_spec, any_spec],
        out_specs=pl.BlockSpec((rb, 8, 128), lambda i, be, nu, sc: (i, 0, 0)),
        scratch_shapes=[pltpu.VMEM((rb, 8, 128), F32), pltpu.VMEM((rb, 8, 128), F32), pltpu.SMEM((2, rb), I32),
                        pltpu.SemaphoreType.DMA((2,)), pltpu.SemaphoreType.DMA((2,)),
                        pltpu.VMEM((d, f2), BF16), pltpu.VMEM((f2 // 2, d), BF16),
                        pltpu.VMEM((2, d, f2), F32), pltpu.VMEM((2, f2 // 2, d), F32),
                        pltpu.SemaphoreType.DMA((2,))])
    return pl.pallas_call(
        _expert_kernel,
        grid_spec=grid_spec,
        out_shape=jax.ShapeDtypeStruct((nblk * rb, 8, 128), F32),
        compiler_params=_cparams("arbitrary"),
        name="experts",
    )(blk_e, nused, sched, row_tok.reshape(nblk, rb), h3, w_exp_up, w_exp_down)


def _combine_kernel(dest_hbm, y_hbm, wt_ref, h_ref, x1_ref, g2_ref, gpost_ref, wsu_ref, wsd_ref, o_ref,
                    buf0, buf1, idx_smem, sem, isem, *, tm):
    i = pl.program_id(0)
    nt = pl.num_programs(0)
    na = tm * TOPK_EXPERTS
    bufs = (buf0, buf1)

    def idx_copy(k, slot):
        return pltpu.make_async_copy(dest_hbm.at[k], idx_smem.at[slot], isem.at[slot])

    def issue_rows(slot):
        for r in range(na):
            pltpu.make_async_copy(y_hbm.at[idx_smem[slot, r]], bufs[slot].at[r], sem.at[slot]).start(priority=r % 2)

    def wait_rows(slot):
        pltpu.make_async_copy(y_hbm.at[pl.ds(0, na)], bufs[slot], sem.at[slot]).wait()

    def compute(slot):
        buf = bufs[slot]
        wt = wt_ref[...]
        f = jnp.zeros((tm, h_ref.shape[1]), F32)
        for k in range(TOPK_EXPERTS):
            yk = jnp.concatenate([buf[pl.ds(k, tm, stride=8), s, :] for s in range(8)], axis=1)
            f = f + yk * wt[:, k:k + 1]
        hb = h_ref[...].astype(BF16)
        up = jnp.dot(hb, wsu_ref[...], preferred_element_type=F32)
        half = up.shape[1] // 2
        act = (jax.nn.silu(up[:, :half]) * up[:, half:]).astype(BF16)
        f = f + jnp.dot(act, wsd_ref[...], preferred_element_type=F32)
        o_ref[...] = x1_ref[...] + g2_ref[...] * _rms(f, gpost_ref[...])

    @pl.when(i == 0)
    def _():
        idx_copy(0, 0).start()
        idx_copy(0, 0).wait()

        def body(r, c):
            pltpu.make_async_copy(y_hbm.at[idx_smem[0, r]], buf0.at[r], sem.at[0]).start()
            return c
        lax.fori_loop(0, na, body, 0, unroll=8)

        @pl.when(1 < nt)
        def _():
            idx_copy(1, 1).start()

    @pl.when(i + 2 < nt)
    def _():
        idx_copy(i + 2, i % 2).start()

    has_next = i + 1 < nt
    for slot in range(2):
        mine = (i % 2) == slot

        @pl.when(mine & has_next)
        def _():
            idx_copy(i + 1, 1 - slot).wait()
            issue_rows(1 - slot)
            wait_rows(slot)
            compute(slot)

        @pl.when(mine & jnp.logical_not(has_next))
        def _():
            wait_rows(slot)
            compute(slot)


def _combine(dest, y, wt, h_all, x1_all, gate2_tab, n_prompt, g_ffn_post, w_sh_up_bf, w_sh_down_bf):
    n, d = h_all.shape
    tm = 128
    assert n_prompt % tm == 0
    const = lambda shape: pl.BlockSpec(shape, lambda i: (0,) * len(shape))
    row = lambda w: pl.BlockSpec((tm, w), lambda i: (i, 0))
    gate_spec = pl.BlockSpec((tm, d), lambda i: (jnp.maximum(i - (n_prompt // tm - 1), 0), 0))
    na = tm * TOPK_EXPERTS
    return pl.pallas_call(
        functools.partial(_combine_kernel, tm=tm),
        grid=(n // tm,),
        in_specs=[pl.BlockSpec(memory_space=pl.ANY), pl.BlockSpec(memory_space=pl.ANY), row(LANES), row(d), row(d),
                  gate_spec, const((1, d)), const(w_sh_up_bf.shape), const(w_sh_down_bf.shape)],
        out_specs=row(d),
        scratch_shapes=[pltpu.VMEM((na, 8, 128), F32), pltpu.VMEM((na, 8, 128), F32), pltpu.SMEM((2, na), I32),
                        pltpu.SemaphoreType.DMA((2,)), pltpu.SemaphoreType.DMA((2,))],
        out_shape=jax.ShapeDtypeStruct((n, d), F32),
        compiler_params=_cparams("arbitrary"),
        name="moe_combine",
    )(dest.reshape(n // tm, na), y, wt, h_all, x1_all, gate2_tab, g_ffn_post.reshape(1, d),
      w_sh_up_bf, w_sh_down_bf)


def _dest_kernel(idx_ref, rank_ref, ps_ref, o_ref):
    tm = idx_ref.shape[0]
    idxf = idx_ref[...].astype(F32)
    ps = ps_ref[...].astype(F32)
    lane_g = lax.broadcasted_iota(I32, (tm, LANES), 1)
    lane = lax.broadcasted_iota(I32, (tm, N_EXPERTS), 1).astype(F32)
    base = jnp.zeros((tm, LANES), F32)
    for k in range(TOPK_EXPERTS):
        idx_k = jnp.sum(jnp.where(lane_g == k, idxf, 0.0), axis=1, keepdims=True)
        ps_k = jnp.sum(jnp.where(lane == idx_k, ps, 0.0), axis=1, keepdims=True)
        base = jnp.where(lane_g == k, ps_k, base)
    o_ref[...] = base.astype(I32) + rank_ref[...]


def _moe_dest(idx_p, rank_p, p_start):
    n = idx_p.shape[0]
    tm = 128
    row = pl.BlockSpec((tm, LANES), lambda i: (i, 0))
    return pl.pallas_call(
        _dest_kernel,
        grid=(n // tm,),
        in_specs=[row, row, pl.BlockSpec((1, N_EXPERTS), lambda i: (0, 0))],
        out_specs=row,
        out_shape=jax.ShapeDtypeStruct((n, LANES), I32),
        compiler_params=_cparams("arbitrary"),
        name="moe_dest",
    )(idx_p, rank_p, p_start.reshape(1, N_EXPERTS))


def _moe(h_all, x1_all, gate2_tab, n_prompt, n_valid, w_router, b_router, w_exp_up, w_exp_down, w_sh_up, w_sh_down,
         g_ffn_post):
    n, d = h_all.shape
    rb = ROW_BLOCK
    e = N_EXPERTS
    idx_p, wt_p, rank_p, cnt = _router(h_all, w_router, b_router, n_valid)
    counts = cnt[0].astype(I32)
    padded = (counts + rb - 1) // rb * rb
    p_end = jnp.cumsum(padded)
    p_start = p_end - padded
    dest = _moe_dest(idx_p, rank_p, p_start)[:, :TOPK_EXPERTS]
    nblk = -(-(n_valid * TOPK_EXPERTS + e * (rb - 1)) // rb)
    tok = jnp.repeat(jnp.arange(n_valid, dtype=I32), TOPK_EXPERTS)
    row_tok = jnp.zeros((nblk * rb,), I32).at[dest[:n_valid].reshape(-1)].set(tok)
    blk_start = jnp.arange(nblk, dtype=I32) * rb
    blk_e = jnp.minimum(jnp.sum((p_end[None, :] <= blk_start[:, None]).astype(I32), axis=1), e - 1)
    nused = (p_end[-1] // rb).astype(I32).reshape(1)
    y = _experts(h_all.reshape(n, 8, 128), blk_e, row_tok, nused, p_end.astype(I32), w_exp_up, w_exp_down)
    dest = jnp.where(jnp.arange(n, dtype=I32)[:, None] < n_valid, dest, 0)
    return _combine(dest, y, wt_p, h_all, x1_all, gate2_tab, n_prompt, g_ffn_post,
                    w_sh_up.astype(BF16), w_sh_down.astype(BF16))


def _kmean_pages_kernel(pt_ref, *refs):
    pages, o_ref = refs[:-1], refs[-1]
    s_idx = pl.program_id(1)
    per_step = len(pages) // 2

    @pl.when(s_idx == 0)
    def _():
        o_ref[...] = jnp.zeros_like(o_ref)

    lane = lax.broadcasted_iota(I32, (512, LANES), 1)
    acc = o_ref[0]
    for u in range(per_step):
        tot = jnp.sum(pages[2 * u][0] + pages[2 * u + 1][0], axis=1, keepdims=True)
        acc = jnp.where(lane == s_idx * per_step + u, tot * (1.0 / MOBA_BLOCK), acc)
    o_ref[0] = acc


def _kmean_pages(cache_a, page_table):
    bs, n_pages = page_table.shape
    pps = PAGES_PER_STEP
    assert n_pages // 2 <= LANES
    in_specs = [pl.BlockSpec((1, 512, PAGE_SIZE), lambda b, s, pt, u=u: (pt[b, s * pps + u], 0, 0)) for u in range(pps)]
    grid_spec = pltpu.PrefetchScalarGridSpec(
        num_scalar_prefetch=1, grid=(bs, n_pages // pps), in_specs=in_specs,
        out_specs=pl.BlockSpec((1, 512, LANES), lambda b, s, pt: (b, 0, 0)))
    return pl.pallas_call(
        _kmean_pages_kernel, grid_spec=grid_spec,
        out_shape=jax.ShapeDtypeStruct((bs, 512, LANES), F32),
        compiler_params=_cparams("arbitrary", "arbitrary"),
        name="kmean_pages",
    )(page_table, *([cache_a] * pps))


def _moba_sample_kernel(pt_ref, qbd_ref, km_ref, new_ref, *refs, nb, ts, past):
    pps = PAGES_PER_STEP
    pages = refs[:pps]
    o_ref, sel_scr, m_scr, l_scr, acc_scr = refs[pps:]
    s_idx = pl.program_id(1)
    rows = H_MOBA * ts
    qbd = qbd_ref[0]
    qb = qbd.astype(BF16)
    rowi = lax.broadcasted_iota(I32, (rows, 1), 0)
    slope = jnp.zeros((rows, 1), F32)
    for h in range(H_MOBA):
        slope = jnp.where(_div_pow2(rowi, ts) == h, SLOPES_A[h], slope)
    lane = lax.broadcasted_iota(I32, (rows, LANES), 1)

    @pl.when(s_idx == 0)
    def _():
        gate = _dot3(qbd, km_ref[0])
        gate = jnp.where(lane < nb, gate, NEG_INF)
        sel = jnp.zeros((rows, LANES), F32)
        for first, m in _topk_axis1(gate, MOBA_TOPK, lane):
            sel = jnp.where((lane == first) & (m > NEG_INF), 1.0, sel)
        sel_scr[...] = sel
        m_scr[...] = jnp.full_like(m_scr, M_INIT)
        l_scr[...] = jnp.zeros_like(l_scr)
        acc_scr[...] = jnp.zeros_like(acc_scr)

    def update(ss, vTs):
        m = m_scr[...]
        m_new = m
        for s in ss:
            m_new = jnp.maximum(m_new, jnp.max(s, axis=1, keepdims=True))
        alpha = jnp.exp(m - m_new)
        lsum = jnp.zeros_like(m)
        pv = jnp.zeros(acc_scr.shape, F32)
        for s, vT in zip(ss, vTs):
            p = jnp.exp(s - m_new)
            lsum = lsum + jnp.sum(p, axis=1, keepdims=True)
            pv = pv + lax.dot_general(p.astype(BF16), vT(), _NT, preferred_element_type=F32)
        l_scr[...] = l_scr[...] * alpha + lsum
        acc_scr[...] = acc_scr[...] * alpha + pv
        m_scr[...] = m_new

    sel = sel_scr[...]
    ss, vTs = [], []
    for u in range(pps):
        page = s_idx * pps + u
        flag = jnp.sum(jnp.where(lane == page // 2, sel, 0.0), axis=1, keepdims=True) > 0.0
        kT = pages[u][0, 0:512, :].astype(BF16)
        s = jnp.dot(qb, kT, preferred_element_type=F32)
        kpos = page * PAGE_SIZE + lane - past
        ss.append(jnp.where(flag, s + slope * kpos.astype(F32), NEG_INF))
        vTs.append(lambda u=u: pages[u][0, 512:1024, :].astype(BF16))
    update(ss, vTs)

    @pl.when(s_idx == pl.num_programs(1) - 1)
    def _():
        kn = new_ref[0, 0:512, :].astype(BF16)
        s = jnp.dot(qb, kn, preferred_element_type=F32)
        ok = (lane < ts) & (lane <= _mod_pow2(rowi, ts))
        s = jnp.where(ok, s + slope * lane.astype(F32), NEG_INF)
        update([s], [lambda: new_ref[0, 512:1024, :].astype(BF16)])
        o = acc_scr[...] / jnp.maximum(l_scr[...], 1e-30)
        col_h = _div_pow2(lax.broadcasted_iota(I32, (rows, 512), 1), D_HEAD)
        o = jnp.where(col_h == _div_pow2(rowi, ts), o, 0.0)
        pick_r = lax.broadcasted_iota(I32, (8, rows), 0)
        pick_c = lax.broadcasted_iota(I32, (8, rows), 1)
        gather = jnp.where(_mod_pow2(pick_c, ts) == pick_r, 1.0, 0.0)
        o_ref[0] = _dot3(gather, o)


def _moba_sample(qbd, km_pad, new_pad, cache_a, page_table, *, nb, ts, past):
    bs, n_pages = page_table.shape
    pps = PAGES_PER_STEP
    rows = H_MOBA * ts
    in_specs = [pl.BlockSpec((1, rows, 512), lambda b, s, pt: (b, 0, 0)),
                pl.BlockSpec((1, 512, LANES), lambda b, s, pt: (b, 0, 0)),
                pl.BlockSpec((1, 1024, LANES), lambda b, s, pt: (b, 0, 0))]
    in_specs += [pl.BlockSpec((1, 1024, PAGE_SIZE), lambda b, s, pt, u=u: (pt[b, s * pps + u], 0, 0)) for u in range(pps)]
    grid_spec = pltpu.PrefetchScalarGridSpec(
        num_scalar_prefetch=1, grid=(bs, n_pages // pps), in_specs=in_specs,
        out_specs=pl.BlockSpec((1, 8, 512), lambda b, s, pt: (b, 0, 0)),
        scratch_shapes=[pltpu.VMEM((rows, LANES), F32), pltpu.VMEM((rows, 1), F32), pltpu.VMEM((rows, 1), F32),
                        pltpu.VMEM((rows, 512), F32)])
    return pl.pallas_call(
        functools.partial(_moba_sample_kernel, nb=nb, ts=ts, past=past),
        grid_spec=grid_spec,
        out_shape=jax.ShapeDtypeStruct((bs, 8, 512), F32),
        compiler_params=_cparams("arbitrary", "arbitrary"),
        name="moba_sample",
    )(page_table, qbd, km_pad, new_pad, *([cache_a] * pps))


def _nsa_sample_kernel(pt_ref, idx_ref, qs_ref, qw_ref, kpos_ref, new_ref, win_ref, wnew_ref, cache_hbm,
                       os_ref, ow_ref, buf, sem, *, ts, past, n_slots):
    b = pl.program_id(0)
    ngq = G_NSA * ts
    slot = b % 2

    def page_copies(bb, sl):
        cps = []
        for gq in range(ngq):
            for t in range(n_slots):
                blk = jnp.maximum(idx_ref[(bb * ngq + gq) * SLC_TOPK + t], 0)
                page = pt_ref[bb, blk // 2]
                cps.append(pltpu.make_async_copy(cache_hbm.at[page, pl.ds(256, 256), :],
                                                 buf.at[sl, gq, :, pl.ds(t * PAGE_SIZE, PAGE_SIZE)], sem.at[sl]))
        return cps

    @pl.when(b == 0)
    def _():
        for cp in page_copies(0, 0):
            cp.start()

    @pl.when(b + 1 < pl.num_programs(0))
    def _():
        for cp in page_copies(b + 1, 1 - slot):
            cp.start()

    new = new_ref[0]
    for gq in range(ngq):
        buf[slot, gq, :, pl.ds(n_slots * PAGE_SIZE, PAGE_SIZE)] = new
    for cp in page_copies(b, slot):
        cp.wait()

    rowi = lax.broadcasted_iota(I32, (8, 1), 0)
    for gq in range(ngq):
        g, q = divmod(gq, ts)
        slope = jnp.zeros((8, 1), F32)
        for r in range(R_NSA):
            slope = jnp.where(rowi == r, SLOPES_B[g * R_NSA + r], slope)
        kT = buf[slot, gq, 0:128, :].astype(BF16)
        vT = buf[slot, gq, 128:256, :].astype(BF16)
        s = jnp.dot(qs_ref[0, gq].astype(BF16), kT, preferred_element_type=F32)
        kpos = kpos_ref[0, gq]
        ok = (kpos >= 0) & (kpos <= past + q)
        s = jnp.where(ok, s + slope * (kpos - past).astype(F32), NEG_INF)
        m = jnp.max(s, axis=1, keepdims=True)
        m = jnp.where(m > NEG_INF, m, 0.0)
        p = jnp.exp(s - m)
        p = p * (1.0 / jnp.maximum(jnp.sum(p, axis=1, keepdims=True), 1e-30))
        os_ref[0, gq] = lax.dot_general(p.astype(BF16), vT, _NT, preferred_element_type=F32)

    nw = WINDOW + LANES
    kw_all = jnp.concatenate([win_ref[0], wnew_ref[0]], axis=1)
    kw = kw_all[0:128, :].astype(BF16)
    vw = kw_all[128:256, :].astype(BF16)
    wrow = lax.broadcasted_iota(I32, (R_NSA * ts, 1), 0)
    wlane = lax.broadcasted_iota(I32, (R_NSA * ts, nw), 1)
    dist = (WINDOW + _mod_pow2(wrow, ts)) - wlane
    okw = (dist >= 0) & (dist <= WINDOW) & (wlane < WINDOW + ts)
    for g in range(G_NSA):
        slope = jnp.zeros((R_NSA * ts, 1), F32)
        for r in range(R_NSA):
            slope = jnp.where(_div_pow2(wrow, ts) == r, SLOPES_B[g * R_NSA + r], slope)
        s = jnp.dot(qw_ref[0, g].astype(BF16), kw, preferred_element_type=F32)
        s = jnp.where(okw, s - slope * dist.astype(F32), NEG_INF)
        m = jnp.max(s, axis=1, keepdims=True)
        m = jnp.where(m > NEG_INF, m, 0.0)
        p = jnp.exp(s - m)
        p = p * (1.0 / jnp.maximum(jnp.sum(p, axis=1, keepdims=True), 1e-30))
        ow_ref[0, g] = lax.dot_general(p.astype(BF16), vw, _NT, preferred_element_type=F32)


def _nsa_sample(idx_flat, qs, qw, kpos, new_pad, win, wnew_pad, cache_b, page_table, *, ts, past, n_slots):
    bs = page_table.shape[0]
    ngq = G_NSA * ts
    nk = (n_slots + 1) * PAGE_SIZE
    m4 = lambda shape: pl.BlockSpec(shape, lambda b, pt, ix: (b,) + (0,) * (len(shape) - 1))
    grid_spec = pltpu.PrefetchScalarGridSpec(
        num_scalar_prefetch=2, grid=(bs,),
        in_specs=[m4((1, ngq, 8, 128)), m4((1, G_NSA, R_NSA * ts, 128)), m4((1, ngq, 1, nk)),
                  m4((1, 256, LANES)), m4((1, 256, WINDOW)), m4((1, 256, LANES)),
                  pl.BlockSpec(memory_space=pl.ANY)],
        out_specs=[m4((1, ngq, 8, 128)), m4((1, G_NSA, R_NSA * ts, 128))],
        scratch_shapes=[pltpu.VMEM((2, ngq, 256, nk), F32), pltpu.SemaphoreType.DMA((2,))])
    return pl.pallas_call(
        functools.partial(_nsa_sample_kernel, ts=ts, past=past, n_slots=n_slots),
        grid_spec=grid_spec,
        out_shape=[jax.ShapeDtypeStruct((bs, ngq, 8, 128), F32),
                   jax.ShapeDtypeStruct((bs, G_NSA, R_NSA * ts, 128), F32)],
        compiler_params=_cparams("arbitrary"),
        name="nsa_sample",
    )(page_table, idx_flat, qs, qw, kpos, new_pad, win, wnew_pad, cache_b)


def _vT_blocks(v, tk):
    t, c = v.shape
    return v.astype(BF16).reshape(t // tk, tk, c // 128, 128).transpose(2, 0, 3, 1)


def _prompt_mixer(proj, cw):
    qa, kva, qb, kvb, kvw, _ = proj
    t = qa.shape[0]
    qaT = (qa * Q_SCALE).T
    qbT = (qb * Q_SCALE).T
    kmean = _kmean_prompt(kva)
    sel_a = _moba_gate_prompt(qaT, kmean)
    oaT = _flash_select(qaT, kva[:, :512].astype(BF16), _vT_blocks(kva[:, 512:], TKF), sel_a, HEADS_MOBA,
                        block=MOBA_BLOCK)
    cmp_out = _compress(kvb, None, cw, n_rows=t, n_batch=1)
    kc = cmp_out[:, :, 0:128].astype(BF16)
    vcT = cmp_out[:, :, 128:256].astype(BF16).transpose(0, 2, 1)
    pos = jnp.arange(t, dtype=I32).reshape(1, t)
    ocT, sel_b, _ = _cmp_attn(qbT, pos, kc, vcT, t // SLC_BLOCK, k_sel=SLC_TOPK, tq=TQ, tiles_per_batch=0)
    osT = _flash_select(qbT, kvb[:, 256:384].astype(BF16), _vT_blocks(kvb[:, 384:512], TKF), sel_b, HEADS_NSA,
                        block=SLC_BLOCK)
    owT = _flash_select(qbT, kvw[:, 0:128].astype(BF16), _vT_blocks(kvw[:, 128:256], TKF), None, HEADS_NSA,
                        block=TKF, window=WINDOW)
    return oaT.T, ocT.T, osT.T, owT.T


def _sample_mixer(proj, cache_a, cache_b, win_state, page_table, cw, bs, ts):
    qa, kva, qb, kvb, kvw, _ = proj
    n_pages = page_table.shape[1]
    past = n_pages * PAGE_SIZE
    nb = past // MOBA_BLOCK
    new_rows_T = lambda a: jnp.zeros((bs, LANES, a.shape[-1]), F32).at[:, :ts].set(a).transpose(0, 2, 1)
    km = _kmean_pages(cache_a, page_table)
    q4 = (qa * Q_SCALE).reshape(bs, ts, H_MOBA, D_HEAD).transpose(0, 2, 1, 3)
    eye = jnp.eye(H_MOBA, dtype=F32)
    qbd = (q4[:, :, :, None, :] * eye[None, :, None, :, None]).reshape(bs, H_MOBA * ts, 512)
    new_a = new_rows_T(kva.reshape(bs, ts, 1024))
    oa = _moba_sample(qbd, km, new_a, cache_a, page_table, nb=nb, ts=ts, past=past)[:, :ts]
    oa = oa.reshape(bs * ts, 512)
    cmp_out = _compress(cache_b, page_table, cw, n_rows=past, n_batch=bs)
    kc = cmp_out[:, :, 0:128].astype(BF16)
    vcT = cmp_out[:, :, 128:256].astype(BF16).transpose(0, 2, 1)
    qbs = (qb * Q_SCALE).reshape(bs, ts, 512)
    qT = jnp.zeros((bs, LANES, 512), F32).at[:, :ts].set(qbs).reshape(bs * LANES, 512).T
    pos = jnp.broadcast_to(past + jnp.minimum(jnp.arange(LANES, dtype=I32), ts - 1), (bs, LANES)).reshape(1, bs * LANES)
    n_slots = SLC_TOPK - 1
    ocT, _, idx = _cmp_attn(qT, pos, kc, vcT, past // SLC_BLOCK, k_sel=n_slots, tq=LANES, tiles_per_batch=1)
    oc = ocT.T.reshape(bs, LANES, 512)[:, :ts].reshape(bs * ts, 512)
    idx = idx.reshape(G_NSA, SLC_TOPK, bs, LANES)[:, :, :, :ts].transpose(2, 0, 3, 1)
    q5 = qbs.reshape(bs, ts, G_NSA, R_NSA, D_HEAD)
    lane_g = jnp.eye(G_NSA, dtype=F32)
    qsel = q5.transpose(0, 2, 1, 3, 4)[:, :, :, :, None, :] * lane_g[None, :, None, None, :, None]
    qs = jnp.zeros((bs, G_NSA, ts, 8, 128), F32).at[:, :, :, :R_NSA].set(qsel.reshape(bs, G_NSA, ts, R_NSA, 128))
    qs = qs.reshape(bs, G_NSA * ts, 8, 128)
    qwin = q5.transpose(0, 2, 3, 1, 4)[:, :, :, :, None, :] * lane_g[None, :, None, None, :, None]
    qw = qwin.reshape(bs, G_NSA, R_NSA * ts, 128)
    sl = idx[..., :n_slots, None]
    lane = jnp.arange(PAGE_SIZE, dtype=I32)
    in_blk = (sl >= 0) & ((lane // SLC_BLOCK) == (sl % 2))
    slot_pos = jnp.where(in_blk, (sl // 2) * PAGE_SIZE + lane, -1).reshape(bs, G_NSA, ts, n_slots * PAGE_SIZE)
    own = jnp.where(lane < ts, past + lane, -1)
    kpos = jnp.concatenate([slot_pos, jnp.broadcast_to(own, (bs, G_NSA, ts, PAGE_SIZE))], axis=-1)
    kpos = kpos.reshape(bs, G_NSA * ts, 1, (n_slots + 1) * PAGE_SIZE).astype(I32)
    new_b = new_rows_T(kvb.reshape(bs, ts, 512)[:, :, 256:])
    wnew = new_rows_T(kvw.reshape(bs, ts, 256))
    os_raw, ow_raw = _nsa_sample(idx.reshape(-1).astype(I32), qs, qw, kpos, new_b, win_state, wnew, cache_b,
                                 page_table, ts=ts, past=past, n_slots=n_slots)
    os5 = os_raw.reshape(bs, G_NSA, ts, 8, G_NSA, D_HEAD)[:, :, :, :R_NSA]
    os_ = jnp.stack([os5[:, g, :, :, g] for g in range(G_NSA)], axis=2).reshape(bs * ts, 512)
    ow5 = ow_raw.reshape(bs, G_NSA, R_NSA, ts, G_NSA, D_HEAD)
    ow = jnp.stack([ow5[:, g, :, :, g] for g in range(G_NSA)], axis=1)
    ow = ow.transpose(0, 3, 1, 2, 4).reshape(bs * ts, 512)
    return oa, oc, os_, ow


def _layer(x_p, x_s, cache_a, cache_b, win_state, page_table, c_p, c_s, w):
    (w_ada, b_ada, g_mix_pre, g_mix_post, g_ffn_pre, g_ffn_post, w_in, g_out_moba, g_out_nsa, w_out,
     cmp_pe, cmp_w1, cmp_b1, cmp_w2, cmp_b2, w_router, b_router, w_exp_up, w_exp_down, w_sh_up, w_sh_down) = w
    t, d = x_p.shape
    bs, ts, _ = x_s.shape
    assert c_p.shape[0] == 1 and t % (8 * MOBA_BLOCK) == 0 and win_state.shape[1] == WINDOW
    n_s = bs * ts
    rows = -(-(1 + bs) // 8) * 8
    c_all = jnp.zeros((rows, d), F32).at[0:1].set(c_p).at[1:1 + bs].set(c_s)
    mod = _ada(c_all, w_ada, b_ada).reshape(rows, 6, d)
    mod_p = [mod[0:1, i] for i in range(6)]
    mod_s = [jnp.repeat(mod[1:1 + bs, i], ts, axis=0) for i in range(6)]

    w_pad = jnp.zeros((d, _PROJ_CUTS[-1]), F32).at[:, :w_in.shape[1]].set(w_in).astype(BF16)
    cw = _compress_weights(cmp_pe, cmp_w1, cmp_b1, cmp_w2, cmp_b2)
    w_out_bf = w_out.astype(BF16)

    proj_p = _inproj(x_p, mod_p[1], mod_p[0], g_mix_pre, w_pad)
    proj_s = _inproj(x_s.reshape(n_s, d), mod_s[1], mod_s[0], g_mix_pre, w_pad)

    o_p = _prompt_mixer(proj_p, cw)
    win_t = win_state.transpose(0, 2, 3, 1).reshape(bs, 2 * G_NSA * D_HEAD, WINDOW)
    o_s = _sample_mixer(proj_s, cache_a, cache_b, win_t, page_table, cw, bs, ts)

    x1_p, h2_p = _post(x_p, *o_p, proj_p[5], mod_p[2], mod_p[4], mod_p[3], g_out_moba, g_out_nsa, g_mix_post,
                       g_ffn_pre, w_out_bf)
    x1_s, h2_s = _post(x_s.reshape(n_s, d), *o_s, proj_s[5], mod_s[2], mod_s[4], mod_s[3], g_out_moba, g_out_nsa,
                       g_mix_post, g_ffn_pre, w_out_bf)

    n_valid = t + n_s
    n_all = -(-n_valid // ROUTER_TILE) * ROUTER_TILE
    pad = lambda a: jnp.concatenate([a, jnp.zeros((n_all - n_valid, d), F32)], axis=0) if n_all > n_valid else a
    h_all = pad(jnp.concatenate([h2_p, h2_s], axis=0))
    x1_all = pad(jnp.concatenate([x1_p, x1_s], axis=0))
    g2_tab = jnp.concatenate([jnp.broadcast_to(mod_p[5], (128, d)), mod_s[5],
                              jnp.zeros((n_all - n_valid, d), F32)], axis=0)
    y_all = _moe(h_all, x1_all, g2_tab, t, n_valid, w_router, b_router, w_exp_up, w_exp_down, w_sh_up, w_sh_down,
                 g_ffn_post)
    y_p = y_all[:t]
    y_s = y_all[t:n_valid].reshape(bs, ts, d)

    _, kva_p, _, kvb_p, kvw_p, _ = proj_p
    _, kva_s, _, kvb_s, kvw_s, _ = proj_s
    state_p = (kva_p.reshape(1, t, 2 * H_MOBA, D_HEAD), kvb_p.reshape(1, t, 4 * G_NSA, D_HEAD),
               kvw_p[t - min(WINDOW, t):].reshape(1, min(WINDOW, t), 2 * G_NSA, D_HEAD))
    win_new = jnp.concatenate([win_state, kvw_s.reshape(bs, ts, 2 * G_NSA, D_HEAD)], axis=1)[:, -WINDOW:]
    state_s = (kva_s.reshape(bs, ts, 2 * H_MOBA, D_HEAD), kvb_s.reshape(bs, ts, 4 * G_NSA, D_HEAD), win_new)
    return y_p, y_s, state_p, state_s


def kernel(x_prompt, x_sample, cache_moba, cache_nsa, state_nsa_win, page_table, c_prompt, c_sample, w_ada, b_ada, g_mix_pre, g_mix_post, g_ffn_pre, g_ffn_post, w_in, g_out_moba, g_out_nsa, w_out, cmp_pe, cmp_w1, cmp_b1, cmp_w2, cmp_b2, w_router, b_router, w_exp_up, w_exp_down, w_sh_up, w_sh_down):
    weights = (w_ada, b_ada, g_mix_pre, g_mix_post, g_ffn_pre, g_ffn_post, w_in, g_out_moba, g_out_nsa, w_out,
               cmp_pe, cmp_w1, cmp_b1, cmp_w2, cmp_b2, w_router, b_router, w_exp_up, w_exp_down, w_sh_up, w_sh_down)
    depth = w_ada.shape[0]
    n_pool = cache_moba.shape[1]
    y_p, y_s = x_prompt[0], x_sample
    st_p, st_s = [], []
    for layer in range(depth):
        w_l = tuple(w[layer] for w in weights)
        cache_a = cache_moba[layer].transpose(0, 2, 3, 1).reshape(n_pool, 2 * H_MOBA * D_HEAD, PAGE_SIZE)
        cache_b = cache_nsa[layer].transpose(0, 2, 3, 1).reshape(n_pool, 4 * G_NSA * D_HEAD, PAGE_SIZE)
        y_p, y_s, sp, ss = _layer(y_p, y_s, cache_a, cache_b, state_nsa_win[layer], page_table, c_prompt, c_sample, w_l)
        st_p.append(sp)
        st_s.append(ss)
    stack = lambda sts, i: jnp.stack([s[i] for s in sts])
    return (y_p[None], y_s, stack(st_p, 0), stack(st_p, 1), stack(st_p, 2), stack(st_s, 0), stack(st_s, 1), stack(st_s, 2))
```
